```python
import math
import jax, jax.numpy as jnp
from jax import lax
import numpy as np

D_MODEL = 1024
BATCH = 2
SEQ = 8192
DEPTH = 4

HEAD_DIM = 64
MOBA_HEADS = 4
MOBA_BLOCK = 256
MOBA_TOPK = 3
MOBA_QBLOCK = 128
MOBA_W = MOBA_HEADS * HEAD_DIM
SSD_HEADS = 8
SSD_HEAD_DIM = 64
SSD_INNER = SSD_HEADS * SSD_HEAD_DIM
SSD_GROUPS = 2
SSD_STATE = 128
SSD_CONV = 4
SSD_CHUNK = 256
SSD_CONV_CH = SSD_INNER + 2 * SSD_GROUPS * SSD_STATE
FOX_HEADS = 4
FOX_QBLOCK = 128
FOX_W = FOX_HEADS * HEAD_DIM
D_MIX = MOBA_W + SSD_INNER + FOX_W
IN_SPLITS = (3 * MOBA_W, SSD_INNER, SSD_CONV_CH, SSD_HEADS, 3 * FOX_W, FOX_HEADS)
D_IN = 3 * MOBA_W + SSD_INNER + SSD_CONV_CH + SSD_HEADS + 3 * FOX_W + FOX_HEADS
D_FF = 4 * D_MODEL
PLE_DIM = 256
RMS_EPS = 1e-6
NEG = -1e30

kernel_name = "hymba_moba_ssd_fox_trunk"


def rms_norm(x, g):
    xf = x.astype(jnp.float32)
    y = xf * lax.rsqrt(jnp.mean(xf * xf, axis=-1, keepdims=True) + RMS_EPS)
    return (y * g.astype(jnp.float32)).astype(x.dtype)


def pad_seq(a, mult):
    s = a.shape[1]
    sp = -(-s // mult) * mult
    if sp == s:
        return a
    widths = [(0, 0)] * a.ndim
    widths[1] = (0, sp - s)
    return jnp.pad(a, widths)


def split_cols(a, sizes):
    outs, start = [], 0
    for s in sizes:
        outs.append(a[..., start:start + s])
        start += s
    return outs


def alibi_slopes(n):
    return jnp.asarray([2.0 ** (-8.0 * (h + 1) / n) for h in range(n)], jnp.float32)


def moba_attention(q, k, v):
    B, S, H, Dh = q.shape
    out_dtype = v.dtype
    q, k, v = (pad_seq(a, MOBA_BLOCK) for a in (q, k, v))
    Sp = q.shape[1]
    nb = Sp // MOBA_BLOCK
    nq = Sp // MOBA_QBLOCK
    ksel = min(MOBA_TOPK, nb)
    scale = Dh ** -0.5
    slopes = alibi_slopes(H)
    qf = q.astype(jnp.float32).transpose(0, 2, 1, 3)
    kb = k.astype(jnp.float32).transpose(0, 2, 1, 3).reshape(B, H, nb, MOBA_BLOCK, Dh)
    vb = v.astype(jnp.float32).transpose(0, 2, 1, 3).reshape(B, H, nb, MOBA_BLOCK, Dh)
    kmean = kb.mean(axis=3)
    q_blocks = qf.reshape(B, H, nq, MOBA_QBLOCK, Dh).transpose(2, 0, 1, 3, 4)
    bi = jnp.arange(B)[:, None, None, None]
    hi = jnp.arange(H)[None, :, None, None]
    offs = jnp.arange(MOBA_BLOCK)
    blk_ids = jnp.arange(nb)

    def one_block(args):
        i, qblk = args
        t = i * MOBA_QBLOCK + jnp.arange(MOBA_QBLOCK)
        own = (i * MOBA_QBLOCK) // MOBA_BLOCK
        blk = jnp.einsum('bhqd,bhnd->bhqn', qblk, kmean)
        blk = jnp.where(blk_ids < own, blk, NEG)
        _, top_i = lax.top_k(blk, ksel)
        valid = top_i < own
        k_sel = kb[bi, hi, top_i]
        v_sel = vb[bi, hi, top_i]
        pos_sel = top_i[..., None] * MOBA_BLOCK + offs
        dist_sel = (t[:, None, None] - pos_sel).astype(jnp.float32)
        s_sel = (jnp.einsum('bhqd,bhqnkd->bhqnk', qblk, k_sel) * scale
                 - slopes[:, None, None, None] * dist_sel)
        s_sel = jnp.where(valid[..., None], s_sel, NEG)
        k_own = lax.dynamic_index_in_dim(kb, own, axis=2, keepdims=False)
        v_own = lax.dynamic_index_in_dim(vb, own, axis=2, keepdims=False)
        dist_own = t[:, None] - (own * MOBA_BLOCK + offs)[None, :]
        s_own = (jnp.einsum('bhqd,bhkd->bhqk', qblk, k_own) * scale
                 - slopes[:, None, None] * dist_own.astype(jnp.float32))
        s_own = jnp.where(dist_own >= 0, s_own, NEG)
        nsel = ksel * MOBA_BLOCK
        logits = jnp.concatenate([s_sel.reshape(B, H, MOBA_QBLOCK, nsel), s_own], axis=-1)
        probs = jax.nn.softmax(logits, axis=-1)
        p_sel = probs[..., :nsel].reshape(B, H, MOBA_QBLOCK, ksel, MOBA_BLOCK)
        p_own = probs[..., nsel:]
        return (jnp.einsum('bhqnk,bhqnkd->bhqd', p_sel, v_sel)
                + jnp.einsum('bhqk,bhkd->bhqd', p_own, v_own))

    out = lax.map(one_block, (jnp.arange(nq), q_blocks))
    out = out.transpose(1, 0, 3, 2, 4).reshape(B, Sp, H * Dh)[:, :S]
    return out.astype(out_dtype)


def forgetting_attention(q, k, v, log_f):
    B, S, H, Dh = q.shape
    out_dtype = v.dtype
    cum = jnp.cumsum(log_f.astype(jnp.float32), axis=1)
    q, k, v, cum = (pad_seq(a, FOX_QBLOCK) for a in (q, k, v, cum))
    Sp = q.shape[1]
    nq = Sp // FOX_QBLOCK
    scale = Dh ** -0.5
    qf = q.astype(jnp.float32).transpose(0, 2, 1, 3)
    kf = k.astype(jnp.float32).transpose(0, 2, 1, 3)
    vf = v.astype(jnp.float32).transpose(0, 2, 1, 3)
    c_t = cum.transpose(0, 2, 1)
    q_blocks = qf.reshape(B, H, nq, FOX_QBLOCK, Dh).transpose(2, 0, 1, 3, 4)
    c_blocks = c_t.reshape(B, H, nq, FOX_QBLOCK).transpose(2, 0, 1, 3)
    kpos = jnp.arange(Sp)

    def one_block(args):
        i, qblk, cq = args
        t = i * FOX_QBLOCK + jnp.arange(FOX_QBLOCK)
        s = (jnp.einsum('bhqd,bhkd->bhqk', qblk, kf) * scale
             + cq[..., None] - c_t[:, :, None, :])
        s = jnp.where(kpos[None, :] <= t[:, None], s, NEG)
        return jnp.einsum('bhqk,bhkd->bhqd', jax.nn.softmax(s, axis=-1), vf)

    out = lax.map(one_block, (jnp.arange(nq), q_blocks, c_blocks))
    out = out.transpose(1, 0, 3, 2, 4).reshape(B, Sp, H * Dh)[:, :S]
    return out.astype(out_dtype)


def causal_depthwise_conv(u, w, b):
    c = u.shape[-1]
    out = lax.conv_general_dilated(u, w[:, None, :], window_strides=(1,),
                                   padding=((SSD_CONV - 1, 0),),
                                   dimension_numbers=('NWC', 'WIO', 'NWC'),
                                   feature_group_count=c)
    return out + b


def ssd_scan(x, dt, A, Bm, Cm):
    Bsz, S, H, P = x.shape
    G, N = Bm.shape[2], Bm.shape[3]
    Bm = jnp.repeat(Bm, H // G, axis=2)
    Cm = jnp.repeat(Cm, H // G, axis=2)
    x, dt, Bm, Cm = (pad_seq(a, SSD_CHUNK) for a in (x, dt, Bm, Cm))
    Sp = x.shape[1]
    nc = Sp // SSD_CHUNK
    L = SSD_CHUNK
    X = (x * dt[..., None]).reshape(Bsz, nc, L, H, P)
    Bc = Bm.reshape(Bsz, nc, L, H, N)
    Cc = Cm.reshape(Bsz, nc, L, H, N)
    a = (dt * A).reshape(Bsz, nc, L, H).transpose(0, 3, 1, 2)
    a_cs = jnp.cumsum(a, axis=-1)
    seg = a_cs[..., :, None] - a_cs[..., None, :]
    causal = jnp.tril(jnp.ones((L, L), dtype=bool))
    decay = jnp.exp(jnp.where(causal, seg, -jnp.inf))
    scores = jnp.einsum('bclhn,bcshn->bhcls', Cc, Bc) * decay
    y_diag = jnp.einsum('bhcls,bcshp->bclhp', scores, X)
    decay_to_end = jnp.exp(a_cs[..., -1:] - a_cs)
    states = jnp.einsum('bclhn,bhcl,bclhp->bchpn', Bc, decay_to_end, X)
    chunk_decay = jnp.exp(a_cs[..., -1])

    def step(h, inp):
        st, dec = inp
        return dec[..., None, None] * h + st, h

    h0 = jnp.zeros((Bsz, H, P, N), jnp.float32)
    _, h_in = lax.scan(step, h0, (states.transpose(1, 0, 2, 3, 4), chunk_decay.transpose(2, 0, 1)))
    y_off = jnp.einsum('bclhn,cbhpn,bhcl->bclhp', Cc, h_in, jnp.exp(a_cs))
    return (y_diag + y_off).reshape(Bsz, Sp, H, P)[:, :S]


def mamba2_mixer(z, xbc, dt_raw, conv_w, conv_b, dt_bias, a_log, d_skip, norm_w):
    Bsz, S, _ = z.shape
    xbc = jax.nn.silu(causal_depthwise_conv(xbc, conv_w, conv_b))
    xs, bm, cm = split_cols(xbc, (SSD_INNER, SSD_GROUPS * SSD_STATE, SSD_GROUPS * SSD_STATE))
    xh = xs.reshape(Bsz, S, SSD_HEADS, SSD_HEAD_DIM).astype(jnp.float32)
    bm = bm.reshape(Bsz, S, SSD_GROUPS, SSD_STATE).astype(jnp.float32)
    cm = cm.reshape(Bsz, S, SSD_GROUPS, SSD_STATE).astype(jnp.float32)
    dt = jax.nn.softplus(dt_raw.astype(jnp.float32) + dt_bias.astype(jnp.float32))
    A = -jnp.exp(a_log.astype(jnp.float32))
    y = ssd_scan(xh, dt, A, bm, cm) + d_skip.astype(jnp.float32)[:, None] * xh
    y = y.reshape(Bsz, S, SSD_INNER) * jax.nn.silu(z.astype(jnp.float32))
    return rms_norm(y, norm_w).astype(z.dtype)


def hybrid_mixer(h, w_in, conv_w, conv_b, dt_bias, a_log, d_skip, ssd_norm_w, fox_f_bias, w_out):
    Bsz, S, _ = h.shape
    proj = h @ w_in
    moba_qkv, z, xbc, dt_raw, fox_qkv, f_raw = split_cols(proj, IN_SPLITS)
    mq, mk, mv = (a.reshape(Bsz, S, MOBA_HEADS, HEAD_DIM) for a in jnp.split(moba_qkv, 3, axis=-1))
    out_a = moba_attention(mq, mk, mv).astype(h.dtype)
    out_b = mamba2_mixer(z, xbc, dt_raw, conv_w, conv_b, dt_bias, a_log, d_skip, ssd_norm_w).astype(h.dtype)
    fq, fk, fv = (a.reshape(Bsz, S, FOX_HEADS, HEAD_DIM) for a in jnp.split(fox_qkv, 3, axis=-1))
    log_f = jax.nn.log_sigmoid(f_raw.astype(jnp.float32) + fox_f_bias.astype(jnp.float32))
    out_c = forgetting_attention(fq, fk, fv, log_f).astype(h.dtype)
    mixed = jnp.concatenate([out_a, out_b, out_c], axis=-1)
    return mixed @ w_out


def setup_inputs(seed: int = 0) -> dict:
    key = jax.random.key(seed)
    ks = jax.random.split(key, 20)
    f32 = jnp.float32

    def nrm(k, shape, scale):
        return jax.random.normal(k, shape, f32) * scale

    x = nrm(ks[0], (BATCH, SEQ, D_MODEL), 1.0)
    p = nrm(ks[1], (DEPTH, BATCH, SEQ, PLE_DIM), 1.0)
    g_mix = 1.0 + nrm(ks[2], (DEPTH, D_MODEL), 0.02)
    w_in = nrm(ks[3], (DEPTH, D_MODEL, D_IN), D_MODEL ** -0.5)
    conv_w = nrm(ks[4], (DEPTH, SSD_CONV, SSD_CONV_CH), SSD_CONV ** -0.5)
    conv_b = nrm(ks[5], (DEPTH, SSD_CONV_CH), 0.02)
    dt0 = jnp.exp(jax.random.uniform(ks[6], (DEPTH, SSD_HEADS), f32, math.log(1e-3), math.log(1e-1)))
    dt_bias = dt0 + jnp.log(-jnp.expm1(-dt0))
    a_log = jnp.log(jax.random.uniform(ks[7], (DEPTH, SSD_HEADS), f32, 1.0, 16.0))
    d_skip = 1.0 + nrm(ks[8], (DEPTH, SSD_HEADS), 0.1)
    ssd_norm_w = 1.0 + nrm(ks[9], (DEPTH, SSD_INNER), 0.02)
    fox_f_bias = 2.0 + nrm(ks[10], (DEPTH, FOX_HEADS), 0.5)
    w_out = nrm(ks[11], (DEPTH, D_MIX, D_MODEL), D_MIX ** -0.5)
    g_mlp = 1.0 + nrm(ks[12], (DEPTH, D_MODEL), 0.02)
    w_ff1 = nrm(ks[13], (DEPTH, D_MODEL, D_FF), D_MODEL ** -0.5)
    w_ff2 = nrm(ks[14], (DEPTH, D_FF, D_MODEL), D_FF ** -0.5)
    g_ple = 1.0 + nrm(ks[15], (DEPTH, D_MODEL), 0.02)
    w_ple_gate = nrm(ks[16], (DEPTH, D_MODEL, D_MODEL), D_MODEL ** -0.5)
    w_ple_proj = nrm(ks[17], (DEPTH, PLE_DIM, D_MODEL), PLE_DIM ** -0.5)
    g_final = 1.0 + nrm(ks[18], (D_MODEL,), 0.02)
    return {"x": x, "p": p, "g_mix": g_mix, "w_in": w_in, "conv_w": conv_w, "conv_b": conv_b,
            "dt_bias": dt_bias, "a_log": a_log, "d_skip": d_skip, "ssd_norm_w": ssd_norm_w,
            "fox_f_bias": fox_f_bias, "w_out": w_out, "g_mlp": g_mlp, "w_ff1": w_ff1,
            "w_ff2": w_ff2, "g_ple": g_ple, "w_ple_gate": w_ple_gate, "w_ple_proj": w_ple_proj,
            "g_final": g_final}


def reference(x, p, g_mix, w_in, conv_w, conv_b, dt_bias, a_log, d_skip, ssd_norm_w,
              fox_f_bias, w_out, g_mlp, w_ff1, w_ff2, g_ple, w_ple_gate, w_ple_proj, g_final):
    for i in range(DEPTH):
        h = rms_norm(x, g_mix[i])
        x = x + hybrid_mixer(h, w_in[i], conv_w[i], conv_b[i], dt_bias[i], a_log[i], d_skip[i],
                             ssd_norm_w[i], fox_f_bias[i], w_out[i])
        h = rms_norm(x, g_mlp[i])
        x = x + jnp.square(jax.nn.relu(h @ w_ff1[i])) @ w_ff2[i]
        gate = jax.nn.sigmoid(rms_norm(x, g_ple[i]) @ w_ple_gate[i])
        x = x + gate * (p[i] @ w_ple_proj[i])
    return rms_norm(x, g_final)
```

```python
import functools

import jax
import jax.numpy as jnp
from jax import lax
from jax.experimental import pallas as pl
from jax.experimental.pallas import tpu as pltpu

F32 = jnp.float32
BF16 = jnp.bfloat16

HEAD_DIM = 64
ATT_HEADS = 4
ATT_W = ATT_HEADS * HEAD_DIM
PAIR_W = 2 * HEAD_DIM
KV_BLOCK = 256
MOBA_TOPK = 3
SSD_HEADS = 8
SSD_P = 64
SSD_INNER = SSD_HEADS * SSD_P
SSD_GROUPS = 2
SSD_N = 128
SSD_CONV = 4
SSD_CHUNK = 256
SSD_XBC = SSD_INNER + 2 * SSD_GROUPS * SSD_N
SMALL_W = 128
SMALL_T = 16
PLE_DIM = 256
RMS_EPS = 1e-6
NEG = -1e30
LANES = 128
VMEM_LIMIT = 56 * 1024 * 1024

ROW_TILE = 512
POST_ROW_TILE = 1024
FF_CHUNK = 512


def _rms(x, g):
    return x * lax.rsqrt(jnp.mean(x * x, axis=-1, keepdims=True) + RMS_EPS) * g


def _dot(a, b):
    return jnp.dot(a, b, preferred_element_type=F32)


def _dot_nt(a, b):
    return lax.dot_general(a, b, (((1,), (1,)), ((), ())), preferred_element_type=F32)


def _dot_tn(a, b):
    return lax.dot_general(a, b, (((0,), (0,)), ((), ())), preferred_element_type=F32)


def _split3(a):
    hi = a.astype(BF16)
    r1 = a - hi.astype(F32)
    mid = r1.astype(BF16)
    lo = (r1 - mid.astype(F32)).astype(BF16)
    return hi, mid, lo


def _softplus(x):
    return jnp.maximum(x, 0.0) + jnp.log(1.0 + jnp.exp(-jnp.abs(x)))


def _silu(x):
    return x * (1.0 / (1.0 + jnp.exp(-x)))


def _inproj_kernel(x_ref, g_ref, wq_ref, wv_ref, wz_ref, wxbc_ref, wsm_ref, wkt_ref, wst_ref,
                   q_ref, v_ref, z_ref, xbc_ref, sm_ref, kt_ref, st_ref):
    h = _rms(x_ref[...], g_ref[...]).astype(BF16)
    q_ref[...] = (_dot(h, wq_ref[...]) * (HEAD_DIM ** -0.5)).astype(BF16)
    v_ref[...] = _dot(h, wv_ref[...]).astype(BF16)
    z_ref[...] = _dot(h, wz_ref[...])
    xbc_ref[...] = _dot(h, wxbc_ref[...])
    sm_ref[...] = _dot(h, wsm_ref[...])
    kt_ref[...] = _dot_nt(wkt_ref[...], h).astype(BF16)
    st_ref[...] = _dot_nt(wst_ref[...], h)


def _inproj(x, g, w):
    t, d = x.shape
    tm = ROW_TILE
    row = lambda i: (i, 0)
    col = lambda i: (0, i)
    fixed = lambda i: (0, 0)
    wspec = lambda a: pl.BlockSpec(a.shape, fixed)
    outs = (
        jax.ShapeDtypeStruct((t, 2 * ATT_W), BF16),
        jax.ShapeDtypeStruct((t, 2 * ATT_W), BF16),
        jax.ShapeDtypeStruct((t, SSD_INNER), F32),
        jax.ShapeDtypeStruct((t, SSD_XBC), F32),
        jax.ShapeDtypeStruct((t, SMALL_W), F32),
        jax.ShapeDtypeStruct((2 * ATT_W, t), BF16),
        jax.ShapeDtypeStruct((SMALL_T, t), F32),
    )
    return pl.pallas_call(
        _inproj_kernel,
        out_shape=outs,
        grid=(t // tm,),
        in_specs=[pl.BlockSpec((tm, d), row), wspec(g), wspec(w["q"]), wspec(w["v"]), wspec(w["z"]),
                  wspec(w["xbc"]), wspec(w["sm"]), wspec(w["kt"]), wspec(w["st"])],
        out_specs=(pl.BlockSpec((tm, 2 * ATT_W), row), pl.BlockSpec((tm, 2 * ATT_W), row),
                   pl.BlockSpec((tm, SSD_INNER), row), pl.BlockSpec((tm, SSD_XBC), row),
                   pl.BlockSpec((tm, SMALL_W), row), pl.BlockSpec((2 * ATT_W, tm), col),
                   pl.BlockSpec((SMALL_T, tm), col)),
        compiler_params=pltpu.CompilerParams(dimension_semantics=("arbitrary",), vmem_limit_bytes=VMEM_LIMIT),
        name="inproj",
    )(x, g, w["q"], w["v"], w["z"], w["xbc"], w["sm"], w["kt"], w["st"])


def _flash_init(m_sc, l_sc, acc_sc):
    m_sc[...] = jnp.full(m_sc.shape, NEG, F32)
    l_sc[...] = jnp.zeros(l_sc.shape, F32)
    acc_sc[...] = jnp.zeros(acc_sc.shape, F32)


def _flash_update(h, s, v_pair, m_sc, l_sc, acc_sc):
    m_prev = m_sc[h]
    m_new = jnp.maximum(m_prev, jnp.max(s, axis=1, keepdims=True))
    alpha = jnp.exp(m_prev - m_new)
    p = jnp.exp(s - jnp.concatenate([m_new] * (KV_BLOCK // LANES), axis=1))
    l_sc[h] = alpha * l_sc[h] + jnp.sum(p, axis=1, keepdims=True)
    acc_sc[h] = alpha * acc_sc[h] + _dot(p.astype(BF16), v_pair)
    m_sc[h] = m_new


def _flash_finish(o_ref, l_sc, acc_sc):
    lane = lax.broadcasted_iota(jnp.int32, (KV_BLOCK, PAIR_W), 1)
    for j in range(ATT_HEADS // 2):
        even = acc_sc[2 * j] / l_sc[2 * j]
        odd = acc_sc[2 * j + 1] / l_sc[2 * j + 1]
        o_ref[:, j * PAIR_W:(j + 1) * PAIR_W] = jnp.where(lane < HEAD_DIM, even, odd).astype(o_ref.dtype)


def _pair_lanes(h):
    lane = lax.broadcasted_iota(jnp.int32, (KV_BLOCK, PAIR_W), 1)
    own = (lane < HEAD_DIM) if h % 2 == 0 else (lane >= HEAD_DIM)
    return lane, own


def _moba_kernel(q_ref, kt_ref, v_ref, o_ref, kaug_sc, km_sc, qaug_sc, m_sc, l_sc, acc_sc, *, nb):
    i = pl.program_id(1)
    slopes = [2.0 ** (-8.0 * (h + 1) / ATT_HEADS) for h in range(ATT_HEADS)]

    @pl.when(i == 0)
    def _():
        km_sc[...] = jnp.zeros(km_sc.shape, F32)
        r64 = lax.broadcasted_iota(jnp.int32, (HEAD_DIM, KV_BLOCK), 0)
        rows = lax.broadcasted_iota(jnp.int32, (PAIR_W, LANES), 0)
        cols = lax.broadcasted_iota(jnp.int32, (PAIR_W, LANES), 1)

        def build(n, carry):
            c0 = pl.multiple_of(n * KV_BLOCK, KV_BLOCK)
            onehot = (r64 == n).astype(BF16)
            for h in range(ATT_HEADS):
                kh = kt_ref[h * HEAD_DIM:(h + 1) * HEAD_DIM, pl.ds(c0, KV_BLOCK)]
                k_at, oh_at = (0, HEAD_DIM) if h % 2 == 0 else (HEAD_DIM, 0)
                kaug_sc[h * PAIR_W + k_at:h * PAIR_W + k_at + HEAD_DIM, pl.ds(c0, KV_BLOCK)] = kh
                kaug_sc[h * PAIR_W + oh_at:h * PAIR_W + oh_at + HEAD_DIM, pl.ds(c0, KV_BLOCK)] = onehot
            for j in range(ATT_HEADS // 2):
                kp = kt_ref[j * PAIR_W:(j + 1) * PAIR_W, pl.ds(c0, KV_BLOCK)].astype(F32)
                mean = jnp.sum(kp, axis=1, keepdims=True) * (1.0 / KV_BLOCK)
                for h in (2 * j, 2 * j + 1):
                    out_lane = (HEAD_DIM + n) if h % 2 == 0 else n
                    own_rows = (rows < HEAD_DIM) if h % 2 == 0 else (rows >= HEAD_DIM)
                    km_sc[h] = jnp.where(own_rows & (cols == out_lane), mean, km_sc[h])
            return carry

        lax.fori_loop(0, nb, build, 0)

    _flash_init(m_sc, l_sc, acc_sc)

    for h in range(ATT_HEADS):
        j = h // 2
        lane, own = _pair_lanes(h)
        qp = q_ref[:, j * PAIR_W:(j + 1) * PAIR_W]
        km_hi, km_mid, km_lo = _split3(km_sc[h])
        blk = _dot(qp, km_hi) + _dot(qp, km_mid) + _dot(qp, km_lo)
        blk_id = jnp.where(own, -1, lane - (HEAD_DIM if h % 2 == 0 else 0))
        blk_idf = blk_id.astype(F32)
        score = jnp.where((blk_id >= 0) & (blk_id < i), blk, NEG)
        sel = blk_id == i
        for _ in range(MOBA_TOPK):
            top = jnp.max(score, axis=1, keepdims=True)
            first = jnp.min(jnp.where(score == top, blk_idf, 1e9), axis=1, keepdims=True)
            hit = blk_idf == first
            sel = sel | (hit & (top > 0.5 * NEG))
            score = jnp.where(hit, 3.0 * NEG, score)
        bias = jnp.where(sel, 0.0, NEG).astype(BF16)
        qaug_sc[h] = jnp.where(own, qp, bias)

    def scores(h, n, c0):
        s = _dot(qaug_sc[h], kaug_sc[h * PAIR_W:(h + 1) * PAIR_W, pl.ds(c0, KV_BLOCK)])
        rel = lax.broadcasted_iota(jnp.int32, (1, KV_BLOCK), 1) + (n - i) * KV_BLOCK
        return s + slopes[h] * rel.astype(F32)

    c_own = pl.multiple_of(i * KV_BLOCK, KV_BLOCK)
    qi = lax.broadcasted_iota(jnp.int32, (KV_BLOCK, KV_BLOCK), 0)
    ki = lax.broadcasted_iota(jnp.int32, (KV_BLOCK, KV_BLOCK), 1)
    for h in range(ATT_HEADS):
        j = h // 2
        s = jnp.where(ki <= qi, scores(h, i, c_own), NEG)
        _flash_update(h, s, v_ref[pl.ds(c_own, KV_BLOCK), j * PAIR_W:(j + 1) * PAIR_W], m_sc, l_sc, acc_sc)

    def past(n, carry):
        c0 = pl.multiple_of(n * KV_BLOCK, KV_BLOCK)
        for h in range(ATT_HEADS):
            j = h // 2
            _flash_update(h, scores(h, n, c0), v_ref[pl.ds(c0, KV_BLOCK), j * PAIR_W:(j + 1) * PAIR_W],
                          m_sc, l_sc, acc_sc)
        return carry

    lax.fori_loop(0, i, past, 0)
    _flash_finish(o_ref, l_sc, acc_sc)


def _moba(q, kt, v, batch, seq):
    nb = seq // KV_BLOCK
    return pl.pallas_call(
        functools.partial(_moba_kernel, nb=nb),
        out_shape=jax.ShapeDtypeStruct((batch * seq, ATT_W), BF16),
        grid=(batch, nb),
        in_specs=[pl.BlockSpec((KV_BLOCK, ATT_W), lambda b, i: (b * nb + i, 0)),
                  pl.BlockSpec((ATT_W, seq), lambda b, i: (0, b)),
                  pl.BlockSpec((seq, ATT_W), lambda b, i: (b, 0))],
        out_specs=pl.BlockSpec((KV_BLOCK, ATT_W), lambda b, i: (b * nb + i, 0)),
        scratch_shapes=[pltpu.VMEM((ATT_HEADS * PAIR_W, seq), BF16),
                        pltpu.VMEM((ATT_HEADS, PAIR_W, LANES), F32),
                        pltpu.VMEM((ATT_HEADS, KV_BLOCK, PAIR_W), BF16),
                        pltpu.VMEM((ATT_HEADS, KV_BLOCK, LANES), F32),
                        pltpu.VMEM((ATT_HEADS, KV_BLOCK, LANES), F32),
                        pltpu.VMEM((ATT_HEADS, KV_BLOCK, PAIR_W), F32)],
        compiler_params=pltpu.CompilerParams(dimension_semantics=("arbitrary", "arbitrary"),
                                             vmem_limit_bytes=VMEM_LIMIT),
        name="moba",
    )(q, kt, v)


def _fox_kernel(q_ref, kt_ref, v_ref, ft_ref, fb_ref, o_ref, c_sc, m_sc, l_sc, acc_sc, *, nb):
    i = pl.program_id(1)

    @pl.when(i == 0)
    def _():
        r = lax.broadcasted_iota(jnp.int32, (KV_BLOCK, KV_BLOCK), 0)
        c = lax.broadcasted_iota(jnp.int32, (KV_BLOCK, KV_BLOCK), 1)
        triu = (r <= c).astype(BF16)

        def scan(n, carry):
            c0 = pl.multiple_of(n * KV_BLOCK, KV_BLOCK)
            logf = -_softplus(-(ft_ref[:, pl.ds(c0, KV_BLOCK)] + fb_ref[...]))
            hi, mid, lo = _split3(logf)
            cs = _dot(hi, triu) + _dot(mid, triu) + _dot(lo, triu) + carry
            c_sc[:, pl.ds(c0, KV_BLOCK)] = cs
            return cs[:, KV_BLOCK - 1:KV_BLOCK]

        lax.fori_loop(0, nb, scan, jnp.zeros((SMALL_T, 1), F32))

    _flash_init(m_sc, l_sc, acc_sc)
    c_own = pl.multiple_of(i * KV_BLOCK, KV_BLOCK)
    c_first = c_sc[:, pl.ds(c_own, KV_BLOCK)][:, 0:1]

    def scores(h, c0):
        j = h // 2
        _, own = _pair_lanes(h)
        qp = q_ref[:, j * PAIR_W:(j + 1) * PAIR_W]
        qz = jnp.where(own, qp, jnp.zeros_like(qp))
        s = _dot(qz, kt_ref[j * PAIR_W:(j + 1) * PAIR_W, pl.ds(c0, KV_BLOCK)])
        f_row = SSD_HEADS + h
        return s + (c_first[f_row:f_row + 1, :] - c_sc[f_row:f_row + 1, pl.ds(c0, KV_BLOCK)])

    qi = lax.broadcasted_iota(jnp.int32, (KV_BLOCK, KV_BLOCK), 0)
    ki = lax.broadcasted_iota(jnp.int32, (KV_BLOCK, KV_BLOCK), 1)
    for h in range(ATT_HEADS):
        j = h // 2
        s = jnp.where(ki <= qi, scores(h, c_own), NEG)
        _flash_update(h, s, v_ref[pl.ds(c_own, KV_BLOCK), j * PAIR_W:(j + 1) * PAIR_W], m_sc, l_sc, acc_sc)

    def past(n, carry):
        c0 = pl.multiple_of(n * KV_BLOCK, KV_BLOCK)
        for h in range(ATT_HEADS):
            j = h // 2
            _flash_update(h, scores(h, c0), v_ref[pl.ds(c0, KV_BLOCK), j * PAIR_W:(j + 1) * PAIR_W],
                          m_sc, l_sc, acc_sc)
        return carry

    lax.fori_loop(0, i, past, 0)
    _flash_finish(o_ref, l_sc, acc_sc)


def _fox(q, kt, v, ft, fb, batch, seq):
    nb = seq // KV_BLOCK
    return pl.pallas_call(
        functools.partial(_fox_kernel, nb=nb),
        out_shape=jax.ShapeDtypeStruct((batch * seq, ATT_W), BF16),
        grid=(batch, nb),
        in_specs=[pl.BlockSpec((KV_BLOCK, ATT_W), lambda b, i: (b * nb + i, 1)),
                  pl.BlockSpec((ATT_W, seq), lambda b, i: (1, b)),
                  pl.BlockSpec((seq, ATT_W), lambda b, i: (b, 1)),
                  pl.BlockSpec((SMALL_T, seq), lambda b, i: (0, b)),
                  pl.BlockSpec((SMALL_T, 1), lambda b, i: (0, 0))],
        out_specs=pl.BlockSpec((KV_BLOCK, ATT_W), lambda b, i: (b * nb + i, 0)),
        scratch_shapes=[pltpu.VMEM((SMALL_T, seq), F32),
                        pltpu.VMEM((ATT_HEADS, KV_BLOCK, LANES), F32),
                        pltpu.VMEM((ATT_HEADS, KV_BLOCK, LANES), F32),
                        pltpu.VMEM((ATT_HEADS, KV_BLOCK, PAIR_W), F32)],
        compiler_params=pltpu.CompilerParams(dimension_semantics=("arbitrary", "arbitrary"),
                                             vmem_limit_bytes=VMEM_LIMIT),
        name="fox",
    )(q, kt, v, ft, fb)


def _ssd_kernel(xbc_ref, z_ref, sm_ref, cw_ref, cb_ref, dtb_ref, alog_ref, dskip_ref, nw_ref, o_ref,
                ext_sc, state_sc, y_sc):
    c = pl.program_id(1)
    L = SSD_CHUNK
    tail = 8

    @pl.when(c == 0)
    def _():
        ext_sc[0:tail, :] = jnp.zeros((tail, SSD_XBC), F32)
        state_sc[...] = jnp.zeros(state_sc.shape, F32)

    ext_sc[tail:tail + L, :] = xbc_ref[...]
    u = cb_ref[...]
    for k in range(SSD_CONV):
        back = SSD_CONV - 1 - k
        u = u + cw_ref[k:k + 1, :] * ext_sc[tail - back:tail - back + L, :]
    ext_sc[0:tail, :] = ext_sc[L:L + tail, :]
    xc = _silu(u)

    dt = _softplus(sm_ref[...] + dtb_ref[...])
    a = dt * (-jnp.exp(alog_ref[...]))
    r = lax.broadcasted_iota(jnp.int32, (L, L), 0)
    cidx = lax.broadcasted_iota(jnp.int32, (L, L), 1)
    causal = cidx <= r
    tril = causal.astype(BF16)
    a_hi, a_mid, a_lo = _split3(a)
    acs = _dot(tril, a_hi) + _dot(tril, a_mid) + _dot(tril, a_lo)
    acs_t = acs.T

    for g in range(SSD_GROUPS):
        bg = xc[:, SSD_INNER + g * SSD_N:SSD_INNER + (g + 1) * SSD_N].astype(BF16)
        cg = xc[:, SSD_INNER + (SSD_GROUPS + g) * SSD_N:SSD_INNER + (SSD_GROUPS + g + 1) * SSD_N].astype(BF16)
        cb = _dot_nt(cg, bg)
        for h in range(g * (SSD_HEADS // SSD_GROUPS), (g + 1) * (SSD_HEADS // SSD_GROUPS)):
            cols = slice(h * SSD_P, (h + 1) * SSD_P)
            acol = acs[:, h:h + 1]
            total = acs[L - 1:L, h:h + 1]
            xs = xc[:, cols]
            xh = xs * dt[:, h:h + 1]
            decay = jnp.exp(jnp.where(causal, acol - acs_t[h:h + 1, :], NEG))
            y_diag = _dot((cb * decay).astype(BF16), xh.astype(BF16))
            st = state_sc[:, cols]
            y_off = jnp.exp(acol) * _dot(cg, st.astype(BF16))
            new = _dot_tn(bg, (xh * jnp.exp(total - acol)).astype(BF16))
            state_sc[:, cols] = jnp.exp(total) * st + new
            y_sc[:, cols] = y_diag + y_off + dskip_ref[:, cols] * xs

    y = y_sc[...] * _silu(z_ref[...])
    o_ref[...] = _rms(y, nw_ref[...]).astype(o_ref.dtype)


def _ssd(xbc, z, sm, lw, batch, seq):
    nc = seq // SSD_CHUNK
    L = SSD_CHUNK
    row = lambda b, c: (b * nc + c, 0)
    fixed = lambda b, c: (0, 0)
    wspec = lambda a: pl.BlockSpec(a.shape, fixed)
    ws = (lw["conv_w"], lw["conv_b"], lw["dt_bias"], lw["a_log"], lw["d_skip"], lw["ssd_norm_w"])
    return pl.pallas_call(
        _ssd_kernel,
        out_shape=jax.ShapeDtypeStruct((batch * seq, SSD_INNER), BF16),
        grid=(batch, nc),
        in_specs=[pl.BlockSpec((L, SSD_XBC), row), pl.BlockSpec((L, SSD_INNER), row),
                  pl.BlockSpec((L, SMALL_W), row)] + [wspec(a) for a in ws],
        out_specs=pl.BlockSpec((L, SSD_INNER), row),
        scratch_shapes=[pltpu.VMEM((L + 8, SSD_XBC), F32),
                        pltpu.VMEM((SSD_N, SSD_INNER), F32),
                        pltpu.VMEM((L, SSD_INNER), F32)],
        compiler_params=pltpu.CompilerParams(dimension_semantics=("arbitrary", "arbitrary"),
                                             vmem_limit_bytes=VMEM_LIMIT),
        name="ssd",
    )(xbc, z, sm, *ws)


def _post_kernel(x_ref, a_ref, b_ref, c_ref, p_ref, wo_ref, gm_ref, w1_ref, w2_ref, gp_ref, wg_ref, wp_ref,
                 gf_ref, o_ref, acc_sc, h_sc, *, final):
    j = pl.program_id(1)

    @pl.when(j == 0)
    def _():
        wa = wo_ref[0:ATT_W, :]
        wb = wo_ref[ATT_W:ATT_W + SSD_INNER, :]
        wc = wo_ref[ATT_W + SSD_INNER:, :]
        x1 = x_ref[...] + _dot(a_ref[...], wa) + _dot(b_ref[...], wb) + _dot(c_ref[...], wc)
        acc_sc[...] = x1
        h_sc[...] = _rms(x1, gm_ref[...]).astype(BF16)

    u = jnp.maximum(_dot(h_sc[...], w1_ref[...]), 0.0)
    acc_sc[...] += _dot((u * u).astype(BF16), w2_ref[...])

    @pl.when(j == pl.num_programs(1) - 1)
    def _():
        x2 = acc_sc[...]
        gate = 1.0 / (1.0 + jnp.exp(-_dot(_rms(x2, gp_ref[...]).astype(BF16), wg_ref[...])))
        x3 = x2 + gate * _dot(p_ref[...].astype(BF16), wp_ref[...])
        o_ref[...] = _rms(x3, gf_ref[...]) if final else x3


def _post(x, a, b, c, p, lw, g_final, final):
    t, d = x.shape
    tm = POST_ROW_TILE
    d_ff = lw["w_ff1"].shape[1]
    row = lambda i, j: (i, 0)
    fixed = lambda i, j: (0, 0)
    wspec = lambda arr: pl.BlockSpec(arr.shape, fixed)
    return pl.pallas_call(
        functools.partial(_post_kernel, final=final),
        out_shape=jax.ShapeDtypeStruct((t, d), F32),
        grid=(t // tm, d_ff // FF_CHUNK),
        in_specs=[pl.BlockSpec((tm, d), row), pl.BlockSpec((tm, ATT_W), row), pl.BlockSpec((tm, SSD_INNER), row),
                  pl.BlockSpec((tm, ATT_W), row), pl.BlockSpec((tm, PLE_DIM), row),
                  wspec(lw["w_out"]), wspec(lw["g_mlp"]),
                  pl.BlockSpec((d, FF_CHUNK), lambda i, j: (0, j)), pl.BlockSpec((FF_CHUNK, d), lambda i, j: (j, 0)),
                  wspec(lw["g_ple"]), wspec(lw["w_ple_gate"]), wspec(lw["w_ple_proj"]), wspec(g_final)],
        out_specs=pl.BlockSpec((tm, d), row),
        scratch_shapes=[pltpu.VMEM((tm, d), F32), pltpu.VMEM((tm, d), BF16)],
        compiler_params=pltpu.CompilerParams(dimension_semantics=("arbitrary", "arbitrary"),
                                             vmem_limit_bytes=VMEM_LIMIT),
        name="post",
    )(x, a, b, c, p, lw["w_out"], lw["g_mlp"], lw["w_ff1"], lw["w_ff2"], lw["g_ple"], lw["w_ple_gate"],
      lw["w_ple_proj"], g_final)


def _layer_weights(i, g_mix, w_in, conv_w, conv_b, dt_bias, a_log, d_skip, ssd_norm_w, fox_f_bias, w_out,
                   g_mlp, w_ff1, w_ff2, g_ple, w_ple_gate, w_ple_proj):
    w = w_in[i]
    d = w.shape[0]
    o = 0
    mq, mk, mv = (w[:, o + k * ATT_W:o + (k + 1) * ATT_W] for k in range(3)); o += 3 * ATT_W
    wz = w[:, o:o + SSD_INNER]; o += SSD_INNER
    wxbc = w[:, o:o + SSD_XBC]; o += SSD_XBC
    wdt = w[:, o:o + SSD_HEADS]; o += SSD_HEADS
    fq, fk, fv = (w[:, o + k * ATT_W:o + (k + 1) * ATT_W] for k in range(3)); o += 3 * ATT_W
    wf = w[:, o:o + ATT_HEADS]
    small = jnp.concatenate([wdt, wf], axis=1)
    pad1 = lambda v, n: jnp.pad(v, (0, n - v.shape[0]))[None, :]
    return {
        "g_mix": g_mix[i][None, :],
        "inproj": {
            "q": jnp.concatenate([mq, fq], axis=1).astype(BF16),
            "v": jnp.concatenate([mv, fv], axis=1).astype(BF16),
            "z": wz.astype(BF16),
            "xbc": wxbc.astype(BF16),
            "sm": jnp.pad(small, ((0, 0), (0, SMALL_W - small.shape[1]))).astype(BF16),
            "kt": jnp.concatenate([mk, fk], axis=1).T.astype(BF16),
            "st": jnp.pad(small.T, ((0, SMALL_T - small.shape[1]), (0, 0))).astype(BF16),
        },
        "ssd": {
            "conv_w": conv_w[i], "conv_b": conv_b[i][None, :],
            "dt_bias": pad1(dt_bias[i], SMALL_W), "a_log": pad1(a_log[i], SMALL_W),
            "d_skip": jnp.repeat(d_skip[i], SSD_P)[None, :], "ssd_norm_w": ssd_norm_w[i][None, :],
        },
        "fox_fb": jnp.pad(fox_f_bias[i], (SSD_HEADS, SMALL_T - SSD_HEADS - ATT_HEADS))[:, None],
        "post": {
            "w_out": w_out[i].astype(BF16), "g_mlp": g_mlp[i][None, :], "w_ff1": w_ff1[i].astype(BF16),
            "w_ff2": w_ff2[i].astype(BF16), "g_ple": g_ple[i][None, :], "w_ple_gate": w_ple_gate[i].astype(BF16),
            "w_ple_proj": w_ple_proj[i].astype(BF16),
        },
    }


def kernel(x, p, g_mix, w_in, conv_w, conv_b, dt_bias, a_log, d_skip, ssd_norm_w, fox_f_bias, w_out, g_mlp, w_ff1, w_ff2, g_ple, w_ple_gate, w_ple_proj, g_final):
    batch, seq, d = x.shape
    depth = w_in.shape[0]
    assert seq % KV_BLOCK == 0 and seq % SSD_CHUNK == 0
    assert (batch * seq) % ROW_TILE == 0 and (batch * seq) % POST_ROW_TILE == 0 and w_ff1.shape[2] % FF_CHUNK == 0
    assert seq // KV_BLOCK <= HEAD_DIM, "one-hot block ids share the 64 lanes beside a head's channels"
    xt = x.reshape(batch * seq, d)
    for i in range(depth):
        lw = _layer_weights(i, g_mix, w_in, conv_w, conv_b, dt_bias, a_log, d_skip, ssd_norm_w, fox_f_bias,
                            w_out, g_mlp, w_ff1, w_ff2, g_ple, w_ple_gate, w_ple_proj)
        q, v, z, xbc, sm, kt, st = _inproj(xt, lw["g_mix"], lw["inproj"])
        out_a = _moba(q, kt, v, batch, seq)
        out_b = _ssd(xbc, z, sm, lw["ssd"], batch, seq)
        out_c = _fox(q, kt, v, st, lw["fox_fb"], batch, seq)
        xt = _post(xt, out_a, out_b, out_c, p[i].reshape(batch * seq, -1), lw["post"], g_final[None, :],
                   final=(i == depth - 1))
    return xt.reshape(batch, seq, d)
```

```python
import functools

import jax
import jax.numpy as jnp
from jax import lax
from jax.experimental import pallas as pl
from jax.experimental.pallas import tpu as pltpu

F32 = jnp.float32
BF16 = jnp.bfloat16

HEAD_DIM = 64
ATT_HEADS = 4
ATT_W = ATT_HEADS * HEAD_DIM
PAIR_W = 2 * HEAD_DIM
KV_BLOCK = 256
MOBA_TOPK = 3
AUG_POS = 32
SSD_HEADS = 8
SSD_P = 64
SSD_INNER = SSD_HEADS * SSD_P
SSD_GROUPS = 2
SSD_N = 128
SSD_CONV = 4
SSD_CHUNK = 256
SSD_XBC = SSD_INNER + 2 * SSD_GROUPS * SSD_N
SMALL_W = 128
PLE_DIM = 256
RMS_EPS = 1e-6
NEG = -1e30
LANES = 128
VMEM_LIMIT = 56 * 1024 * 1024

ROW_TILE = 512
POST_ROW_TILE = 1024
FF_CHUNK = 512


def _rms(x, g):
    return x * lax.rsqrt(jnp.mean(x * x, axis=-1, keepdims=True) + RMS_EPS) * g


def _dot(a, b):
    return jnp.dot(a, b, preferred_element_type=F32)


def _dot_nt(a, b):
    return lax.dot_general(a, b, (((1,), (1,)), ((), ())), preferred_element_type=F32)


def _dot_tn(a, b):
    return lax.dot_general(a, b, (((0,), (0,)), ((), ())), preferred_element_type=F32)


def _split3(a):
    hi = a.astype(BF16)
    r1 = a - hi.astype(F32)
    mid = r1.astype(BF16)
    lo = (r1 - mid.astype(F32)).astype(BF16)
    return hi, mid, lo


def _softplus(x):
    return jnp.maximum(x, 0.0) + jnp.log(1.0 + jnp.exp(-jnp.abs(x)))


def _silu(x):
    return x * (1.0 / (1.0 + jnp.exp(-x)))


def _inproj_kernel(x_ref, g_ref, wqt_ref, wk_ref, wvt_ref, wz_ref, wxbc_ref, wsm_ref,
                   qt_ref, k_ref, vt_ref, z_ref, xbc_ref, sm_ref):
    h = _rms(x_ref[...], g_ref[...]).astype(BF16)
    qt_ref[...] = (_dot_nt(wqt_ref[...], h) * (HEAD_DIM ** -0.5)).astype(BF16)
    k_ref[...] = _dot(h, wk_ref[...]).astype(BF16)
    vt_ref[...] = _dot_nt(wvt_ref[...], h).astype(BF16)
    z_ref[...] = _dot(h, wz_ref[...])
    xbc_ref[...] = _dot(h, wxbc_ref[...])
    sm_ref[...] = _dot(h, wsm_ref[...])


def _inproj(x, g, w):
    t, d = x.shape
    tm = ROW_TILE
    row = lambda i: (i, 0)
    col = lambda i: (0, i)
    fixed = lambda i: (0, 0)
    wspec = lambda a: pl.BlockSpec(a.shape, fixed)
    outs = (
        jax.ShapeDtypeStruct((2 * ATT_W, t), BF16),
        jax.ShapeDtypeStruct((t, 2 * ATT_W), BF16),
        jax.ShapeDtypeStruct((2 * ATT_W, t), BF16),
        jax.ShapeDtypeStruct((t, SSD_INNER), F32),
        jax.ShapeDtypeStruct((t, SSD_XBC), F32),
        jax.ShapeDtypeStruct((t, SMALL_W), F32),
    )
    ws = (w["qt"], w["k"], w["vt"], w["z"], w["xbc"], w["sm"])
    return pl.pallas_call(
        _inproj_kernel,
        out_shape=outs,
        grid=(t // tm,),
        in_specs=[pl.BlockSpec((tm, d), row), wspec(g)] + [wspec(a) for a in ws],
        out_specs=(pl.BlockSpec((2 * ATT_W, tm), col), pl.BlockSpec((tm, 2 * ATT_W), row),
                   pl.BlockSpec((2 * ATT_W, tm), col), pl.BlockSpec((tm, SSD_INNER), row),
                   pl.BlockSpec((tm, SSD_XBC), row), pl.BlockSpec((tm, SMALL_W), row)),
        compiler_params=pltpu.CompilerParams(dimension_semantics=("arbitrary",), vmem_limit_bytes=VMEM_LIMIT),
        name="inproj",
    )(x, g, *ws)


def _flash_pipeline(i, kaug_sc, qaug_sc, vt_ref, shift_of, o_ref, s_sc, p_sc, acc_sc):
    heads = range(ATT_HEADS)

    def scores(h, blk):
        c0 = pl.multiple_of(blk * KV_BLOCK, KV_BLOCK)
        return _dot(kaug_sc[pl.ds(c0, KV_BLOCK), h * PAIR_W:(h + 1) * PAIR_W], qaug_sc[h])

    def pv_stage(blk, alpha):
        c0 = pl.multiple_of(blk * KV_BLOCK, KV_BLOCK)
        for h in heads:
            acc_sc[h] = alpha[h] * acc_sc[h] + _dot(vt_ref[h * HEAD_DIM:(h + 1) * HEAD_DIM, pl.ds(c0, KV_BLOCK)],
                                                    p_sc[h])

    def softmax_stage(blk, m, l):
        m_out, l_out, alpha = [], [], []
        for h in heads:
            s = s_sc[h]
            shift = shift_of(h, blk)
            m_new = jnp.maximum(m[h], jnp.max(s, axis=0, keepdims=True) + shift)
            a = jnp.exp(m[h] - m_new)
            p = jnp.exp(s - (m_new - shift))
            p_sc[h] = p.astype(BF16)
            m_out.append(m_new)
            l_out.append(a * l[h] + jnp.sum(p, axis=0, keepdims=True))
            alpha.append(a)
        return tuple(m_out), tuple(l_out), tuple(alpha)

    causal = _causal_keys()
    for h in heads:
        s_sc[h] = jnp.where(causal, scores(h, i), NEG)
    p_sc[...] = jnp.zeros(p_sc.shape, BF16)
    acc_sc[...] = jnp.zeros(acc_sc.shape, F32)
    row = lambda v: (jnp.full((1, KV_BLOCK), v, F32),) * ATT_HEADS
    carry = (row(NEG), row(0.0), row(1.0))

    def tile_block(t):
        return jnp.where(t < 1, i, t - 1)

    def step(t, carry):
        m, l, alpha = carry
        pv_stage(tile_block(t - 1), alpha)
        nxt = [scores(h, t) for h in heads]
        m, l, alpha = softmax_stage(tile_block(t), m, l)
        for h in heads:
            s_sc[h] = nxt[h]
        return m, l, alpha

    m, l, alpha = lax.fori_loop(0, i, step, carry)
    pv_stage(tile_block(i - 1), alpha)
    m, l, alpha = softmax_stage(tile_block(i), m, l)
    pv_stage(tile_block(i), alpha)
    out_t = jnp.concatenate([acc_sc[h] / l[h] for h in heads], axis=0)
    o_ref[...] = out_t.T.astype(o_ref.dtype)


def _flash_scratch():
    return [pltpu.VMEM((ATT_HEADS, PAIR_W, KV_BLOCK), BF16),
            pltpu.VMEM((ATT_HEADS, KV_BLOCK, KV_BLOCK), F32),
            pltpu.VMEM((ATT_HEADS, KV_BLOCK, KV_BLOCK), BF16),
            pltpu.VMEM((ATT_HEADS, HEAD_DIM, KV_BLOCK), F32)]


def _own_lanes(h, shape):
    lane = lax.broadcasted_iota(jnp.int32, shape, 1)
    if h % 2 == 0:
        return lane < HEAD_DIM, lane - HEAD_DIM
    return lane >= HEAD_DIM, lane


def _stack_q(h, q_h, aug_rows):
    return jnp.concatenate([q_h, aug_rows] if h % 2 == 0 else [aug_rows, q_h], axis=0)


def _causal_keys():
    key = lax.broadcasted_iota(jnp.int32, (KV_BLOCK, KV_BLOCK), 0)
    qry = lax.broadcasted_iota(jnp.int32, (KV_BLOCK, KV_BLOCK), 1)
    return key <= qry


def _moba_kernel(qt_ref, k_ref, vt_ref, o_ref, kaug_sc, km_sc, qaug_sc, s_sc, p_sc, acc_sc, *, nb):
    i = pl.program_id(1)
    slopes = [2.0 ** (-8.0 * (h + 1) / ATT_HEADS) for h in range(ATT_HEADS)]

    @pl.when(i == 0)
    def _():
        km_sc[...] = jnp.zeros(km_sc.shape, F32)
        offs = lax.broadcasted_iota(jnp.int32, (KV_BLOCK, PAIR_W), 0).astype(F32)
        km_row = lax.broadcasted_iota(jnp.int32, (HEAD_DIM, PAIR_W), 0)

        def build(n, carry):
            c0 = pl.multiple_of(n * KV_BLOCK, KV_BLOCK)
            for j in range(ATT_HEADS // 2):
                kp = k_ref[pl.ds(c0, KV_BLOCK), j * PAIR_W:(j + 1) * PAIR_W]
                mean = jnp.sum(kp.astype(F32), axis=0, keepdims=True) * (1.0 / KV_BLOCK)
                for h in (2 * j, 2 * j + 1):
                    own, slot = _own_lanes(h, (KV_BLOCK, PAIR_W))
                    aug = jnp.where(slot == n, 1.0,
                                    jnp.where((slot >= AUG_POS) & (slot < AUG_POS + 3), offs, 0.0))
                    kaug_sc[pl.ds(c0, KV_BLOCK), h * PAIR_W:(h + 1) * PAIR_W] = jnp.where(own, kp, aug.astype(BF16))
                    own_km, _ = _own_lanes(h, (HEAD_DIM, PAIR_W))
                    km_sc[h] = jnp.where(km_row == n, jnp.where(own_km, mean, 0.0), km_sc[h])
            return carry

        lax.fori_loop(0, nb, build, 0)

    blk_id = lax.broadcasted_iota(jnp.int32, (HEAD_DIM, KV_BLOCK), 0)
    blk_idf = blk_id.astype(F32)
    for h in range(ATT_HEADS):
        j = h // 2
        qp = qt_ref[j * PAIR_W:(j + 1) * PAIR_W, :]
        km_hi, km_mid, km_lo = _split3(km_sc[h])
        score = jnp.where(blk_id < i, _dot(km_hi, qp) + _dot(km_mid, qp) + _dot(km_lo, qp), NEG)
        sel = blk_id == i
        for _ in range(MOBA_TOPK):
            top = jnp.max(score, axis=0, keepdims=True)
            first = jnp.min(jnp.where(score == top, blk_idf, 1e9), axis=0, keepdims=True)
            hit = blk_idf == first
            sel = sel | (hit & (top > 0.5 * NEG))
            score = jnp.where(hit, 3.0 * NEG, score)
        s_hi, s_mid, s_lo = (t.astype(F32) for t in _split3(jnp.full((1, 1), slopes[h], F32)))
        aug = jnp.where(blk_id < AUG_POS, jnp.where(sel, 0.0, NEG),
                        jnp.where(blk_id == AUG_POS, s_hi,
                                  jnp.where(blk_id == AUG_POS + 1, s_mid,
                                            jnp.where(blk_id == AUG_POS + 2, s_lo, 0.0))))
        qaug_sc[h] = _stack_q(h, qt_ref[h * HEAD_DIM:(h + 1) * HEAD_DIM, :], aug.astype(BF16))

    def shift_of(h, blk):
        return (slopes[h] * KV_BLOCK) * jnp.full((1, KV_BLOCK), blk - i, jnp.int32).astype(F32)

    _flash_pipeline(i, kaug_sc, qaug_sc, vt_ref, shift_of, o_ref, s_sc, p_sc, acc_sc)


def _moba(qt, k, vt, batch, seq):
    nb = seq // KV_BLOCK
    return pl.pallas_call(
        functools.partial(_moba_kernel, nb=nb),
        out_shape=jax.ShapeDtypeStruct((batch * seq, ATT_W), BF16),
        grid=(batch, nb),
        in_specs=[pl.BlockSpec((ATT_W, KV_BLOCK), lambda b, i: (0, b * nb + i)),
                  pl.BlockSpec((seq, ATT_W), lambda b, i: (b, 0)),
                  pl.BlockSpec((ATT_W, seq), lambda b, i: (0, b))],
        out_specs=pl.BlockSpec((KV_BLOCK, ATT_W), lambda b, i: (b * nb + i, 0)),
        scratch_shapes=[pltpu.VMEM((seq, ATT_HEADS * PAIR_W), BF16),
                        pltpu.VMEM((ATT_HEADS, HEAD_DIM, PAIR_W), F32)] + _flash_scratch(),
        compiler_params=pltpu.CompilerParams(dimension_semantics=("arbitrary", "arbitrary"),
                                             vmem_limit_bytes=VMEM_LIMIT),
        name="moba",
    )(qt, k, vt)


def _fox_kernel(qt_ref, k_ref, vt_ref, sm_ref, fb_ref, o_ref, kaug_sc, cstart_sc, qaug_sc, s_sc, p_sc, acc_sc,
                *, nb):
    i = pl.program_id(1)
    f_lane = SSD_HEADS

    @pl.when(i == 0)
    def _():
        r = lax.broadcasted_iota(jnp.int32, (KV_BLOCK, KV_BLOCK), 0)
        c = lax.broadcasted_iota(jnp.int32, (KV_BLOCK, KV_BLOCK), 1)
        tril = (c <= r).astype(BF16)
        src = lax.broadcasted_iota(jnp.int32, (3 * SMALL_W, PAIR_W), 0)
        places = []
        for h in range(ATT_HEADS):
            _, slot = _own_lanes(h, (3 * SMALL_W, PAIR_W))
            places.append(((src == f_lane + h + SMALL_W * (slot - AUG_POS)) & (slot >= AUG_POS)
                           & (slot < AUG_POS + 3)).astype(BF16))

        def scan(n, carry):
            c0 = pl.multiple_of(n * KV_BLOCK, KV_BLOCK)
            logf = -_softplus(-(sm_ref[pl.ds(c0, KV_BLOCK), :] + fb_ref[...]))
            hi, mid, lo = _split3(logf)
            cs = _dot(tril, hi) + _dot(tril, mid) + _dot(tril, lo)
            cstart_sc[n] = carry
            cs3 = jnp.concatenate(_split3(cs), axis=1)
            for h in range(ATT_HEADS):
                j = h // 2
                own, _ = _own_lanes(h, (KV_BLOCK, PAIR_W))
                kp = k_ref[pl.ds(c0, KV_BLOCK), j * PAIR_W:(j + 1) * PAIR_W]
                kaug_sc[pl.ds(c0, KV_BLOCK), h * PAIR_W:(h + 1) * PAIR_W] = jnp.where(
                    own, kp, _dot(cs3, places[h]).astype(BF16))
            return carry + cs[KV_BLOCK - 1:KV_BLOCK, :]

        lax.fori_loop(0, nb, scan, jnp.zeros((1, SMALL_W), F32))

    aug_id = lax.broadcasted_iota(jnp.int32, (HEAD_DIM, KV_BLOCK), 0)
    aug = jnp.where((aug_id >= AUG_POS) & (aug_id < AUG_POS + 3), -1.0, 0.0).astype(BF16)
    for h in range(ATT_HEADS):
        qaug_sc[h] = _stack_q(h, qt_ref[h * HEAD_DIM:(h + 1) * HEAD_DIM, :], aug)

    def shift_of(h, blk):
        return -cstart_sc[blk][:, f_lane + h:f_lane + h + 1]

    _flash_pipeline(i, kaug_sc, qaug_sc, vt_ref, shift_of, o_ref, s_sc, p_sc, acc_sc)


def _fox(qt, k, vt, sm, fb, batch, seq):
    nb = seq // KV_BLOCK
    return pl.pallas_call(
        functools.partial(_fox_kernel, nb=nb),
        out_shape=jax.ShapeDtypeStruct((batch * seq, ATT_W), BF16),
        grid=(batch, nb),
        in_specs=[pl.BlockSpec((ATT_W, KV_BLOCK), lambda b, i: (1, b * nb + i)),
                  pl.BlockSpec((seq, ATT_W), lambda b, i: (b, 1)),
                  pl.BlockSpec((ATT_W, seq), lambda b, i: (1, b)),
                  pl.BlockSpec((seq, SMALL_W), lambda b, i: (b, 0)),
                  pl.BlockSpec((1, SMALL_W), lambda b, i: (0, 0))],
        out_specs=pl.BlockSpec((KV_BLOCK, ATT_W), lambda b, i: (b * nb + i, 0)),
        scratch_shapes=[pltpu.VMEM((seq, ATT_HEADS * PAIR_W), BF16),
                        pltpu.VMEM((nb, 1, SMALL_W), F32)] + _flash_scratch(),
        compiler_params=pltpu.CompilerParams(dimension_semantics=("arbitrary", "arbitrary"),
                                             vmem_limit_bytes=VMEM_LIMIT),
        name="fox",
    )(qt, k, vt, sm, fb)


def _ssd_kernel(xbc_ref, z_ref, sm_ref, cw_ref, cb_ref, dtb_ref, alog_ref, dskip_ref, nw_ref, o_ref,
                ext_sc, state_sc, y_sc):
    c = pl.program_id(1)
    L = SSD_CHUNK
    tail = 8

    @pl.when(c == 0)
    def _():
        ext_sc[0:tail, :] = jnp.zeros((tail, SSD_XBC), F32)
        state_sc[...] = jnp.zeros(state_sc.shape, F32)

    ext_sc[tail:tail + L, :] = xbc_ref[...]
    u = cb_ref[...]
    for k in range(SSD_CONV):
        back = SSD_CONV - 1 - k
        u = u + cw_ref[k:k + 1, :] * ext_sc[tail - back:tail - back + L, :]
    ext_sc[0:tail, :] = ext_sc[L:L + tail, :]
    xc = _silu(u)

    dt = _softplus(sm_ref[...] + dtb_ref[...])
    a = dt * (-jnp.exp(alog_ref[...]))
    r = lax.broadcasted_iota(jnp.int32, (L, L), 0)
    cidx = lax.broadcasted_iota(jnp.int32, (L, L), 1)
    causal = cidx <= r
    tril = causal.astype(BF16)
    a_hi, a_mid, a_lo = _split3(a)
    acs = _dot(tril, a_hi) + _dot(tril, a_mid) + _dot(tril, a_lo)
    acs_t = acs.T

    for g in range(SSD_GROUPS):
        bg = xc[:, SSD_INNER + g * SSD_N:SSD_INNER + (g + 1) * SSD_N].astype(BF16)
        cg = xc[:, SSD_INNER + (SSD_GROUPS + g) * SSD_N:SSD_INNER + (SSD_GROUPS + g + 1) * SSD_N].astype(BF16)
        cb = _dot_nt(cg, bg)
        for h in range(g * (SSD_HEADS // SSD_GROUPS), (g + 1) * (SSD_HEADS // SSD_GROUPS)):
            cols = slice(h * SSD_P, (h + 1) * SSD_P)
            acol = acs[:, h:h + 1]
            total = acs[L - 1:L, h:h + 1]
            xs = xc[:, cols]
            xh = xs * dt[:, h:h + 1]
            decay = jnp.exp(jnp.where(causal, acol - acs_t[h:h + 1, :], NEG))
            y_diag = _dot((cb * decay).astype(BF16), xh.astype(BF16))
            st = state_sc[:, cols]
            y_off = jnp.exp(acol) * _dot(cg, st.astype(BF16))
            new = _dot_tn(bg, (xh * jnp.exp(total - acol)).astype(BF16))
            state_sc[:, cols] = jnp.exp(total) * st + new
            y_sc[:, cols] = y_diag + y_off + dskip_ref[:, cols] * xs

    y = y_sc[...] * _silu(z_ref[...])
    o_ref[...] = _rms(y, nw_ref[...]).astype(o_ref.dtype)


def _ssd(xbc, z, sm, lw, batch, seq):
    nc = seq // SSD_CHUNK
    L = SSD_CHUNK
    row = lambda b, c: (b * nc + c, 0)
    fixed = lambda b, c: (0, 0)
    wspec = lambda a: pl.BlockSpec(a.shape, fixed)
    ws = (lw["conv_w"], lw["conv_b"], lw["dt_bias"], lw["a_log"], lw["d_skip"], lw["ssd_norm_w"])
    return pl.pallas_call(
        _ssd_kernel,
        out_shape=jax.ShapeDtypeStruct((batch * seq, SSD_INNER), BF16),
        grid=(batch, nc),
        in_specs=[pl.BlockSpec((L, SSD_XBC), row), pl.BlockSpec((L, SSD_INNER), row),
                  pl.BlockSpec((L, SMALL_W), row)] + [wspec(a) for a in ws],
        out_specs=pl.BlockSpec((L, SSD_INNER), row),
        scratch_shapes=[pltpu.VMEM((L + 8, SSD_XBC), F32),
                        pltpu.VMEM((SSD_N, SSD_INNER), F32),
                        pltpu.VMEM((L, SSD_INNER), F32)],
        compiler_params=pltpu.CompilerParams(dimension_semantics=("arbitrary", "arbitrary"),
                                             vmem_limit_bytes=VMEM_LIMIT),
        name="ssd",
    )(xbc, z, sm, *ws)


def _post_kernel(x_ref, a_ref, b_ref, c_ref, p_ref, wo_ref, gm_ref, w1_ref, w2_ref, gp_ref, wg_ref, wp_ref,
                 gf_ref, o_ref, acc_sc, h_sc, *, final):
    j = pl.program_id(1)

    @pl.when(j == 0)
    def _():
        wa = wo_ref[0:ATT_W, :]
        wb = wo_ref[ATT_W:ATT_W + SSD_INNER, :]
        wc = wo_ref[ATT_W + SSD_INNER:, :]
        x1 = x_ref[...] + _dot(a_ref[...], wa) + _dot(b_ref[...], wb) + _dot(c_ref[...], wc)
        acc_sc[...] = x1
        h_sc[...] = _rms(x1, gm_ref[...]).astype(BF16)

    u = jnp.maximum(_dot(h_sc[...], w1_ref[...]), 0.0)
    acc_sc[...] += _dot((u * u).astype(BF16), w2_ref[...])

    @pl.when(j == pl.num_programs(1) - 1)
    def _():
        x2 = acc_sc[...]
        gate = 1.0 / (1.0 + jnp.exp(-_dot(_rms(x2, gp_ref[...]).astype(BF16), wg_ref[...])))
        x3 = x2 + gate * _dot(p_ref[...].astype(BF16), wp_ref[...])
        o_ref[...] = _rms(x3, gf_ref[...]) if final else x3


def _post(x, a, b, c, p, lw, g_final, final):
    t, d = x.shape
    tm = POST_ROW_TILE
    d_ff = lw["w_ff1"].shape[1]
    row = lambda i, j: (i, 0)
    fixed = lambda i, j: (0, 0)
    wspec = lambda arr: pl.BlockSpec(arr.shape, fixed)
    return pl.pallas_call(
        functools.partial(_post_kernel, final=final),
        out_shape=jax.ShapeDtypeStruct((t, d), F32),
        grid=(t // tm, d_ff // FF_CHUNK),
        in_specs=[pl.BlockSpec((tm, d), row), pl.BlockSpec((tm, ATT_W), row), pl.BlockSpec((tm, SSD_INNER), row),
                  pl.BlockSpec((tm, ATT_W), row), pl.BlockSpec((tm, PLE_DIM), row),
                  wspec(lw["w_out"]), wspec(lw["g_mlp"]),
                  pl.BlockSpec((d, FF_CHUNK), lambda i, j: (0, j)), pl.BlockSpec((FF_CHUNK, d), lambda i, j: (j, 0)),
                  wspec(lw["g_ple"]), wspec(lw["w_ple_gate"]), wspec(lw["w_ple_proj"]), wspec(g_final)],
        out_specs=pl.BlockSpec((tm, d), row),
        scratch_shapes=[pltpu.VMEM((tm, d), F32), pltpu.VMEM((tm, d), BF16)],
        compiler_params=pltpu.CompilerParams(dimension_semantics=("arbitrary", "arbitrary"),
                                             vmem_limit_bytes=VMEM_LIMIT),
        name="post",
    )(x, a, b, c, p, lw["w_out"], lw["g_mlp"], lw["w_ff1"], lw["w_ff2"], lw["g_ple"], lw["w_ple_gate"],
      lw["w_ple_proj"], g_final)


def _layer_weights(i, g_mix, w_in, conv_w, conv_b, dt_bias, a_log, d_skip, ssd_norm_w, fox_f_bias, w_out,
                   g_mlp, w_ff1, w_ff2, g_ple, w_ple_gate, w_ple_proj):
    w = w_in[i]
    o = 0
    mq, mk, mv = (w[:, o + k * ATT_W:o + (k + 1) * ATT_W] for k in range(3)); o += 3 * ATT_W
    wz = w[:, o:o + SSD_INNER]; o += SSD_INNER
    wxbc = w[:, o:o + SSD_XBC]; o += SSD_XBC
    wdt = w[:, o:o + SSD_HEADS]; o += SSD_HEADS
    fq, fk, fv = (w[:, o + k * ATT_W:o + (k + 1) * ATT_W] for k in range(3)); o += 3 * ATT_W
    wf = w[:, o:o + ATT_HEADS]
    small = jnp.concatenate([wdt, wf], axis=1)
    pad1 = lambda v, n: jnp.pad(v, (0, n - v.shape[0]))[None, :]
    return {
        "g_mix": g_mix[i][None, :],
        "inproj": {
            "qt": jnp.concatenate([mq, fq], axis=1).T.astype(BF16),
            "k": jnp.concatenate([mk, fk], axis=1).astype(BF16),
            "vt": jnp.concatenate([mv, fv], axis=1).T.astype(BF16),
            "z": wz.astype(BF16),
            "xbc": wxbc.astype(BF16),
            "sm": jnp.pad(small, ((0, 0), (0, SMALL_W - small.shape[1]))).astype(BF16),
        },
        "ssd": {
            "conv_w": conv_w[i], "conv_b": conv_b[i][None, :],
            "dt_bias": pad1(dt_bias[i], SMALL_W), "a_log": pad1(a_log[i], SMALL_W),
            "d_skip": jnp.repeat(d_skip[i], SSD_P)[None, :], "ssd_norm_w": ssd_norm_w[i][None, :],
        },
        "fox_fb": jnp.pad(fox_f_bias[i], (SSD_HEADS, SMALL_W - SSD_HEADS - ATT_HEADS))[None, :],
        "post": {
            "w_out": w_out[i].astype(BF16), "g_mlp": g_mlp[i][None, :], "w_ff1": w_ff1[i].astype(BF16),
            "w_ff2": w_ff2[i].astype(BF16), "g_ple": g_ple[i][None, :], "w_ple_gate": w_ple_gate[i].astype(BF16),
            "w_ple_proj": w_ple_proj[i].astype(BF16),
        },
    }


def kernel(x, p, g_mix, w_in, conv_w, conv_b, dt_bias, a_log, d_skip, ssd_norm_w, fox_f_bias, w_out, g_mlp, w_ff1, w_ff2, g_ple, w_ple_gate, w_ple_proj, g_final):
    batch, seq, d = x.shape
    depth = w_in.shape[0]
    assert seq % KV_BLOCK == 0 and seq % SSD_CHUNK == 0
    assert (batch * seq) % ROW_TILE == 0 and (batch * seq) % POST_ROW_TILE == 0 and w_ff1.shape[2] % FF_CHUNK == 0
    assert seq // KV_BLOCK <= AUG_POS, "one-hot block ids use the augmentation slots below AUG_POS"
    xt = x.reshape(batch * seq, d)
    for i in range(depth):
        lw = _layer_weights(i, g_mix, w_in, conv_w, conv_b, dt_bias, a_log, d_skip, ssd_norm_w, fox_f_bias,
                            w_out, g_mlp, w_ff1, w_ff2, g_ple, w_ple_gate, w_ple_proj)
        qt, k, vt, z, xbc, sm = _inproj(xt, lw["g_mix"], lw["inproj"])
        out_a = _moba(qt, k, vt, batch, seq)
        out_b = _ssd(xbc, z, sm, lw["ssd"], batch, seq)
        out_c = _fox(qt, k, vt, sm, lw["fox_fb"], batch, seq)
        xt = _post(xt, out_a, out_b, out_c, p[i].reshape(batch * seq, -1), lw["post"], g_final[None, :],
                   final=(i == depth - 1))
    return xt.reshape(batch, seq, d)
```

```python
import functools

import jax
import jax.numpy as jnp
from jax import lax
from jax.experimental import pallas as pl
from jax.experimental.pallas import tpu as pltpu

F32 = jnp.float32
BF16 = jnp.bfloat16

HEAD_DIM = 64
ATT_HEADS = 4
ATT_W = ATT_HEADS * HEAD_DIM
PAIR_W = 2 * HEAD_DIM
KV_BLOCK = 256
WIN_BLOCKS = 2
KEY_WIN = WIN_BLOCKS * KV_BLOCK
SOFTMAX_ROWS = 64
MOBA_TOPK = 3
AUG_POS = 32
ACC_ROWS = HEAD_DIM + 16
LOG2E = 1.4426950408889634
SSD_HEADS = 8
SSD_P = 64
SSD_INNER = SSD_HEADS * SSD_P
SSD_GROUPS = 2
SSD_N = 128
SSD_CONV = 4
SSD_CHUNK = 256
SSD_XBC = SSD_INNER + 2 * SSD_GROUPS * SSD_N
SMALL_W = 128
PLE_DIM = 256
RMS_EPS = 1e-6
NEG = -1e30
LANES = 128
VMEM_LIMIT = 56 * 1024 * 1024

ROW_TILE = 512
POST_ROW_TILE = 1024
FF_CHUNK = 512


def _rms(x, g):
    return x * lax.rsqrt(jnp.mean(x * x, axis=-1, keepdims=True) + RMS_EPS) * g


def _dot(a, b):
    return jnp.dot(a, b, preferred_element_type=F32)


def _dot_nt(a, b):
    return lax.dot_general(a, b, (((1,), (1,)), ((), ())), preferred_element_type=F32)


def _dot_tn(a, b):
    return lax.dot_general(a, b, (((0,), (0,)), ((), ())), preferred_element_type=F32)


def _split3(a):
    hi = a.astype(BF16)
    r1 = a - hi.astype(F32)
    mid = r1.astype(BF16)
    lo = (r1 - mid.astype(F32)).astype(BF16)
    return hi, mid, lo


def _softplus(x):
    return jnp.maximum(x, 0.0) + jnp.log(1.0 + jnp.exp(-jnp.abs(x)))


def _silu(x):
    return x * (1.0 / (1.0 + jnp.exp(-x)))


def _inproj_kernel(x_ref, g_ref, wqv_ref, wk_ref, wz_ref, wxbc_ref, wsm_ref,
                   qt_ref, k_ref, vt_ref, z_ref, xbc_ref, sm_ref, wqvt_sc):
    @pl.when(pl.program_id(0) == 0)
    def _():
        wqvt_sc[...] = wqv_ref[...].T.astype(BF16)

    h = _rms(x_ref[...], g_ref[...]).astype(BF16)
    qt_ref[...] = (_dot_nt(wqvt_sc[0:2 * ATT_W, :], h) * (HEAD_DIM ** -0.5 * LOG2E)).astype(BF16)
    k_ref[...] = _dot(h, wk_ref[...]).astype(BF16)
    vt_ref[...] = _dot_nt(wqvt_sc[2 * ATT_W:, :], h).astype(BF16)
    z_ref[...] = _dot(h, wz_ref[...])
    xbc_ref[...] = _dot(h, wxbc_ref[...])
    sm_ref[...] = _dot(h, wsm_ref[...])


def _inproj(x, g, w):
    t, d = x.shape
    tm = ROW_TILE
    row = lambda i: (i, 0)
    col = lambda i: (0, i)
    fixed = lambda i: (0, 0)
    wspec = lambda a: pl.BlockSpec(a.shape, fixed)
    outs = (
        jax.ShapeDtypeStruct((2 * ATT_W, t), BF16),
        jax.ShapeDtypeStruct((t, 2 * ATT_W), BF16),
        jax.ShapeDtypeStruct((2 * ATT_W, t), BF16),
        jax.ShapeDtypeStruct((t, SSD_INNER), F32),
        jax.ShapeDtypeStruct((t, SSD_XBC), F32),
        jax.ShapeDtypeStruct((t, SMALL_W), F32),
    )
    ws = (w["qv"], w["k"], w["z"], w["xbc"], w["sm"])
    return pl.pallas_call(
        _inproj_kernel,
        out_shape=outs,
        grid=(t // tm,),
        in_specs=[pl.BlockSpec((tm, d), row), wspec(g)] + [wspec(a) for a in ws],
        out_specs=(pl.BlockSpec((2 * ATT_W, tm), col), pl.BlockSpec((tm, 2 * ATT_W), row),
                   pl.BlockSpec((2 * ATT_W, tm), col), pl.BlockSpec((tm, SSD_INNER), row),
                   pl.BlockSpec((tm, SSD_XBC), row), pl.BlockSpec((tm, SMALL_W), row)),
        scratch_shapes=[pltpu.VMEM((4 * ATT_W, d), BF16)],
        compiler_params=pltpu.CompilerParams(dimension_semantics=("arbitrary",), vmem_limit_bytes=VMEM_LIMIT),
        name="inproj",
    )(x, g, *ws)


def _flash_pipeline(i, kaug_sc, qaug_sc, vt_ref, shift_of, o_ref, s_sc, p_sc, acc_sc):
    heads = range(ATT_HEADS)
    own_win = i // WIN_BLOCKS

    def scores(h, win):
        c0 = pl.multiple_of(win * KEY_WIN, KEY_WIN)
        return _dot(kaug_sc[pl.ds(c0, KEY_WIN), h * PAIR_W:(h + 1) * PAIR_W], qaug_sc[h])

    ones = jnp.ones((ACC_ROWS - HEAD_DIM, KEY_WIN), BF16)

    def pv_stage(win, alpha):
        c0 = pl.multiple_of(win * KEY_WIN, KEY_WIN)
        for h in heads:
            vt_h = jnp.concatenate([vt_ref[h * HEAD_DIM:(h + 1) * HEAD_DIM, pl.ds(c0, KEY_WIN)], ones], axis=0)
            acc_sc[h] = alpha[h] * acc_sc[h] + _dot(vt_h, p_sc[h])

    def softmax_stage(blk, m, refill_win=None):
        m_out, alpha = [], []
        for h in heads:
            chunks = [pl.ds(r, SOFTMAX_ROWS) for r in range(0, KEY_WIN, SOFTMAX_ROWS)]
            top = s_sc[h, chunks[0], :]
            for rows in chunks[1:]:
                top = jnp.maximum(top, s_sc[h, rows, :])
            shift = shift_of(h, blk)
            m_new = jnp.maximum(m[h], jnp.max(top, axis=0, keepdims=True) + shift)
            base = m_new - shift
            for rows in chunks:
                p_sc[h, rows, :] = jnp.exp2(s_sc[h, rows, :] - base).astype(BF16)
            if refill_win is not None:
                s_sc[h] = scores(h, refill_win)
            m_out.append(m_new)
            alpha.append(jnp.exp2(m[h] - m_new))
        return tuple(m_out), tuple(alpha)

    key = lax.broadcasted_iota(jnp.int32, (KEY_WIN, KV_BLOCK), 0) + (own_win * KEY_WIN - i * KV_BLOCK)
    causal = key <= lax.broadcasted_iota(jnp.int32, (KEY_WIN, KV_BLOCK), 1)
    for h in heads:
        s_sc[h] = jnp.where(causal, scores(h, own_win), NEG)
    p_sc[...] = jnp.zeros(p_sc.shape, BF16)
    acc_sc[...] = jnp.zeros(acc_sc.shape, F32)
    row = lambda v: (jnp.full((1, KV_BLOCK), v, F32),) * ATT_HEADS
    carry = (row(NEG), row(1.0))

    def tile_win(t):
        return jnp.where(t < 1, own_win, t - 1)

    def step(t, carry):
        m, alpha = carry
        pv_stage(tile_win(t - 1), alpha)
        return softmax_stage(tile_win(t), m, refill_win=t)

    m, alpha = lax.fori_loop(0, own_win, step, carry)
    pv_stage(tile_win(own_win - 1), alpha)
    m, alpha = softmax_stage(tile_win(own_win), m)
    pv_stage(tile_win(own_win), alpha)
    out_t = jnp.concatenate([acc_sc[h][0:HEAD_DIM, :] / acc_sc[h][HEAD_DIM:HEAD_DIM + 1, :] for h in heads],
                            axis=0)
    o_ref[...] = out_t.T.astype(o_ref.dtype)


def _flash_scratch():
    return [pltpu.VMEM((ATT_HEADS, PAIR_W, KV_BLOCK), BF16),
            pltpu.VMEM((ATT_HEADS, KEY_WIN, KV_BLOCK), F32),
            pltpu.VMEM((ATT_HEADS, KEY_WIN, KV_BLOCK), BF16),
            pltpu.VMEM((ATT_HEADS, ACC_ROWS, KV_BLOCK), F32)]


def _own_lanes(h, shape):
    lane = lax.broadcasted_iota(jnp.int32, shape, 1)
    if h % 2 == 0:
        return lane < HEAD_DIM, lane - HEAD_DIM
    return lane >= HEAD_DIM, lane


def _stack_q(h, q_h, aug_rows):
    return jnp.concatenate([q_h, aug_rows] if h % 2 == 0 else [aug_rows, q_h], axis=0)


def _moba_kernel(qt_ref, k_ref, vt_ref, o_ref, kaug_sc, km_sc, qaug_sc, s_sc, p_sc, acc_sc, *, nb):
    i = pl.program_id(1)
    slopes = [LOG2E * 2.0 ** (-8.0 * (h + 1) / ATT_HEADS) for h in range(ATT_HEADS)]

    @pl.when(i == 0)
    def _():
        km_sc[...] = jnp.zeros(km_sc.shape, F32)
        offs = lax.broadcasted_iota(jnp.int32, (KV_BLOCK, PAIR_W), 0).astype(F32)
        km_row = lax.broadcasted_iota(jnp.int32, (HEAD_DIM, PAIR_W), 0)

        def build(n, carry):
            c0 = pl.multiple_of(n * KV_BLOCK, KV_BLOCK)
            in_win = jnp.full((KV_BLOCK, PAIR_W), n % WIN_BLOCKS, jnp.int32).astype(F32)
            for j in range(ATT_HEADS // 2):
                kp = k_ref[pl.ds(c0, KV_BLOCK), j * PAIR_W:(j + 1) * PAIR_W]
                mean = jnp.sum(kp.astype(F32), axis=0, keepdims=True) * (1.0 / KV_BLOCK)
                for h in (2 * j, 2 * j + 1):
                    own, slot = _own_lanes(h, (KV_BLOCK, PAIR_W))
                    aug = jnp.where(slot == n, 1.0,
                                    jnp.where((slot >= AUG_POS) & (slot < AUG_POS + 3), offs,
                                              jnp.where((slot >= AUG_POS + 3) & (slot < AUG_POS + 6), in_win, 0.0)))
                    kaug_sc[pl.ds(c0, KV_BLOCK), h * PAIR_W:(h + 1) * PAIR_W] = jnp.where(own, kp, aug.astype(BF16))
                    own_km, _ = _own_lanes(h, (HEAD_DIM, PAIR_W))
                    km_sc[h] = jnp.where(km_row == n, jnp.where(own_km, mean, 0.0), km_sc[h])
            return carry

        lax.fori_loop(0, nb, build, 0)

    blk_id = lax.broadcasted_iota(jnp.int32, (HEAD_DIM, KV_BLOCK), 0)
    blk_idf = blk_id.astype(F32)
    for h in range(ATT_HEADS):
        j = h // 2
        qp = qt_ref[j * PAIR_W:(j + 1) * PAIR_W, :]
        km_hi, km_mid, km_lo = _split3(km_sc[h])
        score = jnp.where(blk_id < i, _dot(km_hi, qp) + _dot(km_mid, qp) + _dot(km_lo, qp), NEG)
        sel = blk_id == i
        for _ in range(MOBA_TOPK):
            top = jnp.max(score, axis=0, keepdims=True)
            first = jnp.min(jnp.where(score == top, blk_idf, 1e9), axis=0, keepdims=True)
            hit = blk_idf == first
            sel = sel | (hit & (top > 0.5 * NEG))
            score = jnp.where(hit, 3.0 * NEG, score)
        terms = [t.astype(F32) for t in _split3(jnp.full((1, 1), slopes[h], F32))]
        terms += [t * KV_BLOCK for t in terms]
        aug = jnp.where(blk_id < AUG_POS, jnp.where(sel, 0.0, NEG), 0.0)
        for k, term in enumerate(terms):
            aug = jnp.where(blk_id == AUG_POS + k, term, aug)
        qaug_sc[h] = _stack_q(h, qt_ref[h * HEAD_DIM:(h + 1) * HEAD_DIM, :], aug.astype(BF16))

    def shift_of(h, win):
        return (slopes[h] * KEY_WIN) * jnp.full((1, KV_BLOCK), win - i // WIN_BLOCKS, jnp.int32).astype(F32)

    _flash_pipeline(i, kaug_sc, qaug_sc, vt_ref, shift_of, o_ref, s_sc, p_sc, acc_sc)


def _moba(qt, k, vt, batch, seq):
    nb = seq // KV_BLOCK
    return pl.pallas_call(
        functools.partial(_moba_kernel, nb=nb),
        out_shape=jax.ShapeDtypeStruct((batch * seq, ATT_W), BF16),
        grid=(batch, nb),
        in_specs=[pl.BlockSpec((ATT_W, KV_BLOCK), lambda b, i: (0, b * nb + i)),
                  pl.BlockSpec((seq, ATT_W), lambda b, i: (b, 0)),
                  pl.BlockSpec((ATT_W, seq), lambda b, i: (0, b))],
        out_specs=pl.BlockSpec((KV_BLOCK, ATT_W), lambda b, i: (b * nb + i, 0)),
        scratch_shapes=[pltpu.VMEM((seq, ATT_HEADS * PAIR_W), BF16),
                        pltpu.VMEM((ATT_HEADS, HEAD_DIM, PAIR_W), F32)] + _flash_scratch(),
        compiler_params=pltpu.CompilerParams(dimension_semantics=("arbitrary", "arbitrary"),
                                             vmem_limit_bytes=VMEM_LIMIT),
        name="moba",
    )(qt, k, vt)


def _fox_kernel(qt_ref, k_ref, vt_ref, sm_ref, fb_ref, o_ref, kaug_sc, cstart_sc, qaug_sc, s_sc, p_sc, acc_sc,
                *, nb):
    i = pl.program_id(1)
    f_lane = SSD_HEADS

    @pl.when(i == 0)
    def _():
        r = lax.broadcasted_iota(jnp.int32, (KEY_WIN, KEY_WIN), 0)
        c = lax.broadcasted_iota(jnp.int32, (KEY_WIN, KEY_WIN), 1)
        tril = (c <= r).astype(BF16)
        src = lax.broadcasted_iota(jnp.int32, (3 * SMALL_W, PAIR_W), 0)
        places = []
        for h in range(ATT_HEADS):
            _, slot = _own_lanes(h, (3 * SMALL_W, PAIR_W))
            places.append(((src == f_lane + h + SMALL_W * (slot - AUG_POS)) & (slot >= AUG_POS)
                           & (slot < AUG_POS + 3)).astype(BF16))

        def scan(w, carry):
            c0 = pl.multiple_of(w * KEY_WIN, KEY_WIN)
            logf = -LOG2E * _softplus(-(sm_ref[pl.ds(c0, KEY_WIN), :] + fb_ref[...]))
            hi, mid, lo = _split3(logf)
            cs = _dot(tril, hi) + _dot(tril, mid) + _dot(tril, lo)
            cstart_sc[w] = jnp.concatenate(
                [jnp.broadcast_to(carry[:, f_lane + h:f_lane + h + 1], (1, KV_BLOCK)) for h in range(ATT_HEADS)],
                axis=0)
            cs3 = jnp.concatenate(_split3(cs), axis=1)
            for h in range(ATT_HEADS):
                j = h // 2
                own, _ = _own_lanes(h, (KEY_WIN, PAIR_W))
                kp = k_ref[pl.ds(c0, KEY_WIN), j * PAIR_W:(j + 1) * PAIR_W]
                kaug_sc[pl.ds(c0, KEY_WIN), h * PAIR_W:(h + 1) * PAIR_W] = jnp.where(
                    own, kp, _dot(cs3, places[h]).astype(BF16))
            return carry + cs[KEY_WIN - 1:KEY_WIN, :]

        lax.fori_loop(0, nb // WIN_BLOCKS, scan, jnp.zeros((1, SMALL_W), F32))

    aug_id = lax.broadcasted_iota(jnp.int32, (HEAD_DIM, KV_BLOCK), 0)
    aug = jnp.where((aug_id >= AUG_POS) & (aug_id < AUG_POS + 3), -1.0, 0.0).astype(BF16)
    for h in range(ATT_HEADS):
        qaug_sc[h] = _stack_q(h, qt_ref[h * HEAD_DIM:(h + 1) * HEAD_DIM, :], aug)

    def shift_of(h, win):
        return -cstart_sc[win][h:h + 1, :]

    _flash_pipeline(i, kaug_sc, qaug_sc, vt_ref, shift_of, o_ref, s_sc, p_sc, acc_sc)


def _fox(qt, k, vt, sm, fb, batch, seq):
    nb = seq // KV_BLOCK
    return pl.pallas_call(
        functools.partial(_fox_kernel, nb=nb),
        out_shape=jax.ShapeDtypeStruct((batch * seq, ATT_W), BF16),
        grid=(batch, nb),
        in_specs=[pl.BlockSpec((ATT_W, KV_BLOCK), lambda b, i: (1, b * nb + i)),
                  pl.BlockSpec((seq, ATT_W), lambda b, i: (b, 1)),
                  pl.BlockSpec((ATT_W, seq), lambda b, i: (1, b)),
                  pl.BlockSpec((seq, SMALL_W), lambda b, i: (b, 0)),
                  pl.BlockSpec((1, SMALL_W), lambda b, i: (0, 0))],
        out_specs=pl.BlockSpec((KV_BLOCK, ATT_W), lambda b, i: (b * nb + i, 0)),
        scratch_shapes=[pltpu.VMEM((seq, ATT_HEADS * PAIR_W), BF16),
                        pltpu.VMEM((nb // WIN_BLOCKS, ATT_HEADS, KV_BLOCK), F32)]
                       + _flash_scratch(),
        compiler_params=pltpu.CompilerParams(dimension_semantics=("arbitrary", "arbitrary"),
                                             vmem_limit_bytes=VMEM_LIMIT),
        name="fox",
    )(qt, k, vt, sm, fb)


def _ssd_kernel(xbc_ref, z_ref, sm_ref, cw_ref, cb_ref, dtb_ref, alog_ref, dskip_ref, nw_ref, o_ref,
                ext_sc, state_sc, y_sc):
    c = pl.program_id(1)
    L = SSD_CHUNK
    tail = 8

    @pl.when(c == 0)
    def _():
        ext_sc[0:tail, :] = jnp.zeros((tail, SSD_XBC), F32)
        state_sc[...] = jnp.zeros(state_sc.shape, F32)

    ext_sc[tail:tail + L, :] = xbc_ref[...]
    u = cb_ref[...]
    for k in range(SSD_CONV):
        back = SSD_CONV - 1 - k
        u = u + cw_ref[k:k + 1, :] * ext_sc[tail - back:tail - back + L, :]
    ext_sc[0:tail, :] = ext_sc[L:L + tail, :]
    xc = _silu(u)

    dt = _softplus(sm_ref[...] + dtb_ref[...])
    a = dt * (-jnp.exp(alog_ref[...]))
    r = lax.broadcasted_iota(jnp.int32, (L, L), 0)
    cidx = lax.broadcasted_iota(jnp.int32, (L, L), 1)
    causal = cidx <= r
    tril = causal.astype(BF16)
    a_hi, a_mid, a_lo = _split3(a)
    acs = _dot(tril, a_hi) + _dot(tril, a_mid) + _dot(tril, a_lo)
    acs_t = acs.T

    for g in range(SSD_GROUPS):
        bg = xc[:, SSD_INNER + g * SSD_N:SSD_INNER + (g + 1) * SSD_N].astype(BF16)
        cg = xc[:, SSD_INNER + (SSD_GROUPS + g) * SSD_N:SSD_INNER + (SSD_GROUPS + g + 1) * SSD_N].astype(BF16)
        cb = _dot_nt(cg, bg)
        for h in range(g * (SSD_HEADS // SSD_GROUPS), (g + 1) * (SSD_HEADS // SSD_GROUPS)):
            cols = slice(h * SSD_P, (h + 1) * SSD_P)
            acol = acs[:, h:h + 1]
            total = acs[L - 1:L, h:h + 1]
            xs = xc[:, cols]
            xh = xs * dt[:, h:h + 1]
            decay = jnp.exp(jnp.where(causal, acol - acs_t[h:h + 1, :], NEG))
            y_diag = _dot((cb * decay).astype(BF16), xh.astype(BF16))
            st = state_sc[:, cols]
            y_off = jnp.exp(acol) * _dot(cg, st.astype(BF16))
            new = _dot_tn(bg, (xh * jnp.exp(total - acol)).astype(BF16))
            state_sc[:, cols] = jnp.exp(total) * st + new
            y_sc[:, cols] = y_diag + y_off + dskip_ref[:, cols] * xs

    y = y_sc[...] * _silu(z_ref[...])
    o_ref[...] = _rms(y, nw_ref[...]).astype(o_ref.dtype)


def _ssd(xbc, z, sm, lw, batch, seq):
    nc = seq // SSD_CHUNK
    L = SSD_CHUNK
    row = lambda b, c: (b * nc + c, 0)
    fixed = lambda b, c: (0, 0)
    wspec = lambda a: pl.BlockSpec(a.shape, fixed)
    ws = (lw["conv_w"], lw["conv_b"], lw["dt_bias"], lw["a_log"], lw["d_skip"], lw["ssd_norm_w"])
    return pl.pallas_call(
        _ssd_kernel,
        out_shape=jax.ShapeDtypeStruct((batch * seq, SSD_INNER), BF16),
        grid=(batch, nc),
        in_specs=[pl.BlockSpec((L, SSD_XBC), row), pl.BlockSpec((L, SSD_INNER), row),
                  pl.BlockSpec((L, SMALL_W), row)] + [wspec(a) for a in ws],
        out_specs=pl.BlockSpec((L, SSD_INNER), row),
        scratch_shapes=[pltpu.VMEM((L + 8, SSD_XBC), F32),
                        pltpu.VMEM((SSD_N, SSD_INNER), F32),
                        pltpu.VMEM((L, SSD_INNER), F32)],
        compiler_params=pltpu.CompilerParams(dimension_semantics=("arbitrary", "arbitrary"),
                                             vmem_limit_bytes=VMEM_LIMIT),
        name="ssd",
    )(xbc, z, sm, *ws)


def _post_kernel(x_ref, a_ref, b_ref, c_ref, p_ref, wo_ref, gm_ref, w1_ref, w2_ref, gp_ref, wg_ref, wp_ref,
                 gf_ref, o_ref, acc_sc, h_sc, *, final):
    j = pl.program_id(1)

    @pl.when(j == 0)
    def _():
        wa = wo_ref[0:ATT_W, :]
        wb = wo_ref[ATT_W:ATT_W + SSD_INNER, :]
        wc = wo_ref[ATT_W + SSD_INNER:, :]
        x1 = x_ref[...] + _dot(a_ref[...], wa) + _dot(b_ref[...], wb) + _dot(c_ref[...], wc)
        acc_sc[...] = x1
        h_sc[...] = _rms(x1, gm_ref[...]).astype(BF16)

    u = jnp.maximum(_dot(h_sc[...], w1_ref[...]), 0.0)
    acc_sc[...] += _dot((u * u).astype(BF16), w2_ref[...])

    @pl.when(j == pl.num_programs(1) - 1)
    def _():
        x2 = acc_sc[...]
        gate = 1.0 / (1.0 + jnp.exp(-_dot(_rms(x2, gp_ref[...]).astype(BF16), wg_ref[...])))
        x3 = x2 + gate * _dot(p_ref[...].astype(BF16), wp_ref[...])
        o_ref[...] = _rms(x3, gf_ref[...]) if final else x3


def _post(x, a, b, c, p, lw, g_final, final):
    t, d = x.shape
    tm = POST_ROW_TILE
    d_ff = lw["w_ff1"].shape[1]
    row = lambda i, j: (i, 0)
    fixed = lambda i, j: (0, 0)
    wspec = lambda arr: pl.BlockSpec(arr.shape, fixed)
    return pl.pallas_call(
        functools.partial(_post_kernel, final=final),
        out_shape=jax.ShapeDtypeStruct((t, d), F32),
        grid=(t // tm, d_ff // FF_CHUNK),
        in_specs=[pl.BlockSpec((tm, d), row), pl.BlockSpec((tm, ATT_W), row), pl.BlockSpec((tm, SSD_INNER), row),
                  pl.BlockSpec((tm, ATT_W), row), pl.BlockSpec((tm, PLE_DIM), row),
                  wspec(lw["w_out"]), wspec(lw["g_mlp"]),
                  pl.BlockSpec((d, FF_CHUNK), lambda i, j: (0, j)), pl.BlockSpec((FF_CHUNK, d), lambda i, j: (j, 0)),
                  wspec(lw["g_ple"]), wspec(lw["w_ple_gate"]), wspec(lw["w_ple_proj"]), wspec(g_final)],
        out_specs=pl.BlockSpec((tm, d), row),
        scratch_shapes=[pltpu.VMEM((tm, d), F32), pltpu.VMEM((tm, d), BF16)],
        compiler_params=pltpu.CompilerParams(dimension_semantics=("arbitrary", "arbitrary"),
                                             vmem_limit_bytes=VMEM_LIMIT),
        name="post",
    )(x, a, b, c, p, lw["w_out"], lw["g_mlp"], lw["w_ff1"], lw["w_ff2"], lw["g_ple"], lw["w_ple_gate"],
      lw["w_ple_proj"], g_final)


def _layer_weights(i, g_mix, w_in, conv_w, conv_b, dt_bias, a_log, d_skip, ssd_norm_w, fox_f_bias, w_out,
                   g_mlp, w_ff1, w_ff2, g_ple, w_ple_gate, w_ple_proj):
    w = w_in[i]
    o = 0
    mq, mk, mv = (w[:, o + k * ATT_W:o + (k + 1) * ATT_W] for k in range(3)); o += 3 * ATT_W
    wz = w[:, o:o + SSD_INNER]; o += SSD_INNER
    wxbc = w[:, o:o + SSD_XBC]; o += SSD_XBC
    wdt = w[:, o:o + SSD_HEADS]; o += SSD_HEADS
    fq, fk, fv = (w[:, o + k * ATT_W:o + (k + 1) * ATT_W] for k in range(3)); o += 3 * ATT_W
    wf = w[:, o:o + ATT_HEADS]
    small = jnp.concatenate([wdt, wf], axis=1)
    pad1 = lambda v, n: jnp.pad(v, (0, n - v.shape[0]))[None, :]
    return {
        "g_mix": g_mix[i][None, :],
        "inproj": {
            "qv": jnp.concatenate([mq, fq, mv, fv], axis=1),
            "k": jnp.concatenate([mk, fk], axis=1).astype(BF16),
            "z": wz.astype(BF16),
            "xbc": wxbc.astype(BF16),
            "sm": jnp.pad(small, ((0, 0), (0, SMALL_W - small.shape[1]))).astype(BF16),
        },
        "ssd": {
            "conv_w": conv_w[i], "conv_b": conv_b[i][None, :],
            "dt_bias": pad1(dt_bias[i], SMALL_W), "a_log": pad1(a_log[i], SMALL_W),
            "d_skip": jnp.repeat(d_skip[i], SSD_P)[None, :], "ssd_norm_w": ssd_norm_w[i][None, :],
        },
        "fox_fb": jnp.pad(fox_f_bias[i], (SSD_HEADS, SMALL_W - SSD_HEADS - ATT_HEADS))[None, :],
        "post": {
            "w_out": w_out[i].astype(BF16), "g_mlp": g_mlp[i][None, :], "w_ff1": w_ff1[i].astype(BF16),
            "w_ff2": w_ff2[i].astype(BF16), "g_ple": g_ple[i][None, :], "w_ple_gate": w_ple_gate[i].astype(BF16),
            "w_ple_proj": w_ple_proj[i].astype(BF16),
        },
    }


def kernel(x, p, g_mix, w_in, conv_w, conv_b, dt_bias, a_log, d_skip, ssd_norm_w, fox_f_bias, w_out, g_mlp, w_ff1, w_ff2, g_ple, w_ple_gate, w_ple_proj, g_final):
    batch, seq, d = x.shape
    depth = w_in.shape[0]
    assert seq % KEY_WIN == 0 and seq % SSD_CHUNK == 0
    assert (batch * seq) % ROW_TILE == 0 and (batch * seq) % POST_ROW_TILE == 0 and w_ff1.shape[2] % FF_CHUNK == 0
    assert seq // KV_BLOCK <= AUG_POS, "one-hot block ids use the augmentation slots below AUG_POS"
    xt = x.reshape(batch * seq, d)
    for i in range(depth):
        lw = _layer_weights(i, g_mix, w_in, conv_w, conv_b, dt_bias, a_log, d_skip, ssd_norm_w, fox_f_bias,
                            w_out, g_mlp, w_ff1, w_ff2, g_ple, w_ple_gate, w_ple_proj)
        qt, k, vt, z, xbc, sm = _inproj(xt, lw["g_mix"], lw["inproj"])
        out_a = _moba(qt, k, vt, batch, seq)
        out_b = _ssd(xbc, z, sm, lw["ssd"], batch, seq)
        out_c = _fox(qt, k, vt, sm, lw["fox_fb"], batch, seq)
        xt = _post(xt, out_a, out_b, out_c, p[i].reshape(batch * seq, -1), lw["post"], g_final[None, :],
                   final=(i == depth - 1))
    return xt.reshape(batch, seq, d)
```

```python
import functools

import jax
import jax.numpy as jnp
from jax import lax
from jax.experimental import pallas as pl
from jax.experimental.pallas import tpu as pltpu

F32 = jnp.float32
BF16 = jnp.bfloat16

HEAD_DIM = 64
ATT_HEADS = 4
ATT_W = ATT_HEADS * HEAD_DIM
PAIR_W = 2 * HEAD_DIM
KV_BLOCK = 256
WIN_BLOCKS = 2
KEY_WIN = WIN_BLOCKS * KV_BLOCK
SOFTMAX_ROWS = 64
MOBA_TOPK = 3
AUG_POS = 32
ACC_ROWS = HEAD_DIM + 16
LOG2E = 1.4426950408889634
SSD_HEADS = 8
SSD_P = 64
SSD_INNER = SSD_HEADS * SSD_P
SSD_GROUPS = 2
SSD_N = 128
SSD_CONV = 4
SSD_CHUNK = 256
SSD_XBC = SSD_INNER + 2 * SSD_GROUPS * SSD_N
SMALL_W = 128
IN_COLS = {"qv": (0, 4 * ATT_W), "k": (4 * ATT_W, 2 * ATT_W), "z": (6 * ATT_W, SSD_INNER),
           "xbc": (6 * ATT_W + SSD_INNER, SSD_XBC), "sm": (6 * ATT_W + SSD_INNER + SSD_XBC, SMALL_W)}
PLE_DIM = 256
RMS_EPS = 1e-6
NEG = -1e30
LANES = 128
VMEM_LIMIT = 56 * 1024 * 1024

ROW_TILE = 512
POST_ROW_TILE = 1024
FF_CHUNK = 512


def _rms(x, g):
    return x * lax.rsqrt(jnp.mean(x * x, axis=-1, keepdims=True) + RMS_EPS) * g


def _dot(a, b):
    return jnp.dot(a, b, preferred_element_type=F32)


def _dot_nt(a, b):
    return lax.dot_general(a, b, (((1,), (1,)), ((), ())), preferred_element_type=F32)


def _split3(a):
    hi = a.astype(BF16)
    r1 = a - hi.astype(F32)
    mid = r1.astype(BF16)
    lo = (r1 - mid.astype(F32)).astype(BF16)
    return hi, mid, lo


def _softplus(x):
    return jnp.maximum(x, 0.0) + jnp.log(1.0 + jnp.exp(-jnp.abs(x)))


def _silu(x):
    return x * (1.0 / (1.0 + jnp.exp(-x)))


def _inproj_kernel(x_ref, g_ref, wqv_ref, wk_ref, wz_ref, wxbc_ref, wsm_ref,
                   qt_ref, k_ref, vt_ref, z_ref, xbc_ref, sm_ref, wqvt_sc):
    @pl.when(pl.program_id(0) == 0)
    def _():
        wqvt_sc[...] = wqv_ref[...].astype(F32).T.astype(BF16)

    h = _rms(x_ref[...], g_ref[...]).astype(BF16)
    qt_ref[...] = (_dot_nt(wqvt_sc[0:2 * ATT_W, :], h) * (HEAD_DIM ** -0.5 * LOG2E)).astype(BF16)
    k_ref[...] = _dot(h, wk_ref[...]).astype(BF16)
    vt_ref[...] = _dot_nt(wqvt_sc[2 * ATT_W:, :], h).astype(BF16)
    z_ref[...] = _dot(h, wz_ref[...])
    xbc_ref[...] = _dot(h, wxbc_ref[...])
    sm_ref[...] = _dot(h, wsm_ref[...])


def _inproj(x, W, layer):
    t, d = x.shape
    tm = ROW_TILE
    row = lambda i: (i, 0)
    col = lambda i: (0, i)

    def wspec(name):
        start, width = IN_COLS[name]
        assert start % width == 0
        return pl.BlockSpec((None, d, width), lambda i: (layer, 0, start // width))
    outs = (
        jax.ShapeDtypeStruct((2 * ATT_W, t), BF16),
        jax.ShapeDtypeStruct((t, 2 * ATT_W), BF16),
        jax.ShapeDtypeStruct((2 * ATT_W, t), BF16),
        jax.ShapeDtypeStruct((t, SSD_INNER), F32),
        jax.ShapeDtypeStruct((t, SSD_XBC), F32),
        jax.ShapeDtypeStruct((t, SMALL_W), F32),
    )
    names = ("qv", "k", "z", "xbc", "sm")
    return pl.pallas_call(
        _inproj_kernel,
        out_shape=outs,
        grid=(t // tm,),
        in_specs=[pl.BlockSpec((tm, d), row), pl.BlockSpec((None, 1, d), lambda i: (layer, 0, 0))]
                 + [wspec(n) for n in names],
        out_specs=(pl.BlockSpec((2 * ATT_W, tm), col), pl.BlockSpec((tm, 2 * ATT_W), row),
                   pl.BlockSpec((2 * ATT_W, tm), col), pl.BlockSpec((tm, SSD_INNER), row),
                   pl.BlockSpec((tm, SSD_XBC), row), pl.BlockSpec((tm, SMALL_W), row)),
        scratch_shapes=[pltpu.VMEM((4 * ATT_W, d), BF16)],
        compiler_params=pltpu.CompilerParams(dimension_semantics=("arbitrary",), vmem_limit_bytes=VMEM_LIMIT),
        name="inproj",
    )(x, W["g_mix"], *([W["w_in"]] * len(names)))


def _flash_pipeline(i, kaug_sc, qaug_sc, vt_ref, shift_of, o_ref, s_sc, p_sc, acc_sc):
    heads = range(ATT_HEADS)
    own_win = i // WIN_BLOCKS

    def scores(h, win):
        c0 = pl.multiple_of(win * KEY_WIN, KEY_WIN)
        return _dot(kaug_sc[pl.ds(c0, KEY_WIN), h * PAIR_W:(h + 1) * PAIR_W], qaug_sc[h])

    ones = jnp.ones((ACC_ROWS - HEAD_DIM, KEY_WIN), BF16)

    def pv_stage(win, alpha):
        c0 = pl.multiple_of(win * KEY_WIN, KEY_WIN)
        for h in heads:
            vt_h = jnp.concatenate([vt_ref[h * HEAD_DIM:(h + 1) * HEAD_DIM, pl.ds(c0, KEY_WIN)], ones], axis=0)
            acc_sc[h] = alpha[h] * acc_sc[h] + _dot(vt_h, p_sc[h])

    def softmax_stage(blk, m, refill_win=None):
        m_out, alpha = [], []
        for h in heads:
            chunks = [pl.ds(r, SOFTMAX_ROWS) for r in range(0, KEY_WIN, SOFTMAX_ROWS)]
            top = s_sc[h, chunks[0], :]
            for rows in chunks[1:]:
                top = jnp.maximum(top, s_sc[h, rows, :])
            shift = shift_of(h, blk)
            m_new = jnp.maximum(m[h], jnp.max(top, axis=0, keepdims=True) + shift)
            base = m_new - shift
            for rows in chunks:
                p_sc[h, rows, :] = jnp.exp2(s_sc[h, rows, :] - base).astype(BF16)
            if refill_win is not None:
                s_sc[h] = scores(h, refill_win)
            m_out.append(m_new)
            alpha.append(jnp.exp2(m[h] - m_new))
        return tuple(m_out), tuple(alpha)

    key = lax.broadcasted_iota(jnp.int32, (KEY_WIN, KV_BLOCK), 0) + (own_win * KEY_WIN - i * KV_BLOCK)
    causal = key <= lax.broadcasted_iota(jnp.int32, (KEY_WIN, KV_BLOCK), 1)
    for h in heads:
        s_sc[h] = jnp.where(causal, scores(h, own_win), NEG)
    p_sc[...] = jnp.zeros(p_sc.shape, BF16)
    acc_sc[...] = jnp.zeros(acc_sc.shape, F32)
    row = lambda v: (jnp.full((1, KV_BLOCK), v, F32),) * ATT_HEADS
    carry = (row(NEG), row(1.0))

    def tile_win(t):
        return jnp.where(t < 1, own_win, t - 1)

    def step(t, carry):
        m, alpha = carry
        pv_stage(tile_win(t - 1), alpha)
        return softmax_stage(tile_win(t), m, refill_win=t)

    m, alpha = lax.fori_loop(0, own_win, step, carry)
    pv_stage(tile_win(own_win - 1), alpha)
    m, alpha = softmax_stage(tile_win(own_win), m)
    pv_stage(tile_win(own_win), alpha)
    out_t = jnp.concatenate([acc_sc[h][0:HEAD_DIM, :] / acc_sc[h][HEAD_DIM:HEAD_DIM + 1, :] for h in heads],
                            axis=0)
    o_ref[...] = out_t.T.astype(o_ref.dtype)


def _flash_scratch():
    return [pltpu.VMEM((ATT_HEADS, PAIR_W, KV_BLOCK), BF16),
            pltpu.VMEM((ATT_HEADS, KEY_WIN, KV_BLOCK), F32),
            pltpu.VMEM((ATT_HEADS, KEY_WIN, KV_BLOCK), BF16),
            pltpu.VMEM((ATT_HEADS, ACC_ROWS, KV_BLOCK), F32)]


def _f_lane(h):
    return (HEAD_DIM if h % 2 == 0 else 0) + AUG_POS + 3 * (h // 2)


def _own_lanes(h, shape):
    lane = lax.broadcasted_iota(jnp.int32, shape, 1)
    if h % 2 == 0:
        return lane < HEAD_DIM, lane - HEAD_DIM
    return lane >= HEAD_DIM, lane


def _stack_q(h, q_h, aug_rows):
    return jnp.concatenate([q_h, aug_rows] if h % 2 == 0 else [aug_rows, q_h], axis=0)


def _moba_kernel(qt_ref, k_ref, vt_ref, o_ref, kaug_sc, km_sc, qaug_sc, s_sc, p_sc, acc_sc, *, nb):
    i = pl.program_id(1)
    slopes = [LOG2E * 2.0 ** (-8.0 * (h + 1) / ATT_HEADS) for h in range(ATT_HEADS)]

    @pl.when(i == 0)
    def _():
        km_sc[...] = jnp.zeros(km_sc.shape, F32)
        offs = lax.broadcasted_iota(jnp.int32, (KV_BLOCK, PAIR_W), 0).astype(F32)
        km_row = lax.broadcasted_iota(jnp.int32, (HEAD_DIM, PAIR_W), 0)

        def build(n, carry):
            c0 = pl.multiple_of(n * KV_BLOCK, KV_BLOCK)
            in_win = jnp.full((KV_BLOCK, PAIR_W), n % WIN_BLOCKS, jnp.int32).astype(F32)
            for j in range(ATT_HEADS // 2):
                kp = k_ref[pl.ds(c0, KV_BLOCK), j * PAIR_W:(j + 1) * PAIR_W]
                mean = jnp.sum(kp.astype(F32), axis=0, keepdims=True) * (1.0 / KV_BLOCK)
                for h in (2 * j, 2 * j + 1):
                    own, slot = _own_lanes(h, (KV_BLOCK, PAIR_W))
                    aug = jnp.where(slot == n, 1.0,
                                    jnp.where((slot >= AUG_POS) & (slot < AUG_POS + 3), offs,
                                              jnp.where((slot >= AUG_POS + 3) & (slot < AUG_POS + 6), in_win, 0.0)))
                    kaug_sc[pl.ds(c0, KV_BLOCK), h * PAIR_W:(h + 1) * PAIR_W] = jnp.where(own, kp, aug.astype(BF16))
                    own_km, _ = _own_lanes(h, (HEAD_DIM, PAIR_W))
                    km_sc[h] = jnp.where(km_row == n, jnp.where(own_km, mean, 0.0), km_sc[h])
            return carry

        lax.fori_loop(0, nb, build, 0)

    blk_id = lax.broadcasted_iota(jnp.int32, (HEAD_DIM, KV_BLOCK), 0)
    blk_idf = blk_id.astype(F32)
    for h in range(ATT_HEADS):
        j = h // 2
        qp = qt_ref[j * PAIR_W:(j + 1) * PAIR_W, :]
        km_hi, km_mid, km_lo = _split3(km_sc[h])
        score = jnp.where(blk_id < i, _dot(km_hi, qp) + _dot(km_mid, qp) + _dot(km_lo, qp), NEG)
        sel = blk_id == i
        for _ in range(MOBA_TOPK):
            top = jnp.max(score, axis=0, keepdims=True)
            first = jnp.min(jnp.where(score == top, blk_idf, 1e9), axis=0, keepdims=True)
            hit = blk_idf == first
            sel = sel | (hit & (top > 0.5 * NEG))
            score = jnp.where(hit, 3.0 * NEG, score)
        terms = [t.astype(F32) for t in _split3(jnp.full((1, 1), slopes[h], F32))]
        terms += [t * KV_BLOCK for t in terms]
        aug = jnp.where(blk_id < AUG_POS, jnp.where(sel, 0.0, NEG), 0.0)
        for k, term in enumerate(terms):
            aug = jnp.where(blk_id == AUG_POS + k, term, aug)
        qaug_sc[h] = _stack_q(h, qt_ref[h * HEAD_DIM:(h + 1) * HEAD_DIM, :], aug.astype(BF16))

    def shift_of(h, win):
        return (slopes[h] * KEY_WIN) * jnp.full((1, KV_BLOCK), win - i // WIN_BLOCKS, jnp.int32).astype(F32)

    _flash_pipeline(i, kaug_sc, qaug_sc, vt_ref, shift_of, o_ref, s_sc, p_sc, acc_sc)


def _moba(qt, k, vt, batch, seq):
    nb = seq // KV_BLOCK
    return pl.pallas_call(
        functools.partial(_moba_kernel, nb=nb),
        out_shape=jax.ShapeDtypeStruct((batch * seq, ATT_W), BF16),
        grid=(batch, nb),
        in_specs=[pl.BlockSpec((ATT_W, KV_BLOCK), lambda b, i: (0, b * nb + i)),
                  pl.BlockSpec((seq, ATT_W), lambda b, i: (b, 0)),
                  pl.BlockSpec((ATT_W, seq), lambda b, i: (0, b))],
        out_specs=pl.BlockSpec((KV_BLOCK, ATT_W), lambda b, i: (b * nb + i, 0)),
        scratch_shapes=[pltpu.VMEM((seq, ATT_HEADS * PAIR_W), BF16),
                        pltpu.VMEM((ATT_HEADS, HEAD_DIM, PAIR_W), F32)] + _flash_scratch(),
        compiler_params=pltpu.CompilerParams(dimension_semantics=("arbitrary", "arbitrary"),
                                             vmem_limit_bytes=VMEM_LIMIT),
        name="moba",
    )(qt, k, vt)


def _fox_kernel(qt_ref, k_ref, vt_ref, sm_ref, fb_ref, o_ref, kaug_sc, cstart_sc, qaug_sc, s_sc, p_sc, acc_sc,
                *, nb):
    i = pl.program_id(1)

    @pl.when(i == 0)
    def _():
        r = lax.broadcasted_iota(jnp.int32, (KV_BLOCK, KV_BLOCK), 0)
        c = lax.broadcasted_iota(jnp.int32, (KV_BLOCK, KV_BLOCK), 1)
        tril = (c <= r).astype(BF16)
        lane = lax.broadcasted_iota(jnp.int32, (KV_BLOCK, PAIR_W), 1)
        term = (lane % HEAD_DIM - AUG_POS) % 3

        def scan(n, carry):
            c0 = pl.multiple_of(n * KV_BLOCK, KV_BLOCK)
            first = jnp.full((8, SMALL_W), n % WIN_BLOCKS, jnp.int32) == 0
            blk_start, win_start = carry[0], jnp.where(first, carry[0], carry[1])
            logf = -LOG2E * _softplus(-(sm_ref[pl.ds(c0, KV_BLOCK), :] + fb_ref[...]))
            hi, mid, lo = _split3(logf)
            cs = _dot(tril, hi) + _dot(tril, mid) + _dot(tril, lo)
            in_win = cs + (blk_start - win_start)[0:1, :]
            t_hi, t_mid, t_lo = (t.astype(F32) for t in _split3(in_win))
            terms = jnp.where(term == 0, t_hi, jnp.where(term == 1, t_mid, t_lo))
            cstart_sc[n // WIN_BLOCKS] = jnp.concatenate(
                [jnp.broadcast_to(win_start[0:1, _f_lane(h):_f_lane(h) + 1], (1, KV_BLOCK)) for h in range(ATT_HEADS)],
                axis=0)
            for h in range(ATT_HEADS):
                j = h // 2
                own, _ = _own_lanes(h, (KV_BLOCK, PAIR_W))
                mine = (lane >= _f_lane(h)) & (lane < _f_lane(h) + 3)
                kp = k_ref[pl.ds(c0, KV_BLOCK), j * PAIR_W:(j + 1) * PAIR_W]
                kaug_sc[pl.ds(c0, KV_BLOCK), h * PAIR_W:(h + 1) * PAIR_W] = jnp.where(
                    own, kp, jnp.where(mine, terms, 0.0).astype(BF16))
            return blk_start + jnp.broadcast_to(cs[KV_BLOCK - 1:KV_BLOCK, :], (8, SMALL_W)), win_start

        zero = jnp.broadcast_to(fb_ref[...] * 0.0, (8, SMALL_W))
        lax.fori_loop(0, nb, scan, (zero, zero))

    aug_id = lax.broadcasted_iota(jnp.int32, (HEAD_DIM, KV_BLOCK), 0)
    for h in range(ATT_HEADS):
        slot = _f_lane(h) % HEAD_DIM
        aug = jnp.where((aug_id >= slot) & (aug_id < slot + 3), -1.0, 0.0).astype(BF16)
        qaug_sc[h] = _stack_q(h, qt_ref[h * HEAD_DIM:(h + 1) * HEAD_DIM, :], aug)

    def shift_of(h, win):
        return -cstart_sc[win][h:h + 1, :]

    _flash_pipeline(i, kaug_sc, qaug_sc, vt_ref, shift_of, o_ref, s_sc, p_sc, acc_sc)


def _fox(qt, k, vt, sm, W, layer, batch, seq):
    nb = seq // KV_BLOCK
    return pl.pallas_call(
        functools.partial(_fox_kernel, nb=nb),
        out_shape=jax.ShapeDtypeStruct((batch * seq, ATT_W), BF16),
        grid=(batch, nb),
        in_specs=[pl.BlockSpec((ATT_W, KV_BLOCK), lambda b, i: (1, b * nb + i)),
                  pl.BlockSpec((seq, ATT_W), lambda b, i: (b, 1)),
                  pl.BlockSpec((ATT_W, seq), lambda b, i: (1, b)),
                  pl.BlockSpec((seq, SMALL_W), lambda b, i: (b, 0)),
                  pl.BlockSpec((None, 1, SMALL_W), lambda b, i: (layer, 0, 0))],
        out_specs=pl.BlockSpec((KV_BLOCK, ATT_W), lambda b, i: (b * nb + i, 0)),
        scratch_shapes=[pltpu.VMEM((seq, ATT_HEADS * PAIR_W), BF16),
                        pltpu.VMEM((nb // WIN_BLOCKS, ATT_HEADS, KV_BLOCK), F32)]
                       + _flash_scratch(),
        compiler_params=pltpu.CompilerParams(dimension_semantics=("arbitrary", "arbitrary"),
                                             vmem_limit_bytes=VMEM_LIMIT),
        name="fox",
    )(qt, k, vt, sm, W["fox_fb"])


def _ssd_kernel(xbc_ref, z_ref, sm_ref, cw_ref, cb_ref, dtb_ref, alog_ref, dskip_ref, nw_ref, o_ref,
                ext_sc, state_sc, y_sc):
    c = pl.program_id(1)
    L = SSD_CHUNK
    tail = 8

    @pl.when(c == 0)
    def _():
        ext_sc[0:tail, :] = jnp.zeros((tail, SSD_XBC), F32)
        state_sc[...] = jnp.zeros(state_sc.shape, F32)

    ext_sc[tail:tail + L, :] = xbc_ref[...]
    u = cb_ref[...]
    for k in range(SSD_CONV):
        back = SSD_CONV - 1 - k
        u = u + cw_ref[k:k + 1, :] * ext_sc[tail - back:tail - back + L, :]
    ext_sc[0:tail, :] = ext_sc[L:L + tail, :]
    xc = _silu(u)

    dt = _softplus(sm_ref[...] + dtb_ref[...])
    a = dt * (-jnp.exp(alog_ref[...]))
    r = lax.broadcasted_iota(jnp.int32, (L, L), 0)
    cidx = lax.broadcasted_iota(jnp.int32, (L, L), 1)
    causal = cidx <= r
    tril = causal.astype(BF16)
    a_hi, a_mid, a_lo = _split3(a)
    acs = (_dot(tril, a_hi) + _dot(tril, a_mid) + _dot(tril, a_lo)) * LOG2E
    acs_t = acs.T

    for g in range(SSD_GROUPS):
        bg = xc[:, SSD_INNER + g * SSD_N:SSD_INNER + (g + 1) * SSD_N].astype(BF16)
        cg = xc[:, SSD_INNER + (SSD_GROUPS + g) * SSD_N:SSD_INNER + (SSD_GROUPS + g + 1) * SSD_N].astype(BF16)
        bgt = xc[:, SSD_INNER + g * SSD_N:SSD_INNER + (g + 1) * SSD_N].T.astype(BF16)
        cb = jnp.where(causal, _dot_nt(cg, bg), 0.0)
        for h in range(g * (SSD_HEADS // SSD_GROUPS), (g + 1) * (SSD_HEADS // SSD_GROUPS)):
            cols = slice(h * SSD_P, (h + 1) * SSD_P)
            acol = acs[:, h:h + 1]
            total = acs[L - 1:L, h:h + 1]
            xs = xc[:, cols]
            xh = xs * dt[:, h:h + 1]
            decay = jnp.exp2(jnp.minimum(acol - acs_t[h:h + 1, :], 0.0))
            y_diag = _dot((cb * decay).astype(BF16), xh.astype(BF16))
            st = state_sc[:, cols]
            y_off = jnp.exp2(acol) * _dot(cg, st.astype(BF16))
            new = _dot(bgt, (xh * jnp.exp2(total - acol)).astype(BF16))
            state_sc[:, cols] = jnp.exp2(total) * st + new
            y_sc[:, cols] = y_diag + y_off + dskip_ref[:, cols] * xs

    y = y_sc[...] * _silu(z_ref[...])
    o_ref[...] = _rms(y, nw_ref[...]).astype(o_ref.dtype)


def _ssd(xbc, z, sm, W, layer, batch, seq):
    nc = seq // SSD_CHUNK
    L = SSD_CHUNK
    row = lambda b, c: (b * nc + c, 0)
    wspec = lambda a: pl.BlockSpec((None,) + a.shape[1:], lambda b, c: (layer, 0, 0))
    ws = tuple(W[n] for n in ("conv_w", "conv_b", "dt_bias", "a_log", "d_skip", "ssd_norm_w"))
    return pl.pallas_call(
        _ssd_kernel,
        out_shape=jax.ShapeDtypeStruct((batch * seq, SSD_INNER), BF16),
        grid=(batch, nc),
        in_specs=[pl.BlockSpec((L, SSD_XBC), row), pl.BlockSpec((L, SSD_INNER), row),
                  pl.BlockSpec((L, SMALL_W), row)] + [wspec(a) for a in ws],
        out_specs=pl.BlockSpec((L, SSD_INNER), row),
        scratch_shapes=[pltpu.VMEM((L + 8, SSD_XBC), F32),
                        pltpu.VMEM((SSD_N, SSD_INNER), F32),
                        pltpu.VMEM((L, SSD_INNER), F32)],
        compiler_params=pltpu.CompilerParams(dimension_semantics=("arbitrary", "arbitrary"),
                                             vmem_limit_bytes=VMEM_LIMIT),
        name="ssd",
    )(xbc, z, sm, *ws)


def _post_kernel(x_ref, a_ref, b_ref, c_ref, p_ref, wo_ref, gm_ref, w1_ref, w2_ref, gp_ref, wg_ref, wp_ref,
                 gf_ref, o_ref, acc_sc, h_sc, *, final):
    j = pl.program_id(1)

    @pl.when(j == 0)
    def _():
        wa = wo_ref[0:ATT_W, :]
        wb = wo_ref[ATT_W:ATT_W + SSD_INNER, :]
        wc = wo_ref[ATT_W + SSD_INNER:, :]
        x1 = x_ref[...] + _dot(a_ref[...], wa) + _dot(b_ref[...], wb) + _dot(c_ref[...], wc)
        acc_sc[...] = x1
        h_sc[...] = _rms(x1, gm_ref[...]).astype(BF16)

    u = jnp.maximum(_dot(h_sc[...], w1_ref[...]), 0.0)
    acc_sc[...] += _dot((u * u).astype(BF16), w2_ref[...])

    @pl.when(j == pl.num_programs(1) - 1)
    def _():
        x2 = acc_sc[...]
        gate = 1.0 / (1.0 + jnp.exp(-_dot(_rms(x2, gp_ref[...]).astype(BF16), wg_ref[...])))
        x3 = x2 + gate * _dot(p_ref[...].astype(BF16), wp_ref[...])
        o_ref[...] = _rms(x3, gf_ref[...]) if final else x3


def _post(x, a, b, c, W, layer, final):
    t, d = x.shape
    tm = POST_ROW_TILE
    d_ff = W["w_ff1"].shape[2]
    row = lambda i, j: (i, 0)
    wspec = lambda n: pl.BlockSpec((None,) + W[n].shape[1:], lambda i, j: (layer, 0, 0))
    return pl.pallas_call(
        functools.partial(_post_kernel, final=final),
        out_shape=jax.ShapeDtypeStruct((t, d), F32),
        grid=(t // tm, d_ff // FF_CHUNK),
        in_specs=[pl.BlockSpec((tm, d), row), pl.BlockSpec((tm, ATT_W), row), pl.BlockSpec((tm, SSD_INNER), row),
                  pl.BlockSpec((tm, ATT_W), row), pl.BlockSpec((None, tm, PLE_DIM), lambda i, j: (layer, i, 0)),
                  wspec("w_out"), wspec("g_mlp"),
                  pl.BlockSpec((None, d, FF_CHUNK), lambda i, j: (layer, 0, j)),
                  pl.BlockSpec((None, FF_CHUNK, d), lambda i, j: (layer, j, 0)),
                  wspec("g_ple"), wspec("w_ple_gate"), wspec("w_ple_proj"),
                  pl.BlockSpec(W["g_final"].shape, lambda i, j: (0, 0))],
        out_specs=pl.BlockSpec((tm, d), row),
        scratch_shapes=[pltpu.VMEM((tm, d), F32), pltpu.VMEM((tm, d), BF16)],
        compiler_params=pltpu.CompilerParams(dimension_semantics=("arbitrary", "arbitrary"),
                                             vmem_limit_bytes=VMEM_LIMIT),
        name="post",
    )(x, a, b, c, W["p"], W["w_out"], W["g_mlp"], W["w_ff1"], W["w_ff2"], W["g_ple"], W["w_ple_gate"],
      W["w_ple_proj"], W["g_final"])


def _prepare(p, g_mix, w_in, conv_w, conv_b, dt_bias, a_log, d_skip, ssd_norm_w, fox_f_bias, w_out,
             g_mlp, w_ff1, w_ff2, g_ple, w_ple_gate, w_ple_proj, g_final):
    depth, d, _ = w_in.shape
    o = 0
    mq, mk, mv = (w_in[:, :, o + k * ATT_W:o + (k + 1) * ATT_W] for k in range(3)); o += 3 * ATT_W
    wz = w_in[:, :, o:o + SSD_INNER]; o += SSD_INNER
    wxbc = w_in[:, :, o:o + SSD_XBC]; o += SSD_XBC
    wdt = w_in[:, :, o:o + SSD_HEADS]; o += SSD_HEADS
    fq, fk, fv = (w_in[:, :, o + k * ATT_W:o + (k + 1) * ATT_W] for k in range(3)); o += 3 * ATT_W
    wf = w_in[:, :, o:o + ATT_HEADS]
    src = [SSD_HEADS + ATT_HEADS] * SMALL_W
    for j in range(SSD_HEADS):
        src[j] = j
    for h in range(ATT_HEADS):
        for k in range(3):
            src[_f_lane(h) + k] = SSD_HEADS + h
    src = jnp.asarray(src, jnp.int32)
    small = jnp.take(jnp.concatenate([wdt, wf, jnp.zeros_like(wf[:, :, :1])], axis=2), src, axis=2)
    fb = jnp.take(jnp.concatenate([jnp.zeros((depth, SSD_HEADS), F32), fox_f_bias, jnp.zeros((depth, 1), F32)],
                                  axis=1), src, axis=1)
    w_all = jnp.concatenate([mq, fq, mv, fv, mk, fk, wz, wxbc, small], axis=2).astype(BF16)
    assert w_all.shape[2] == IN_COLS["sm"][0] + SMALL_W
    row = lambda a: a[:, None, :]
    pad = lambda a: jnp.pad(a, ((0, 0), (0, SMALL_W - a.shape[1])))
    return {
        "w_in": w_all, "g_mix": row(g_mix),
        "conv_w": conv_w, "conv_b": row(conv_b), "dt_bias": row(pad(dt_bias)), "a_log": row(pad(a_log)),
        "d_skip": row(jnp.repeat(d_skip, SSD_P, axis=1)), "ssd_norm_w": row(ssd_norm_w),
        "fox_fb": row(fb),
        "p": p.reshape(depth, -1, p.shape[-1]),
        "w_out": w_out.astype(BF16), "g_mlp": row(g_mlp), "w_ff1": w_ff1.astype(BF16), "w_ff2": w_ff2.astype(BF16),
        "g_ple": row(g_ple), "w_ple_gate": w_ple_gate.astype(BF16), "w_ple_proj": w_ple_proj.astype(BF16),
        "g_final": g_final[None, :],
    }


def kernel(x, p, g_mix, w_in, conv_w, conv_b, dt_bias, a_log, d_skip, ssd_norm_w, fox_f_bias, w_out, g_mlp, w_ff1, w_ff2, g_ple, w_ple_gate, w_ple_proj, g_final):
    batch, seq, d = x.shape
    depth = w_in.shape[0]
    assert seq % KEY_WIN == 0 and seq % SSD_CHUNK == 0
    assert (batch * seq) % ROW_TILE == 0 and (batch * seq) % POST_ROW_TILE == 0 and w_ff1.shape[2] % FF_CHUNK == 0
    assert seq // KV_BLOCK <= AUG_POS, "one-hot block ids use the augmentation slots below AUG_POS"
    xt = x.reshape(batch * seq, d)
    W = _prepare(p, g_mix, w_in, conv_w, conv_b, dt_bias, a_log, d_skip, ssd_norm_w, fox_f_bias, w_out,
                 g_mlp, w_ff1, w_ff2, g_ple, w_ple_gate, w_ple_proj, g_final)
    for i in range(depth):
        qt, k, vt, z, xbc, sm = _inproj(xt, W, i)
        out_a = _moba(qt, k, vt, batch, seq)
        out_b = _ssd(xbc, z, sm, W, i, batch, seq)
        out_c = _fox(qt, k, vt, sm, W, i, batch, seq)
        xt = _post(xt, out_a, out_b, out_c, W, i, final=(i == depth - 1))
    return xt.reshape(batch, seq, d)
```

```python
import functools

import jax
import jax.numpy as jnp
from jax import lax
from jax.experimental import pallas as pl
from jax.experimental.pallas import tpu as pltpu

F32 = jnp.float32
BF16 = jnp.bfloat16

HEAD_DIM = 64
ATT_HEADS = 4
ATT_W = ATT_HEADS * HEAD_DIM
PAIR_W = 2 * HEAD_DIM
KV_BLOCK = 256
WIN_BLOCKS = 2
KEY_WIN = WIN_BLOCKS * KV_BLOCK
SOFTMAX_ROWS = 64
MOBA_TOPK = 3
AUG_POS = 32
ACC_ROWS = HEAD_DIM + 16
LOG2E = 1.4426950408889634
UNDERFLOW = 170.0
BOUND_SLACK = 1.01
SSD_HEADS = 8
SSD_P = 64
SSD_INNER = SSD_HEADS * SSD_P
SSD_GROUPS = 2
SSD_N = 128
SSD_CONV = 4
SSD_CHUNK = 256
SSD_XBC = SSD_INNER + 2 * SSD_GROUPS * SSD_N
SMALL_W = 128
IN_COLS = {"qv": (0, 4 * ATT_W), "k": (4 * ATT_W, 2 * ATT_W), "z": (6 * ATT_W, SSD_INNER),
           "xbc": (6 * ATT_W + SSD_INNER, SSD_XBC), "sm": (6 * ATT_W + SSD_INNER + SSD_XBC, SMALL_W)}
PLE_DIM = 256
RMS_EPS = 1e-6
NEG = -1e30
LANES = 128
VMEM_LIMIT = 56 * 1024 * 1024

ROW_TILE = 1024
POST_ROW_TILE = 1024
FF_CHUNK = 1024


def _rms(x, g):
    return x * lax.rsqrt(jnp.mean(x * x, axis=-1, keepdims=True) + RMS_EPS) * g


def _dot(a, b):
    return jnp.dot(a, b, preferred_element_type=F32)


def _dot_nt(a, b):
    return lax.dot_general(a, b, (((1,), (1,)), ((), ())), preferred_element_type=F32)


def _split3(a):
    hi = a.astype(BF16)
    r1 = a - hi.astype(F32)
    mid = r1.astype(BF16)
    lo = (r1 - mid.astype(F32)).astype(BF16)
    return hi, mid, lo


def _softplus(x):
    return jnp.maximum(x, 0.0) + jnp.log(1.0 + jnp.exp(-jnp.abs(x)))


def _silu(x):
    return x * (1.0 / (1.0 + jnp.exp(-x)))


def _inproj_kernel(x_ref, g_ref, wqv_ref, wk_ref, wz_ref, wxbc_ref, wsm_ref,
                   qt_ref, k_ref, vt_ref, z_ref, xbc_ref, sm_ref, wqvt_sc):
    @pl.when(pl.program_id(0) == 0)
    def _():
        wqvt_sc[...] = wqv_ref[...].astype(F32).T.astype(BF16)

    h = _rms(x_ref[...], g_ref[...]).astype(BF16)
    qt_ref[...] = (_dot_nt(wqvt_sc[0:2 * ATT_W, :], h) * (HEAD_DIM ** -0.5 * LOG2E)).astype(BF16)
    k_ref[...] = _dot(h, wk_ref[...]).astype(BF16)
    vt_ref[...] = _dot_nt(wqvt_sc[2 * ATT_W:, :], h).astype(BF16)
    z_ref[...] = _dot(h, wz_ref[...])
    xbc_ref[...] = _dot(h, wxbc_ref[...])
    sm_ref[...] = _dot(h, wsm_ref[...])


def _inproj(x, W, layer):
    t, d = x.shape
    tm = ROW_TILE
    row = lambda i: (i, 0)
    col = lambda i: (0, i)

    def wspec(name):
        start, width = IN_COLS[name]
        assert start % width == 0
        return pl.BlockSpec((None, d, width), lambda i: (layer, 0, start // width))
    outs = (
        jax.ShapeDtypeStruct((2 * ATT_W, t), BF16),
        jax.ShapeDtypeStruct((t, 2 * ATT_W), BF16),
        jax.ShapeDtypeStruct((2 * ATT_W, t), BF16),
        jax.ShapeDtypeStruct((t, SSD_INNER), F32),
        jax.ShapeDtypeStruct((t, SSD_XBC), F32),
        jax.ShapeDtypeStruct((t, SMALL_W), F32),
    )
    names = ("qv", "k", "z", "xbc", "sm")
    return pl.pallas_call(
        _inproj_kernel,
        out_shape=outs,
        grid=(t // tm,),
        in_specs=[pl.BlockSpec((tm, d), row), pl.BlockSpec((None, 1, d), lambda i: (layer, 0, 0))]
                 + [wspec(n) for n in names],
        out_specs=(pl.BlockSpec((2 * ATT_W, tm), col), pl.BlockSpec((tm, 2 * ATT_W), row),
                   pl.BlockSpec((2 * ATT_W, tm), col), pl.BlockSpec((tm, SSD_INNER), row),
                   pl.BlockSpec((tm, SSD_XBC), row), pl.BlockSpec((tm, SMALL_W), row)),
        scratch_shapes=[pltpu.VMEM((4 * ATT_W, d), BF16)],
        compiler_params=pltpu.CompilerParams(dimension_semantics=("arbitrary",), vmem_limit_bytes=VMEM_LIMIT),
        name="inproj",
    )(x, W["g_mix"], *([W["w_in"]] * len(names)))


def _flash_pipeline(i, first_win, kaug_sc, qaug_sc, vt_ref, shift_of, o_ref, s_sc, p_sc, acc_sc):
    heads = range(ATT_HEADS)
    own_win = i // WIN_BLOCKS

    def scores(h, win):
        c0 = pl.multiple_of(win * KEY_WIN, KEY_WIN)
        return _dot(kaug_sc[pl.ds(c0, KEY_WIN), h * PAIR_W:(h + 1) * PAIR_W], qaug_sc[h])

    ones = jnp.ones((ACC_ROWS - HEAD_DIM, KEY_WIN), BF16)

    def pv_stage(win, alpha):
        c0 = pl.multiple_of(win * KEY_WIN, KEY_WIN)
        for h in heads:
            vt_h = jnp.concatenate([vt_ref[h * HEAD_DIM:(h + 1) * HEAD_DIM, pl.ds(c0, KEY_WIN)], ones], axis=0)
            acc_sc[h] = alpha[h] * acc_sc[h] + _dot(vt_h, p_sc[h])

    def softmax_stage(blk, m, refill_win=None):
        m_out, alpha = [], []
        for h in heads:
            chunks = [pl.ds(r, SOFTMAX_ROWS) for r in range(0, KEY_WIN, SOFTMAX_ROWS)]
            top = s_sc[h, chunks[0], :]
            for rows in chunks[1:]:
                top = jnp.maximum(top, s_sc[h, rows, :])
            shift = shift_of(h, blk)
            m_new = jnp.maximum(m[h], jnp.max(top, axis=0, keepdims=True) + shift)
            base = m_new - shift
            for rows in chunks:
                p_sc[h, rows, :] = jnp.exp2(s_sc[h, rows, :] - base).astype(BF16)
            if refill_win is not None:
                s_sc[h] = scores(h, refill_win)
            m_out.append(m_new)
            alpha.append(jnp.exp2(m[h] - m_new))
        return tuple(m_out), tuple(alpha)

    key = lax.broadcasted_iota(jnp.int32, (KEY_WIN, KV_BLOCK), 0) + (own_win * KEY_WIN - i * KV_BLOCK)
    causal = key <= lax.broadcasted_iota(jnp.int32, (KEY_WIN, KV_BLOCK), 1)
    for h in heads:
        s_sc[h] = jnp.where(causal, scores(h, own_win), NEG)
    p_sc[...] = jnp.zeros(p_sc.shape, BF16)
    acc_sc[...] = jnp.zeros(acc_sc.shape, F32)
    row = lambda v: (jnp.full((1, KV_BLOCK), v, F32),) * ATT_HEADS
    carry = (row(NEG), row(1.0))

    n_past = own_win - first_win

    def tile_win(t):
        return jnp.where(t < 1, own_win, first_win + t - 1)

    def step(t, carry):
        m, alpha = carry
        pv_stage(tile_win(t - 1), alpha)
        return softmax_stage(tile_win(t), m, refill_win=first_win + t)

    m, alpha = lax.fori_loop(0, n_past, step, carry)
    pv_stage(tile_win(n_past - 1), alpha)
    m, alpha = softmax_stage(tile_win(n_past), m)
    pv_stage(tile_win(n_past), alpha)
    out_t = jnp.concatenate([acc_sc[h][0:HEAD_DIM, :] / acc_sc[h][HEAD_DIM:HEAD_DIM + 1, :] for h in heads],
                            axis=0)
    o_ref[...] = out_t.T.astype(o_ref.dtype)


def _flash_scratch():
    return [pltpu.VMEM((ATT_HEADS, PAIR_W, KV_BLOCK), BF16),
            pltpu.VMEM((ATT_HEADS, KEY_WIN, KV_BLOCK), F32),
            pltpu.VMEM((ATT_HEADS, KEY_WIN, KV_BLOCK), BF16),
            pltpu.VMEM((ATT_HEADS, ACC_ROWS, KV_BLOCK), F32)]


def _f_lane(h):
    return (HEAD_DIM if h % 2 == 0 else 0) + AUG_POS + 3 * (h // 2)


def _own_lanes(h, shape):
    lane = lax.broadcasted_iota(jnp.int32, shape, 1)
    if h % 2 == 0:
        return lane < HEAD_DIM, lane - HEAD_DIM
    return lane >= HEAD_DIM, lane


def _stack_q(h, q_h, aug_rows):
    return jnp.concatenate([q_h, aug_rows] if h % 2 == 0 else [aug_rows, q_h], axis=0)


def _moba_kernel(qt_ref, k_ref, vt_ref, o_ref, kaug_sc, km_sc, qaug_sc, s_sc, p_sc, acc_sc, *, nb):
    i = pl.program_id(1)
    slopes = [LOG2E * 2.0 ** (-8.0 * (h + 1) / ATT_HEADS) for h in range(ATT_HEADS)]

    @pl.when(i == 0)
    def _():
        km_sc[...] = jnp.zeros(km_sc.shape, F32)
        offs = lax.broadcasted_iota(jnp.int32, (KV_BLOCK, PAIR_W), 0).astype(F32)
        km_row = lax.broadcasted_iota(jnp.int32, (HEAD_DIM, PAIR_W), 0)

        def build(n, carry):
            c0 = pl.multiple_of(n * KV_BLOCK, KV_BLOCK)
            in_win = jnp.full((KV_BLOCK, PAIR_W), n % WIN_BLOCKS, jnp.int32).astype(F32)
            for j in range(ATT_HEADS // 2):
                kp = k_ref[pl.ds(c0, KV_BLOCK), j * PAIR_W:(j + 1) * PAIR_W]
                mean = jnp.sum(kp.astype(F32), axis=0, keepdims=True) * (1.0 / KV_BLOCK)
                for h in (2 * j, 2 * j + 1):
                    own, slot = _own_lanes(h, (KV_BLOCK, PAIR_W))
                    aug = jnp.where(slot == n, 1.0,
                                    jnp.where((slot >= AUG_POS) & (slot < AUG_POS + 3), offs,
                                              jnp.where((slot >= AUG_POS + 3) & (slot < AUG_POS + 6), in_win, 0.0)))
                    kaug_sc[pl.ds(c0, KV_BLOCK), h * PAIR_W:(h + 1) * PAIR_W] = jnp.where(own, kp, aug.astype(BF16))
                    own_km, _ = _own_lanes(h, (HEAD_DIM, PAIR_W))
                    km_sc[h] = jnp.where(km_row == n, jnp.where(own_km, mean, 0.0), km_sc[h])
            return carry

        lax.fori_loop(0, nb, build, 0)

    blk_id = lax.broadcasted_iota(jnp.int32, (HEAD_DIM, KV_BLOCK), 0)
    blk_idf = blk_id.astype(F32)
    for h in range(ATT_HEADS):
        j = h // 2
        qp = qt_ref[j * PAIR_W:(j + 1) * PAIR_W, :]
        km_hi, km_mid, km_lo = _split3(km_sc[h])
        score = jnp.where(blk_id < i, _dot(km_hi, qp) + _dot(km_mid, qp) + _dot(km_lo, qp), NEG)
        sel = blk_id == i
        for _ in range(MOBA_TOPK):
            top = jnp.max(score, axis=0, keepdims=True)
            first = jnp.min(jnp.where(score == top, blk_idf, 1e9), axis=0, keepdims=True)
            hit = blk_idf == first
            sel = sel | (hit & (top > 0.5 * NEG))
            score = jnp.where(hit, 3.0 * NEG, score)
        terms = [t.astype(F32) for t in _split3(jnp.full((1, 1), slopes[h], F32))]
        terms += [t * KV_BLOCK for t in terms]
        aug = jnp.where(blk_id < AUG_POS, jnp.where(sel, 0.0, NEG), 0.0)
        for k, term in enumerate(terms):
            aug = jnp.where(blk_id == AUG_POS + k, term, aug)
        qaug_sc[h] = _stack_q(h, qt_ref[h * HEAD_DIM:(h + 1) * HEAD_DIM, :], aug.astype(BF16))

    def shift_of(h, win):
        return (slopes[h] * KEY_WIN) * jnp.full((1, KV_BLOCK), win - i // WIN_BLOCKS, jnp.int32).astype(F32)

    _flash_pipeline(i, 0, kaug_sc, qaug_sc, vt_ref, shift_of, o_ref, s_sc, p_sc, acc_sc)


def _moba(qt, k, vt, batch, seq):
    nb = seq // KV_BLOCK
    return pl.pallas_call(
        functools.partial(_moba_kernel, nb=nb),
        out_shape=jax.ShapeDtypeStruct((batch * seq, ATT_W), BF16),
        grid=(batch, nb),
        in_specs=[pl.BlockSpec((ATT_W, KV_BLOCK), lambda b, i: (0, b * nb + i)),
                  pl.BlockSpec((seq, ATT_W), lambda b, i: (b, 0)),
                  pl.BlockSpec((ATT_W, seq), lambda b, i: (0, b))],
        out_specs=pl.BlockSpec((KV_BLOCK, ATT_W), lambda b, i: (b * nb + i, 0)),
        scratch_shapes=[pltpu.VMEM((seq, ATT_HEADS * PAIR_W), BF16),
                        pltpu.VMEM((ATT_HEADS, HEAD_DIM, PAIR_W), F32)] + _flash_scratch(),
        compiler_params=pltpu.CompilerParams(dimension_semantics=("arbitrary", "arbitrary"),
                                             vmem_limit_bytes=VMEM_LIMIT),
        name="moba",
    )(qt, k, vt)


def _fox_kernel(qt_ref, k_ref, vt_ref, sm_ref, fb_ref, o_ref, kaug_sc, cstart_sc, cend_sc, knorm_sc, qaug_sc,
                s_sc, p_sc, acc_sc, *, nb):
    i = pl.program_id(1)

    @pl.when(i == 0)
    def _():
        r = lax.broadcasted_iota(jnp.int32, (KV_BLOCK, KV_BLOCK), 0)
        c = lax.broadcasted_iota(jnp.int32, (KV_BLOCK, KV_BLOCK), 1)
        tril = (c <= r).astype(BF16)
        lane = lax.broadcasted_iota(jnp.int32, (KV_BLOCK, PAIR_W), 1)
        term = (lane % HEAD_DIM - AUG_POS) % 3

        def head_rows(v):
            return jnp.concatenate(
                [jnp.broadcast_to(v[0:1, _f_lane(h):_f_lane(h) + 1], (1, KV_BLOCK)) for h in range(ATT_HEADS)], axis=0)

        def scan(n, carry):
            c0 = pl.multiple_of(n * KV_BLOCK, KV_BLOCK)
            first = jnp.full((8, SMALL_W), n % WIN_BLOCKS, jnp.int32) == 0
            blk_start, win_start = carry[0], jnp.where(first, carry[0], carry[1])
            logf = -LOG2E * _softplus(-(sm_ref[pl.ds(c0, KV_BLOCK), :] + fb_ref[...]))
            hi, mid, lo = _split3(logf)
            cs = _dot(tril, hi) + _dot(tril, mid) + _dot(tril, lo)
            in_win = cs + (blk_start - win_start)[0:1, :]
            t_hi, t_mid, t_lo = (t.astype(F32) for t in _split3(in_win))
            terms = jnp.where(term == 0, t_hi, jnp.where(term == 1, t_mid, t_lo))
            blk_end = blk_start + jnp.broadcast_to(cs[KV_BLOCK - 1:KV_BLOCK, :], (8, SMALL_W))
            cstart_sc[n // WIN_BLOCKS] = head_rows(win_start)
            cend_sc[n // WIN_BLOCKS] = head_rows(blk_end)
            knorm = list(carry[2:])
            for h in range(ATT_HEADS):
                j = h // 2
                own, _ = _own_lanes(h, (KV_BLOCK, PAIR_W))
                mine = (lane >= _f_lane(h)) & (lane < _f_lane(h) + 3)
                kp = k_ref[pl.ds(c0, KV_BLOCK), j * PAIR_W:(j + 1) * PAIR_W]
                kaug_sc[pl.ds(c0, KV_BLOCK), h * PAIR_W:(h + 1) * PAIR_W] = jnp.where(
                    own, kp, jnp.where(mine, terms, 0.0).astype(BF16))
                kf = jnp.where(own, kp.astype(F32), 0.0)
                sq = jnp.max(jnp.sum(kf * kf, axis=1, keepdims=True), axis=0, keepdims=True)
                knorm[h] = jnp.maximum(knorm[h], jnp.broadcast_to(sq, (8, SMALL_W)))
            return (blk_end, win_start, *knorm)

        zero = jnp.broadcast_to(fb_ref[...] * 0.0, (8, SMALL_W))
        done = lax.fori_loop(0, nb, scan, (zero,) * (2 + ATT_HEADS))
        knorm_sc[...] = jnp.concatenate(
            [jnp.concatenate([v[0:1, :]] * (KV_BLOCK // SMALL_W), axis=1) for v in done[2:]], axis=0)

    aug_id = lax.broadcasted_iota(jnp.int32, (HEAD_DIM, KV_BLOCK), 0)
    for h in range(ATT_HEADS):
        slot = _f_lane(h) % HEAD_DIM
        aug = jnp.where((aug_id >= slot) & (aug_id < slot + 3), -1.0, 0.0).astype(BF16)
        qaug_sc[h] = _stack_q(h, qt_ref[h * HEAD_DIM:(h + 1) * HEAD_DIM, :], aug)

    def shift_of(h, win):
        return -cstart_sc[win][h:h + 1, :]

    own_win = i // WIN_BLOCKS
    q2 = [jnp.max(jnp.sum(jnp.square(qt_ref[h * HEAD_DIM:(h + 1) * HEAD_DIM, :].astype(F32)), axis=0, keepdims=True),
                  axis=1, keepdims=True) for h in range(ATT_HEADS)]
    bound = jnp.sqrt(jnp.concatenate(q2, axis=0) * knorm_sc[...])
    reach = (2.0 * BOUND_SLACK) * bound + UNDERFLOW
    c_own = cstart_sc[own_win]
    count = jnp.zeros((1, KV_BLOCK), F32)
    for w in range(nb // WIN_BLOCKS):
        matters = jnp.max(jnp.where(cend_sc[w] - c_own < reach, 1.0, 0.0), axis=0, keepdims=True)
        count = count + jnp.where(w < own_win, matters, 0.0)
    first_win = own_win - count[0, 0].astype(jnp.int32)

    _flash_pipeline(i, first_win, kaug_sc, qaug_sc, vt_ref, shift_of, o_ref, s_sc, p_sc, acc_sc)


def _fox(qt, k, vt, sm, W, layer, batch, seq):
    nb = seq // KV_BLOCK
    return pl.pallas_call(
        functools.partial(_fox_kernel, nb=nb),
        out_shape=jax.ShapeDtypeStruct((batch * seq, ATT_W), BF16),
        grid=(batch, nb),
        in_specs=[pl.BlockSpec((ATT_W, KV_BLOCK), lambda b, i: (1, b * nb + i)),
                  pl.BlockSpec((seq, ATT_W), lambda b, i: (b, 1)),
                  pl.BlockSpec((ATT_W, seq), lambda b, i: (1, b)),
                  pl.BlockSpec((seq, SMALL_W), lambda b, i: (b, 0)),
                  pl.BlockSpec((None, 1, SMALL_W), lambda b, i: (layer, 0, 0))],
        out_specs=pl.BlockSpec((KV_BLOCK, ATT_W), lambda b, i: (b * nb + i, 0)),
        scratch_shapes=[pltpu.VMEM((seq, ATT_HEADS * PAIR_W), BF16),
                        pltpu.VMEM((nb // WIN_BLOCKS, ATT_HEADS, KV_BLOCK), F32),
                        pltpu.VMEM((nb // WIN_BLOCKS, ATT_HEADS, KV_BLOCK), F32),
                        pltpu.VMEM((ATT_HEADS, KV_BLOCK), F32)]
                       + _flash_scratch(),
        compiler_params=pltpu.CompilerParams(dimension_semantics=("arbitrary", "arbitrary"),
                                             vmem_limit_bytes=VMEM_LIMIT),
        name="fox",
    )(qt, k, vt, sm, W["fox_fb"])


def _ssd_kernel(xbc_ref, z_ref, sm_ref, cw_ref, cb_ref, dtb_ref, alog_ref, dskip_ref, nw_ref, o_ref,
                ext_sc, state_sc, y_sc):
    c = pl.program_id(1)
    L = SSD_CHUNK
    tail = 8

    @pl.when(c == 0)
    def _():
        ext_sc[0:tail, :] = jnp.zeros((tail, SSD_XBC), F32)
        state_sc[...] = jnp.zeros(state_sc.shape, F32)

    ext_sc[tail:tail + L, :] = xbc_ref[...]
    u = cb_ref[...]
    for k in range(SSD_CONV):
        back = SSD_CONV - 1 - k
        u = u + cw_ref[k:k + 1, :] * ext_sc[tail - back:tail - back + L, :]
    ext_sc[0:tail, :] = ext_sc[L:L + tail, :]
    xc = _silu(u)

    dt = _softplus(sm_ref[...] + dtb_ref[...])
    a = dt * (-jnp.exp(alog_ref[...]))
    r = lax.broadcasted_iota(jnp.int32, (L, L), 0)
    cidx = lax.broadcasted_iota(jnp.int32, (L, L), 1)
    causal = cidx <= r
    tril = causal.astype(BF16)
    a_hi, a_mid, a_lo = _split3(a)
    acs = (_dot(tril, a_hi) + _dot(tril, a_mid) + _dot(tril, a_lo)) * LOG2E
    acs_t = acs.T

    widen_m = (lax.broadcasted_iota(jnp.int32, (SMALL_W, SSD_INNER), 0)
               == lax.broadcasted_iota(jnp.int32, (SMALL_W, SSD_INNER), 1) // SSD_P).astype(BF16)

    def widen(v):
        hi, mid, lo = _split3(v)
        return _dot(hi, widen_m) + _dot(mid, widen_m) + _dot(lo, widen_m)

    total = acs[L - 1:L, :]
    xs = xc[:, 0:SSD_INNER]
    xh = xs * widen(dt)
    x_end = (xh * widen(jnp.exp2(total - acs))).astype(BF16)
    from_start = widen(jnp.exp2(acs))
    keep = widen(jnp.broadcast_to(jnp.exp2(total), (8, SMALL_W)))[0:1, :]
    lane = lax.broadcasted_iota(jnp.int32, (L, PAIR_W), 1)
    heads_per_group = SSD_HEADS // SSD_GROUPS
    gw = heads_per_group * SSD_P

    for g in range(SSD_GROUPS):
        bg = xc[:, SSD_INNER + g * SSD_N:SSD_INNER + (g + 1) * SSD_N].astype(BF16)
        cg = xc[:, SSD_INNER + (SSD_GROUPS + g) * SSD_N:SSD_INNER + (SSD_GROUPS + g + 1) * SSD_N].astype(BF16)
        bgt = xc[:, SSD_INNER + g * SSD_N:SSD_INNER + (g + 1) * SSD_N].T.astype(BF16)
        cb = jnp.where(causal, _dot_nt(cg, bg), 0.0)
        gcols = slice(g * gw, (g + 1) * gw)
        st = state_sc[:, gcols]
        y_off = from_start[:, gcols] * _dot(cg, st.astype(BF16))
        state_sc[:, gcols] = keep[:, gcols] * st + _dot(bgt, x_end[:, gcols])
        for pair in range(heads_per_group // 2):
            h0 = g * heads_per_group + 2 * pair
            pcols = slice(h0 * SSD_P, (h0 + 2) * SSD_P)
            xp = xh[:, pcols].astype(BF16)
            y_pair = []
            for h in (h0, h0 + 1):
                decay = jnp.exp2(jnp.minimum(acs[:, h:h + 1] - acs_t[h:h + 1, :], 0.0))
                y_pair.append(_dot((cb * decay).astype(BF16), xp))
            y_diag = jnp.where(lane < SSD_P, y_pair[0], y_pair[1])
            y_sc[:, pcols] = (y_diag + y_off[:, pair * PAIR_W:(pair + 1) * PAIR_W]
                              + dskip_ref[:, pcols] * xs[:, pcols])

    y = y_sc[...] * _silu(z_ref[...])
    o_ref[...] = _rms(y, nw_ref[...]).astype(o_ref.dtype)


def _ssd(xbc, z, sm, W, layer, batch, seq):
    nc = seq // SSD_CHUNK
    L = SSD_CHUNK
    row = lambda b, c: (b * nc + c, 0)
    wspec = lambda a: pl.BlockSpec((None,) + a.shape[1:], lambda b, c: (layer, 0, 0))
    ws = tuple(W[n] for n in ("conv_w", "conv_b", "dt_bias", "a_log", "d_skip", "ssd_norm_w"))
    return pl.pallas_call(
        _ssd_kernel,
        out_shape=jax.ShapeDtypeStruct((batch * seq, SSD_INNER), BF16),
        grid=(batch, nc),
        in_specs=[pl.BlockSpec((L, SSD_XBC), row), pl.BlockSpec((L, SSD_INNER), row),
                  pl.BlockSpec((L, SMALL_W), row)] + [wspec(a) for a in ws],
        out_specs=pl.BlockSpec((L, SSD_INNER), row),
        scratch_shapes=[pltpu.VMEM((L + 8, SSD_XBC), F32),
                        pltpu.VMEM((SSD_N, SSD_INNER), F32),
                        pltpu.VMEM((L, SSD_INNER), F32)],
        compiler_params=pltpu.CompilerParams(dimension_semantics=("arbitrary", "arbitrary"),
                                             vmem_limit_bytes=VMEM_LIMIT),
        name="ssd",
    )(xbc, z, sm, *ws)


def _post_kernel(x_ref, a_ref, b_ref, c_ref, p_ref, wo_ref, gm_ref, w1_ref, w2_ref, gp_ref, wg_ref, wp_ref,
                 gf_ref, o_ref, acc_sc, h_sc, *, final):
    j = pl.program_id(1)

    @pl.when(j == 0)
    def _():
        wa = wo_ref[0:ATT_W, :]
        wb = wo_ref[ATT_W:ATT_W + SSD_INNER, :]
        wc = wo_ref[ATT_W + SSD_INNER:, :]
        x1 = x_ref[...] + _dot(a_ref[...], wa) + _dot(b_ref[...], wb) + _dot(c_ref[...], wc)
        acc_sc[...] = x1
        h_sc[...] = _rms(x1, gm_ref[...]).astype(BF16)

    u = jnp.maximum(_dot(h_sc[...], w1_ref[...]), 0.0)
    acc_sc[...] += _dot((u * u).astype(BF16), w2_ref[...])

    @pl.when(j == pl.num_programs(1) - 1)
    def _():
        x2 = acc_sc[...]
        gate = 1.0 / (1.0 + jnp.exp(-_dot(_rms(x2, gp_ref[...]).astype(BF16), wg_ref[...])))
        x3 = x2 + gate * _dot(p_ref[...].astype(BF16), wp_ref[...])
        o_ref[...] = _rms(x3, gf_ref[...]) if final else x3


def _post(x, a, b, c, W, layer, final):
    t, d = x.shape
    tm = POST_ROW_TILE
    d_ff = W["w_ff1"].shape[2]
    row = lambda i, j: (i, 0)
    wspec = lambda n: pl.BlockSpec((None,) + W[n].shape[1:], lambda i, j: (layer, 0, 0))
    return pl.pallas_call(
        functools.partial(_post_kernel, final=final),
        out_shape=jax.ShapeDtypeStruct((t, d), F32),
        grid=(t // tm, d_ff // FF_CHUNK),
        in_specs=[pl.BlockSpec((tm, d), row), pl.BlockSpec((tm, ATT_W), row), pl.BlockSpec((tm, SSD_INNER), row),
                  pl.BlockSpec((tm, ATT_W), row), pl.BlockSpec((None, tm, PLE_DIM), lambda i, j: (layer, i, 0)),
                  wspec("w_out"), wspec("g_mlp"),
                  pl.BlockSpec((None, d, FF_CHUNK), lambda i, j: (layer, 0, j)),
                  pl.BlockSpec((None, FF_CHUNK, d), lambda i, j: (layer, j, 0)),
                  wspec("g_ple"), wspec("w_ple_gate"), wspec("w_ple_proj"),
                  pl.BlockSpec(W["g_final"].shape, lambda i, j: (0, 0))],
        out_specs=pl.BlockSpec((tm, d), row),
        scratch_shapes=[pltpu.VMEM((tm, d), F32), pltpu.VMEM((tm, d), BF16)],
        compiler_params=pltpu.CompilerParams(dimension_semantics=("arbitrary", "arbitrary"),
                                             vmem_limit_bytes=VMEM_LIMIT),
        name="post",
    )(x, a, b, c, W["p"], W["w_out"], W["g_mlp"], W["w_ff1"], W["w_ff2"], W["g_ple"], W["w_ple_gate"],
      W["w_ple_proj"], W["g_final"])


def _prepare(p, g_mix, w_in, conv_w, conv_b, dt_bias, a_log, d_skip, ssd_norm_w, fox_f_bias, w_out,
             g_mlp, w_ff1, w_ff2, g_ple, w_ple_gate, w_ple_proj, g_final):
    depth, d, _ = w_in.shape
    o = 0
    mq, mk, mv = (w_in[:, :, o + k * ATT_W:o + (k + 1) * ATT_W] for k in range(3)); o += 3 * ATT_W
    wz = w_in[:, :, o:o + SSD_INNER]; o += SSD_INNER
    wxbc = w_in[:, :, o:o + SSD_XBC]; o += SSD_XBC
    wdt = w_in[:, :, o:o + SSD_HEADS]; o += SSD_HEADS
    fq, fk, fv = (w_in[:, :, o + k * ATT_W:o + (k + 1) * ATT_W] for k in range(3)); o += 3 * ATT_W
    wf = w_in[:, :, o:o + ATT_HEADS]
    src = [SSD_HEADS + ATT_HEADS] * SMALL_W
    for j in range(SSD_HEADS):
        src[j] = j
    for h in range(ATT_HEADS):
        for k in range(3):
            src[_f_lane(h) + k] = SSD_HEADS + h
    src = jnp.asarray(src, jnp.int32)
    small = jnp.take(jnp.concatenate([wdt, wf, jnp.zeros_like(wf[:, :, :1])], axis=2), src, axis=2)
    fb = jnp.take(jnp.concatenate([jnp.zeros((depth, SSD_HEADS), F32), fox_f_bias, jnp.zeros((depth, 1), F32)],
                                  axis=1), src, axis=1)
    w_all = jnp.concatenate([mq, fq, mv, fv, mk, fk, wz, wxbc, small], axis=2).astype(BF16)
    assert w_all.shape[2] == IN_COLS["sm"][0] + SMALL_W
    row = lambda a: a[:, None, :]
    pad = lambda a: jnp.pad(a, ((0, 0), (0, SMALL_W - a.shape[1])))
    return {
        "w_in": w_all, "g_mix": row(g_mix),
        "conv_w": conv_w, "conv_b": row(conv_b), "dt_bias": row(pad(dt_bias)), "a_log": row(pad(a_log)),
        "d_skip": row(jnp.repeat(d_skip, SSD_P, axis=1)), "ssd_norm_w": row(ssd_norm_w),
        "fox_fb": row(fb),
        "p": p.reshape(depth, -1, p.shape[-1]),
        "w_out": w_out.astype(BF16), "g_mlp": row(g_mlp), "w_ff1": w_ff1.astype(BF16), "w_ff2": w_ff2.astype(BF16),
        "g_ple": row(g_ple), "w_ple_gate": w_ple_gate.astype(BF16), "w_ple_proj": w_ple_proj.astype(BF16),
        "g_final": g_final[None, :],
    }


def kernel(x, p, g_mix, w_in, conv_w, conv_b, dt_bias, a_log, d_skip, ssd_norm_w, fox_f_bias, w_out, g_mlp, w_ff1, w_ff2, g_ple, w_ple_gate, w_ple_proj, g_final):
    batch, seq, d = x.shape
    depth = w_in.shape[0]
    assert seq % KEY_WIN == 0 and seq % SSD_CHUNK == 0
    assert (batch * seq) % ROW_TILE == 0 and (batch * seq) % POST_ROW_TILE == 0 and w_ff1.shape[2] % FF_CHUNK == 0
    assert seq // KV_BLOCK <= AUG_POS, "one-hot block ids use the augmentation slots below AUG_POS"
    xt = x.reshape(batch * seq, d)
    W = _prepare(p, g_mix, w_in, conv_w, conv_b, dt_bias, a_log, d_skip, ssd_norm_w, fox_f_bias, w_out,
                 g_mlp, w_ff1, w_ff2, g_ple, w_ple_gate, w_ple_proj, g_final)
    for i in range(depth):
        qt, k, vt, z, xbc, sm = _inproj(xt, W, i)
        out_a = _moba(qt, k, vt, batch, seq)
        out_b = _ssd(xbc, z, sm, W, i, batch, seq)
        out_c = _fox(qt, k, vt, sm, W, i, batch, seq)
        xt = _post(xt, out_a, out_b, out_c, W, i, final=(i == depth - 1))
    return xt.reshape(batch, seq, d)
```

```python
import functools

import jax
import jax.numpy as jnp
from jax import lax
from jax.experimental import pallas as pl
from jax.experimental.pallas import tpu as pltpu

F32 = jnp.float32
BF16 = jnp.bfloat16

HEAD_DIM = 64
ATT_HEADS = 4
ATT_W = ATT_HEADS * HEAD_DIM
PAIR_W = 2 * HEAD_DIM
KV_BLOCK = 256
WIN_BLOCKS = 2
KEY_WIN = WIN_BLOCKS * KV_BLOCK
SOFTMAX_ROWS = 64
MOBA_TOPK = 3
AUG_POS = 32
ACC_ROWS = HEAD_DIM + 16
LOG2E = 1.4426950408889634
UNDERFLOW = 170.0
BOUND_SLACK = 1.01
SSD_HEADS = 8
SSD_P = 64
SSD_INNER = SSD_HEADS * SSD_P
SSD_GROUPS = 2
SSD_N = 128
SSD_CONV = 4
SSD_CHUNK = 256
SSD_XBC = SSD_INNER + 2 * SSD_GROUPS * SSD_N
SMALL_W = 128
IN_COLS = {"qv": (0, 4 * ATT_W), "k": (4 * ATT_W, 2 * ATT_W), "z": (6 * ATT_W, SSD_INNER),
           "xbc": (6 * ATT_W + SSD_INNER, SSD_XBC), "sm": (6 * ATT_W + SSD_INNER + SSD_XBC, SMALL_W)}
PLE_DIM = 256
RMS_EPS = 1e-6
NEG = -1e30
LANES = 128
VMEM_LIMIT = 56 * 1024 * 1024

ROW_TILE = 1024
POST_ROW_TILE = 1024
FF_CHUNK = 1024


def _rms(x, g):
    return x * lax.rsqrt(jnp.mean(x * x, axis=-1, keepdims=True) + RMS_EPS) * g


def _dot(a, b):
    return jnp.dot(a, b, preferred_element_type=F32)


def _dot_nt(a, b):
    return lax.dot_general(a, b, (((1,), (1,)), ((), ())), preferred_element_type=F32)


def _split3(a):
    hi = a.astype(BF16)
    r1 = a - hi.astype(F32)
    mid = r1.astype(BF16)
    lo = (r1 - mid.astype(F32)).astype(BF16)
    return hi, mid, lo


def _softplus(x):
    return jnp.maximum(x, 0.0) + jnp.log(1.0 + jnp.exp(-jnp.abs(x)))


def _silu(x):
    return x * (1.0 / (1.0 + jnp.exp(-x)))


def _inproj_kernel(x_ref, g_ref, wqv_ref, wk_ref, wz_ref, wxbc_ref, wsm_ref,
                   qt_ref, k_ref, vt_ref, z_ref, xbc_ref, sm_ref, wqvt_sc):
    @pl.when(pl.program_id(0) == 0)
    def _():
        wqvt_sc[...] = wqv_ref[...].astype(F32).T.astype(BF16)

    h = _rms(x_ref[...], g_ref[...]).astype(BF16)
    qt_ref[...] = (_dot_nt(wqvt_sc[0:2 * ATT_W, :], h) * (HEAD_DIM ** -0.5 * LOG2E)).astype(BF16)
    k_ref[...] = _dot(h, wk_ref[...]).astype(BF16)
    vt_ref[...] = _dot_nt(wqvt_sc[2 * ATT_W:, :], h).astype(BF16)
    z_ref[...] = _dot(h, wz_ref[...])
    xbc_ref[...] = _dot(h, wxbc_ref[...])
    sm_ref[...] = _dot(h, wsm_ref[...])


def _inproj(x, W, layer):
    t, d = x.shape
    tm = ROW_TILE
    row = lambda i: (i, 0)
    col = lambda i: (0, i)

    def wspec(name):
        start, width = IN_COLS[name]
        assert start % width == 0
        return pl.BlockSpec((None, d, width), lambda i: (layer, 0, start // width))
    outs = (
        jax.ShapeDtypeStruct((2 * ATT_W, t), BF16),
        jax.ShapeDtypeStruct((t, 2 * ATT_W), BF16),
        jax.ShapeDtypeStruct((2 * ATT_W, t), BF16),
        jax.ShapeDtypeStruct((t, SSD_INNER), F32),
        jax.ShapeDtypeStruct((t, SSD_XBC), F32),
        jax.ShapeDtypeStruct((t, SMALL_W), F32),
    )
    names = ("qv", "k", "z", "xbc", "sm")
    return pl.pallas_call(
        _inproj_kernel,
        out_shape=outs,
        grid=(t // tm,),
        in_specs=[pl.BlockSpec((tm, d), row), pl.BlockSpec((None, 1, d), lambda i: (layer, 0, 0))]
                 + [wspec(n) for n in names],
        out_specs=(pl.BlockSpec((2 * ATT_W, tm), col), pl.BlockSpec((tm, 2 * ATT_W), row),
                   pl.BlockSpec((2 * ATT_W, tm), col), pl.BlockSpec((tm, SSD_INNER), row),
                   pl.BlockSpec((tm, SSD_XBC), row), pl.BlockSpec((tm, SMALL_W), row)),
        scratch_shapes=[pltpu.VMEM((4 * ATT_W, d), BF16)],
        compiler_params=pltpu.CompilerParams(dimension_semantics=("arbitrary",), vmem_limit_bytes=VMEM_LIMIT),
        name="inproj",
    )(x, W["g_mix"], *([W["w_in"]] * len(names)))


class _Flash:
    def __init__(self, kaug_sc, qaug_sc, vt_ref, shift_of, s0_sc, s_sc, p_sc, acc_sc, m_sc, alpha_sc):
        self.kaug_sc, self.qaug_sc, self.vt_ref, self.shift_of = kaug_sc, qaug_sc, vt_ref, shift_of
        self.s0_sc, self.s_sc, self.p_sc, self.acc_sc = s0_sc, s_sc, p_sc, acc_sc
        self.m_sc, self.alpha_sc = m_sc, alpha_sc
        self.heads = range(ATT_HEADS)
        self.ones = jnp.ones((ACC_ROWS - HEAD_DIM, KEY_WIN), BF16)

    def reset(self):
        self.s_sc[...] = jnp.zeros(self.s_sc.shape, F32)
        self.p_sc[...] = jnp.zeros(self.p_sc.shape, BF16)
        self.acc_sc[...] = jnp.ones(self.acc_sc.shape, F32)
        self.m_sc[...] = jnp.zeros(self.m_sc.shape, F32)
        self.alpha_sc[...] = jnp.ones(self.alpha_sc.shape, F32)

    def scores(self, h, win):
        c0 = pl.multiple_of(win * KEY_WIN, KEY_WIN)
        return _dot(self.kaug_sc[pl.ds(c0, KEY_WIN), h * PAIR_W:(h + 1) * PAIR_W], self.qaug_sc[h])

    @staticmethod
    def plan(tile, first_win, live=True):
        own_win = tile // WIN_BLOCKS
        n_past = jnp.where(live, own_win - first_win, 0)
        dead = jnp.full((1, KV_BLOCK), n_past, jnp.int32) == 0
        return own_win, first_win, jnp.maximum(n_past, 1), jnp.where(dead, 1.0, 0.0)

    @staticmethod
    def tile_win(t, plan):
        own_win, first_win, _, _ = plan
        return jnp.where(t < 1, own_win, first_win + t - 1)

    def pv_stage(self, t, plan, alpha):
        c0 = pl.multiple_of(self.tile_win(t, plan) * KEY_WIN, KEY_WIN)
        for h in self.heads:
            vt_h = jnp.concatenate([self.vt_ref[h * HEAD_DIM:(h + 1) * HEAD_DIM, pl.ds(c0, KEY_WIN)], self.ones],
                                   axis=0)
            self.acc_sc[h] = alpha[h] * self.acc_sc[h] + _dot(vt_h, self.p_sc[h])

    def softmax_stage(self, t, tile, plan, m, src, refill):
        m_out, alpha = [], []
        for h in self.heads:
            chunks = [pl.ds(r, SOFTMAX_ROWS) for r in range(0, KEY_WIN, SOFTMAX_ROWS)]
            top = src[h, chunks[0], :]
            for rows in chunks[1:]:
                top = jnp.maximum(top, src[h, rows, :])
            shift = self.shift_of(h, self.tile_win(t, plan), tile)
            if src is self.s_sc:
                shift = shift + plan[3] * NEG
            m_new = jnp.maximum(m[h], jnp.max(top, axis=0, keepdims=True) + shift)
            base = m_new - shift
            for rows in chunks:
                self.p_sc[h, rows, :] = jnp.exp2(src[h, rows, :] - base).astype(BF16)
            if refill:
                self.s_sc[h] = self.scores(h, self.tile_win(t + 1, plan))
            m_out.append(m_new)
            alpha.append(jnp.exp2(m[h] - m_new))
        return tuple(m_out), tuple(alpha)

    def own_scores(self, tile):
        own_win = tile // WIN_BLOCKS
        key = lax.broadcasted_iota(jnp.int32, (KEY_WIN, KV_BLOCK), 0) + (own_win * KEY_WIN - tile * KV_BLOCK)
        causal = key <= lax.broadcasted_iota(jnp.int32, (KEY_WIN, KV_BLOCK), 1)
        for h in self.heads:
            self.s0_sc[h] = jnp.where(causal, self.scores(h, own_win), NEG)

    def finish(self, tile, first_win, o_ref):
        plan = self.plan(tile, first_win)
        n = plan[2]
        m = tuple(self.m_sc[h] for h in self.heads)
        alpha = tuple(self.alpha_sc[h] for h in self.heads)
        self.pv_stage(n - 1, plan, alpha)
        m, alpha = self.softmax_stage(n, tile, plan, m, self.s_sc, refill=False)
        self.pv_stage(n, plan, alpha)
        out_t = jnp.concatenate([self.acc_sc[h][0:HEAD_DIM, :] / self.acc_sc[h][HEAD_DIM:HEAD_DIM + 1, :]
                                 for h in self.heads], axis=0)
        o_ref[...] = out_t.T.astype(o_ref.dtype)

    def run(self, tile, first_win, live):
        plan = self.plan(tile, first_win, live)
        self.acc_sc[...] = jnp.zeros(self.acc_sc.shape, F32)
        start = (jnp.full((1, KV_BLOCK), NEG, F32),) * ATT_HEADS
        carry = self.softmax_stage(0, tile, plan, start, self.s0_sc, refill=True)

        def step(t, carry):
            m, alpha = carry
            self.pv_stage(t - 1, plan, alpha)
            return self.softmax_stage(t, tile, plan, m, self.s_sc, refill=True)

        m, alpha = lax.fori_loop(1, plan[2], step, carry)
        for h in self.heads:
            self.m_sc[h] = m[h]
            self.alpha_sc[h] = alpha[h]


def _flash_scratch():
    return [pltpu.VMEM((ATT_HEADS, PAIR_W, KV_BLOCK), BF16),
            pltpu.VMEM((ATT_HEADS, KEY_WIN, KV_BLOCK), F32),
            pltpu.VMEM((ATT_HEADS, KEY_WIN, KV_BLOCK), F32),
            pltpu.VMEM((ATT_HEADS, KEY_WIN, KV_BLOCK), BF16),
            pltpu.VMEM((ATT_HEADS, ACC_ROWS, KV_BLOCK), F32),
            pltpu.VMEM((ATT_HEADS, 1, KV_BLOCK), F32),
            pltpu.VMEM((ATT_HEADS, 1, KV_BLOCK), F32)]


def _f_lane(h):
    return (HEAD_DIM if h % 2 == 0 else 0) + AUG_POS + 3 * (h // 2)


def _own_lanes(h, shape):
    lane = lax.broadcasted_iota(jnp.int32, shape, 1)
    if h % 2 == 0:
        return lane < HEAD_DIM, lane - HEAD_DIM
    return lane >= HEAD_DIM, lane


def _stack_q(h, q_h, aug_rows):
    return jnp.concatenate([q_h, aug_rows] if h % 2 == 0 else [aug_rows, q_h], axis=0)


def _moba_kernel(qt_ref, k_ref, vt_ref, o_ref, kaug_sc, km_sc, qaug_sc, s0_sc, s_sc, p_sc, acc_sc, m_sc, alpha_sc,
                 *, nb):
    step = pl.program_id(1)
    i = jnp.minimum(step, nb - 1)
    slopes = [LOG2E * 2.0 ** (-8.0 * (h + 1) / ATT_HEADS) for h in range(ATT_HEADS)]

    def shift_of(h, win, tile):
        return (slopes[h] * KEY_WIN) * jnp.full((1, KV_BLOCK), win - tile // WIN_BLOCKS, jnp.int32).astype(F32)

    flash = _Flash(kaug_sc, qaug_sc, vt_ref, shift_of, s0_sc, s_sc, p_sc, acc_sc, m_sc, alpha_sc)

    @pl.when(step == 0)
    def _():
        flash.reset()
        km_sc[...] = jnp.zeros(km_sc.shape, F32)
        offs = lax.broadcasted_iota(jnp.int32, (KV_BLOCK, PAIR_W), 0).astype(F32)
        km_row = lax.broadcasted_iota(jnp.int32, (HEAD_DIM, PAIR_W), 0)

        def build(n, carry):
            c0 = pl.multiple_of(n * KV_BLOCK, KV_BLOCK)
            in_win = jnp.full((KV_BLOCK, PAIR_W), n % WIN_BLOCKS, jnp.int32).astype(F32)
            for j in range(ATT_HEADS // 2):
                kp = k_ref[pl.ds(c0, KV_BLOCK), j * PAIR_W:(j + 1) * PAIR_W]
                mean = jnp.sum(kp.astype(F32), axis=0, keepdims=True) * (1.0 / KV_BLOCK)
                for h in (2 * j, 2 * j + 1):
                    own, slot = _own_lanes(h, (KV_BLOCK, PAIR_W))
                    aug = jnp.where(slot == n, 1.0,
                                    jnp.where((slot >= AUG_POS) & (slot < AUG_POS + 3), offs,
                                              jnp.where((slot >= AUG_POS + 3) & (slot < AUG_POS + 6), in_win, 0.0)))
                    kaug_sc[pl.ds(c0, KV_BLOCK), h * PAIR_W:(h + 1) * PAIR_W] = jnp.where(own, kp, aug.astype(BF16))
                    own_km, _ = _own_lanes(h, (HEAD_DIM, PAIR_W))
                    km_sc[h] = jnp.where(km_row == n, jnp.where(own_km, mean, 0.0), km_sc[h])
            return carry

        lax.fori_loop(0, nb, build, 0)

    blk_id = lax.broadcasted_iota(jnp.int32, (HEAD_DIM, KV_BLOCK), 0)
    blk_idf = blk_id.astype(F32)
    for h in range(ATT_HEADS):
        j = h // 2
        qp = qt_ref[j * PAIR_W:(j + 1) * PAIR_W, :]
        km_hi, km_mid, km_lo = _split3(km_sc[h])
        score = jnp.where(blk_id < i, _dot(km_hi, qp) + _dot(km_mid, qp) + _dot(km_lo, qp), NEG)
        sel = blk_id == i
        for _ in range(MOBA_TOPK):
            top = jnp.max(score, axis=0, keepdims=True)
            first = jnp.min(jnp.where(score == top, blk_idf, 1e9), axis=0, keepdims=True)
            hit = blk_idf == first
            sel = sel | (hit & (top > 0.5 * NEG))
            score = jnp.where(hit, 3.0 * NEG, score)
        terms = [t.astype(F32) for t in _split3(jnp.full((1, 1), slopes[h], F32))]
        terms += [t * KV_BLOCK for t in terms]
        aug = jnp.where(blk_id < AUG_POS, jnp.where(sel, 0.0, NEG), 0.0)
        for k, term in enumerate(terms):
            aug = jnp.where(blk_id == AUG_POS + k, term, aug)
        qaug_sc[h] = _stack_q(h, qt_ref[h * HEAD_DIM:(h + 1) * HEAD_DIM, :], aug.astype(BF16))

    flash.finish(jnp.maximum(step - 1, 0), 0, o_ref)
    flash.own_scores(i)
    flash.run(i, 0, step < nb)


def _moba(qt, k, vt, batch, seq):
    nb = seq // KV_BLOCK
    return pl.pallas_call(
        functools.partial(_moba_kernel, nb=nb),
        out_shape=jax.ShapeDtypeStruct((batch * seq, ATT_W), BF16),
        grid=(batch, nb + 1),
        in_specs=[pl.BlockSpec((ATT_W, KV_BLOCK), lambda b, i: (0, b * nb + jnp.minimum(i, nb - 1))),
                  pl.BlockSpec((seq, ATT_W), lambda b, i: (b, 0)),
                  pl.BlockSpec((ATT_W, seq), lambda b, i: (0, b))],
        out_specs=pl.BlockSpec((KV_BLOCK, ATT_W), lambda b, i: (b * nb + jnp.maximum(i - 1, 0), 0)),
        scratch_shapes=[pltpu.VMEM((seq, ATT_HEADS * PAIR_W), BF16),
                        pltpu.VMEM((ATT_HEADS, HEAD_DIM, PAIR_W), F32)] + _flash_scratch(),
        compiler_params=pltpu.CompilerParams(dimension_semantics=("arbitrary", "arbitrary"),
                                             vmem_limit_bytes=VMEM_LIMIT),
        name="moba",
    )(qt, k, vt)


def _fox_kernel(qt_ref, k_ref, vt_ref, sm_ref, fb_ref, o_ref, kaug_sc, cstart_sc, cend_sc, knorm_sc, first_sc,
                qaug_sc, s0_sc, s_sc, p_sc, acc_sc, m_sc, alpha_sc, *, nb):
    step = pl.program_id(1)
    i = jnp.minimum(step, nb - 1)

    def shift_of(h, win, tile):
        return -cstart_sc[win][h:h + 1, :]

    flash = _Flash(kaug_sc, qaug_sc, vt_ref, shift_of, s0_sc, s_sc, p_sc, acc_sc, m_sc, alpha_sc)

    @pl.when(step == 0)
    def _():
        flash.reset()
        first_sc[0] = 0
        r = lax.broadcasted_iota(jnp.int32, (KV_BLOCK, KV_BLOCK), 0)
        c = lax.broadcasted_iota(jnp.int32, (KV_BLOCK, KV_BLOCK), 1)
        tril = (c <= r).astype(BF16)
        lane = lax.broadcasted_iota(jnp.int32, (KV_BLOCK, PAIR_W), 1)
        term = (lane % HEAD_DIM - AUG_POS) % 3

        def head_rows(v):
            return jnp.concatenate(
                [jnp.broadcast_to(v[0:1, _f_lane(h):_f_lane(h) + 1], (1, KV_BLOCK)) for h in range(ATT_HEADS)], axis=0)

        def scan(n, carry):
            c0 = pl.multiple_of(n * KV_BLOCK, KV_BLOCK)
            first = jnp.full((8, SMALL_W), n % WIN_BLOCKS, jnp.int32) == 0
            blk_start, win_start = carry[0], jnp.where(first, carry[0], carry[1])
            logf = -LOG2E * _softplus(-(sm_ref[pl.ds(c0, KV_BLOCK), :] + fb_ref[...]))
            hi, mid, lo = _split3(logf)
            cs = _dot(tril, hi) + _dot(tril, mid) + _dot(tril, lo)
            in_win = cs + (blk_start - win_start)[0:1, :]
            t_hi, t_mid, t_lo = (t.astype(F32) for t in _split3(in_win))
            terms = jnp.where(term == 0, t_hi, jnp.where(term == 1, t_mid, t_lo))
            blk_end = blk_start + jnp.broadcast_to(cs[KV_BLOCK - 1:KV_BLOCK, :], (8, SMALL_W))
            cstart_sc[n // WIN_BLOCKS] = head_rows(win_start)
            cend_sc[n // WIN_BLOCKS] = head_rows(blk_end)
            knorm = list(carry[2:])
            for h in range(ATT_HEADS):
                j = h // 2
                own, _ = _own_lanes(h, (KV_BLOCK, PAIR_W))
                mine = (lane >= _f_lane(h)) & (lane < _f_lane(h) + 3)
                kp = k_ref[pl.ds(c0, KV_BLOCK), j * PAIR_W:(j + 1) * PAIR_W]
                kaug_sc[pl.ds(c0, KV_BLOCK), h * PAIR_W:(h + 1) * PAIR_W] = jnp.where(
                    own, kp, jnp.where(mine, terms, 0.0).astype(BF16))
                kf = jnp.where(own, kp.astype(F32), 0.0)
                sq = jnp.max(jnp.sum(kf * kf, axis=1, keepdims=True), axis=0, keepdims=True)
                knorm[h] = jnp.maximum(knorm[h], jnp.broadcast_to(sq, (8, SMALL_W)))
            return (blk_end, win_start, *knorm)

        zero = jnp.broadcast_to(fb_ref[...] * 0.0, (8, SMALL_W))
        done = lax.fori_loop(0, nb, scan, (zero,) * (2 + ATT_HEADS))
        knorm_sc[...] = jnp.concatenate(
            [jnp.concatenate([v[0:1, :]] * (KV_BLOCK // SMALL_W), axis=1) for v in done[2:]], axis=0)

    aug_id = lax.broadcasted_iota(jnp.int32, (HEAD_DIM, KV_BLOCK), 0)
    for h in range(ATT_HEADS):
        slot = _f_lane(h) % HEAD_DIM
        aug = jnp.where((aug_id >= slot) & (aug_id < slot + 3), -1.0, 0.0).astype(BF16)
        qaug_sc[h] = _stack_q(h, qt_ref[h * HEAD_DIM:(h + 1) * HEAD_DIM, :], aug)

    own_win = i // WIN_BLOCKS
    q2 = [jnp.max(jnp.sum(jnp.square(qt_ref[h * HEAD_DIM:(h + 1) * HEAD_DIM, :].astype(F32)), axis=0, keepdims=True),
                  axis=1, keepdims=True) for h in range(ATT_HEADS)]
    bound = jnp.sqrt(jnp.concatenate(q2, axis=0) * knorm_sc[...])
    reach = (2.0 * BOUND_SLACK) * bound + UNDERFLOW
    c_own = cstart_sc[own_win]
    count = jnp.zeros((1, KV_BLOCK), F32)
    for w in range(nb // WIN_BLOCKS):
        matters = jnp.max(jnp.where(cend_sc[w] - c_own < reach, 1.0, 0.0), axis=0, keepdims=True)
        count = count + jnp.where(w < own_win, matters, 0.0)
    first_win = own_win - count[0, 0].astype(jnp.int32)

    flash.finish(jnp.maximum(step - 1, 0), first_sc[0], o_ref)
    flash.own_scores(i)
    flash.run(i, first_win, step < nb)
    first_sc[0] = first_win


def _fox(qt, k, vt, sm, W, layer, batch, seq):
    nb = seq // KV_BLOCK
    return pl.pallas_call(
        functools.partial(_fox_kernel, nb=nb),
        out_shape=jax.ShapeDtypeStruct((batch * seq, ATT_W), BF16),
        grid=(batch, nb + 1),
        in_specs=[pl.BlockSpec((ATT_W, KV_BLOCK), lambda b, i: (1, b * nb + jnp.minimum(i, nb - 1))),
                  pl.BlockSpec((seq, ATT_W), lambda b, i: (b, 1)),
                  pl.BlockSpec((ATT_W, seq), lambda b, i: (1, b)),
                  pl.BlockSpec((seq, SMALL_W), lambda b, i: (b, 0)),
                  pl.BlockSpec((None, 1, SMALL_W), lambda b, i: (layer, 0, 0))],
        out_specs=pl.BlockSpec((KV_BLOCK, ATT_W), lambda b, i: (b * nb + jnp.maximum(i - 1, 0), 0)),
        scratch_shapes=[pltpu.VMEM((seq, ATT_HEADS * PAIR_W), BF16),
                        pltpu.VMEM((nb // WIN_BLOCKS, ATT_HEADS, KV_BLOCK), F32),
                        pltpu.VMEM((nb // WIN_BLOCKS, ATT_HEADS, KV_BLOCK), F32),
                        pltpu.VMEM((ATT_HEADS, KV_BLOCK), F32),
                        pltpu.SMEM((1,), jnp.int32)]
                       + _flash_scratch(),
        compiler_params=pltpu.CompilerParams(dimension_semantics=("arbitrary", "arbitrary"),
                                             vmem_limit_bytes=VMEM_LIMIT),
        name="fox",
    )(qt, k, vt, sm, W["fox_fb"])


def _ssd_kernel(xbc_ref, z_ref, sm_ref, cw_ref, cb_ref, dtb_ref, alog_ref, dskip_ref, nw_ref, o_ref,
                ext_sc, state_sc, y_sc):
    c = pl.program_id(1)
    L = SSD_CHUNK
    tail = 8

    @pl.when(c == 0)
    def _():
        ext_sc[0:tail, :] = jnp.zeros((tail, SSD_XBC), F32)
        state_sc[...] = jnp.zeros(state_sc.shape, F32)

    ext_sc[tail:tail + L, :] = xbc_ref[...]
    u = cb_ref[...]
    for k in range(SSD_CONV):
        back = SSD_CONV - 1 - k
        u = u + cw_ref[k:k + 1, :] * ext_sc[tail - back:tail - back + L, :]
    ext_sc[0:tail, :] = ext_sc[L:L + tail, :]
    xc = _silu(u)

    dt = _softplus(sm_ref[...] + dtb_ref[...])
    a = dt * (-jnp.exp(alog_ref[...]))
    r = lax.broadcasted_iota(jnp.int32, (L, L), 0)
    cidx = lax.broadcasted_iota(jnp.int32, (L, L), 1)
    causal = cidx <= r
    tril = causal.astype(BF16)
    a_hi, a_mid, a_lo = _split3(a)
    acs = (_dot(tril, a_hi) + _dot(tril, a_mid) + _dot(tril, a_lo)) * LOG2E
    acs_t = acs.T

    widen_m = (lax.broadcasted_iota(jnp.int32, (SMALL_W, SSD_INNER), 0)
               == lax.broadcasted_iota(jnp.int32, (SMALL_W, SSD_INNER), 1) // SSD_P).astype(BF16)

    def widen(v):
        hi, mid, lo = _split3(v)
        return _dot(hi, widen_m) + _dot(mid, widen_m) + _dot(lo, widen_m)

    total = acs[L - 1:L, :]
    xs = xc[:, 0:SSD_INNER]
    xh = xs * widen(dt)
    x_end = (xh * widen(jnp.exp2(total - acs))).astype(BF16)
    from_start = widen(jnp.exp2(acs))
    keep = widen(jnp.broadcast_to(jnp.exp2(total), (8, SMALL_W)))[0:1, :]
    lane = lax.broadcasted_iota(jnp.int32, (L, PAIR_W), 1)
    heads_per_group = SSD_HEADS // SSD_GROUPS
    gw = heads_per_group * SSD_P

    for g in range(SSD_GROUPS):
        bg = xc[:, SSD_INNER + g * SSD_N:SSD_INNER + (g + 1) * SSD_N].astype(BF16)
        cg = xc[:, SSD_INNER + (SSD_GROUPS + g) * SSD_N:SSD_INNER + (SSD_GROUPS + g + 1) * SSD_N].astype(BF16)
        bgt = xc[:, SSD_INNER + g * SSD_N:SSD_INNER + (g + 1) * SSD_N].T.astype(BF16)
        cb = jnp.where(causal, _dot_nt(cg, bg), 0.0)
        gcols = slice(g * gw, (g + 1) * gw)
        st = state_sc[:, gcols]
        y_off = from_start[:, gcols] * _dot(cg, st.astype(BF16))
        state_sc[:, gcols] = keep[:, gcols] * st + _dot(bgt, x_end[:, gcols])
        for pair in range(heads_per_group // 2):
            h0 = g * heads_per_group + 2 * pair
            pcols = slice(h0 * SSD_P, (h0 + 2) * SSD_P)
            xp = xh[:, pcols].astype(BF16)
            y_pair = []
            for h in (h0, h0 + 1):
                decay = jnp.exp2(jnp.minimum(acs[:, h:h + 1] - acs_t[h:h + 1, :], 0.0))
                y_pair.append(_dot((cb * decay).astype(BF16), xp))
            y_diag = jnp.where(lane < SSD_P, y_pair[0], y_pair[1])
            y_sc[:, pcols] = (y_diag + y_off[:, pair * PAIR_W:(pair + 1) * PAIR_W]
                              + dskip_ref[:, pcols] * xs[:, pcols])

    y = y_sc[...] * _silu(z_ref[...])
    o_ref[...] = _rms(y, nw_ref[...]).astype(o_ref.dtype)


def _ssd(xbc, z, sm, W, layer, batch, seq):
    nc = seq // SSD_CHUNK
    L = SSD_CHUNK
    row = lambda b, c: (b * nc + c, 0)
    wspec = lambda a: pl.BlockSpec((None,) + a.shape[1:], lambda b, c: (layer, 0, 0))
    ws = tuple(W[n] for n in ("conv_w", "conv_b", "dt_bias", "a_log", "d_skip", "ssd_norm_w"))
    return pl.pallas_call(
        _ssd_kernel,
        out_shape=jax.ShapeDtypeStruct((batch * seq, SSD_INNER), BF16),
        grid=(batch, nc),
        in_specs=[pl.BlockSpec((L, SSD_XBC), row), pl.BlockSpec((L, SSD_INNER), row),
                  pl.BlockSpec((L, SMALL_W), row)] + [wspec(a) for a in ws],
        out_specs=pl.BlockSpec((L, SSD_INNER), row),
        scratch_shapes=[pltpu.VMEM((L + 8, SSD_XBC), F32),
                        pltpu.VMEM((SSD_N, SSD_INNER), F32),
                        pltpu.VMEM((L, SSD_INNER), F32)],
        compiler_params=pltpu.CompilerParams(dimension_semantics=("arbitrary", "arbitrary"),
                                             vmem_limit_bytes=VMEM_LIMIT),
        name="ssd",
    )(xbc, z, sm, *ws)


def _post_kernel(x_ref, a_ref, b_ref, c_ref, p_ref, wo_ref, gm_ref, w1_ref, w2_ref, gp_ref, wg_ref, wp_ref,
                 gf_ref, o_ref, acc_sc, h_sc, *, final):
    j = pl.program_id(1)

    @pl.when(j == 0)
    def _():
        wa = wo_ref[0:ATT_W, :]
        wb = wo_ref[ATT_W:ATT_W + SSD_INNER, :]
        wc = wo_ref[ATT_W + SSD_INNER:, :]
        x1 = x_ref[...] + _dot(a_ref[...], wa) + _dot(b_ref[...], wb) + _dot(c_ref[...], wc)
        acc_sc[...] = x1
        h_sc[...] = _rms(x1, gm_ref[...]).astype(BF16)

    u = jnp.maximum(_dot(h_sc[...], w1_ref[...]), 0.0)
    acc_sc[...] += _dot((u * u).astype(BF16), w2_ref[...])

    @pl.when(j == pl.num_programs(1) - 1)
    def _():
        x2 = acc_sc[...]
        gate = 1.0 / (1.0 + jnp.exp(-_dot(_rms(x2, gp_ref[...]).astype(BF16), wg_ref[...])))
        x3 = x2 + gate * _dot(p_ref[...].astype(BF16), wp_ref[...])
        o_ref[...] = _rms(x3, gf_ref[...]) if final else x3


def _post(x, a, b, c, W, layer, final):
    t, d = x.shape
    tm = POST_ROW_TILE
    d_ff = W["w_ff1"].shape[2]
    row = lambda i, j: (i, 0)
    wspec = lambda n: pl.BlockSpec((None,) + W[n].shape[1:], lambda i, j: (layer, 0, 0))
    return pl.pallas_call(
        functools.partial(_post_kernel, final=final),
        out_shape=jax.ShapeDtypeStruct((t, d), F32),
        grid=(t // tm, d_ff // FF_CHUNK),
        in_specs=[pl.BlockSpec((tm, d), row), pl.BlockSpec((tm, ATT_W), row), pl.BlockSpec((tm, SSD_INNER), row),
                  pl.BlockSpec((tm, ATT_W), row), pl.BlockSpec((None, tm, PLE_DIM), lambda i, j: (layer, i, 0)),
                  wspec("w_out"), wspec("g_mlp"),
                  pl.BlockSpec((None, d, FF_CHUNK), lambda i, j: (layer, 0, j)),
                  pl.BlockSpec((None, FF_CHUNK, d), lambda i, j: (layer, j, 0)),
                  wspec("g_ple"), wspec("w_ple_gate"), wspec("w_ple_proj"),
                  pl.BlockSpec(W["g_final"].shape, lambda i, j: (0, 0))],
        out_specs=pl.BlockSpec((tm, d), row),
        scratch_shapes=[pltpu.VMEM((tm, d), F32), pltpu.VMEM((tm, d), BF16)],
        compiler_params=pltpu.CompilerParams(dimension_semantics=("arbitrary", "arbitrary"),
                                             vmem_limit_bytes=VMEM_LIMIT),
        name="post",
    )(x, a, b, c, W["p"], W["w_out"], W["g_mlp"], W["w_ff1"], W["w_ff2"], W["g_ple"], W["w_ple_gate"],
      W["w_ple_proj"], W["g_final"])


def _prepare(p, g_mix, w_in, conv_w, conv_b, dt_bias, a_log, d_skip, ssd_norm_w, fox_f_bias, w_out,
             g_mlp, w_ff1, w_ff2, g_ple, w_ple_gate, w_ple_proj, g_final):
    depth, d, _ = w_in.shape
    o = 0
    mq, mk, mv = (w_in[:, :, o + k * ATT_W:o + (k + 1) * ATT_W] for k in range(3)); o += 3 * ATT_W
    wz = w_in[:, :, o:o + SSD_INNER]; o += SSD_INNER
    wxbc = w_in[:, :, o:o + SSD_XBC]; o += SSD_XBC
    wdt = w_in[:, :, o:o + SSD_HEADS]; o += SSD_HEADS
    fq, fk, fv = (w_in[:, :, o + k * ATT_W:o + (k + 1) * ATT_W] for k in range(3)); o += 3 * ATT_W
    wf = w_in[:, :, o:o + ATT_HEADS]
    src = [SSD_HEADS + ATT_HEADS] * SMALL_W
    for j in range(SSD_HEADS):
        src[j] = j
    for h in range(ATT_HEADS):
        for k in range(3):
            src[_f_lane(h) + k] = SSD_HEADS + h
    src = jnp.asarray(src, jnp.int32)
    small = jnp.take(jnp.concatenate([wdt, wf, jnp.zeros_like(wf[:, :, :1])], axis=2), src, axis=2)
    fb = jnp.take(jnp.concatenate([jnp.zeros((depth, SSD_HEADS), F32), fox_f_bias, jnp.zeros((depth, 1), F32)],
                                  axis=1), src, axis=1)
    w_all = jnp.concatenate([mq, fq, mv, fv, mk, fk, wz, wxbc, small], axis=2).astype(BF16)
    assert w_all.shape[2] == IN_COLS["sm"][0] + SMALL_W
    row = lambda a: a[:, None, :]
    pad = lambda a: jnp.pad(a, ((0, 0), (0, SMALL_W - a.shape[1])))
    return {
        "w_in": w_all, "g_mix": row(g_mix),
        "conv_w": conv_w, "conv_b": row(conv_b), "dt_bias": row(pad(dt_bias)), "a_log": row(pad(a_log)),
        "d_skip": row(jnp.repeat(d_skip, SSD_P, axis=1)), "ssd_norm_w": row(ssd_norm_w),
        "fox_fb": row(fb),
        "p": p.reshape(depth, -1, p.shape[-1]),
        "w_out": w_out.astype(BF16), "g_mlp": row(g_mlp), "w_ff1": w_ff1.astype(BF16), "w_ff2": w_ff2.astype(BF16),
        "g_ple": row(g_ple), "w_ple_gate": w_ple_gate.astype(BF16), "w_ple_proj": w_ple_proj.astype(BF16),
        "g_final": g_final[None, :],
    }


def kernel(x, p, g_mix, w_in, conv_w, conv_b, dt_bias, a_log, d_skip, ssd_norm_w, fox_f_bias, w_out, g_mlp, w_ff1, w_ff2, g_ple, w_ple_gate, w_ple_proj, g_final):
    batch, seq, d = x.shape
    depth = w_in.shape[0]
    assert seq % KEY_WIN == 0 and seq % SSD_CHUNK == 0
    assert (batch * seq) % ROW_TILE == 0 and (batch * seq) % POST_ROW_TILE == 0 and w_ff1.shape[2] % FF_CHUNK == 0
    assert seq // KV_BLOCK <= AUG_POS, "one-hot block ids use the augmentation slots below AUG_POS"
    xt = x.reshape(batch * seq, d)
    W = _prepare(p, g_mix, w_in, conv_w, conv_b, dt_bias, a_log, d_skip, ssd_norm_w, fox_f_bias, w_out,
                 g_mlp, w_ff1, w_ff2, g_ple, w_ple_gate, w_ple_proj, g_final)
    for i in range(depth):
        qt, k, vt, z, xbc, sm = _inproj(xt, W, i)
        out_a = _moba(qt, k, vt, batch, seq)
        out_b = _ssd(xbc, z, sm, W, i, batch, seq)
        out_c = _fox(qt, k, vt, sm, W, i, batch, seq)
        xt = _post(xt, out_a, out_b, out_c, W, i, final=(i == depth - 1))
    return xt.reshape(batch, seq, d)
```

```python
import functools

import jax
import jax.numpy as jnp
from jax import lax
from jax.experimental import pallas as pl
from jax.experimental.pallas import tpu as pltpu

F32 = jnp.float32
BF16 = jnp.bfloat16

HEAD_DIM = 64
ATT_HEADS = 4
ATT_W = ATT_HEADS * HEAD_DIM
PAIR_W = 2 * HEAD_DIM
KV_BLOCK = 256
WIN_BLOCKS = 2
KEY_WIN = WIN_BLOCKS * KV_BLOCK
SOFTMAX_ROWS = 64
MOBA_TOPK = 3
AUG_POS = 32
ACC_ROWS = HEAD_DIM + 16
LOG2E = 1.4426950408889634
UNDERFLOW = 170.0
BOUND_SLACK = 1.01
SSD_HEADS = 8
SSD_P = 64
SSD_INNER = SSD_HEADS * SSD_P
SSD_GROUPS = 2
SSD_N = 128
SSD_CONV = 4
SSD_CHUNK = 256
SSD_XBC = SSD_INNER + 2 * SSD_GROUPS * SSD_N
SMALL_W = 128
IN_COLS = {"qv": (0, 4 * ATT_W), "k": (4 * ATT_W, 2 * ATT_W), "z": (6 * ATT_W, SSD_INNER),
           "xbc": (6 * ATT_W + SSD_INNER, SSD_XBC), "sm": (6 * ATT_W + SSD_INNER + SSD_XBC, SMALL_W)}
PLE_DIM = 256
RMS_EPS = 1e-6
NEG = -1e30
LANES = 128
VMEM_LIMIT = 56 * 1024 * 1024

ROW_TILE = 1024
POST_ROW_TILE = 1024
FF_CHUNK = 1024


def _rms(x, g):
    return x * lax.rsqrt(jnp.mean(x * x, axis=-1, keepdims=True) + RMS_EPS) * g


def _dot(a, b):
    return jnp.dot(a, b, preferred_element_type=F32)


def _dot_nt(a, b):
    return lax.dot_general(a, b, (((1,), (1,)), ((), ())), preferred_element_type=F32)


def _split3(a):
    hi = a.astype(BF16)
    r1 = a - hi.astype(F32)
    mid = r1.astype(BF16)
    lo = (r1 - mid.astype(F32)).astype(BF16)
    return hi, mid, lo


def _softplus(x):
    return jnp.maximum(x, 0.0) + jnp.log(1.0 + jnp.exp(-jnp.abs(x)))


def _silu(x):
    return x * (1.0 / (1.0 + jnp.exp(-x)))


def _inproj_kernel(x_ref, g_ref, wqv_ref, wk_ref, wz_ref, wxbc_ref, wsm_ref,
                   qt_ref, k_ref, vt_ref, z_ref, xbc_ref, sm_ref, wqvt_sc):
    @pl.when(pl.program_id(0) == 0)
    def _():
        wqvt_sc[...] = wqv_ref[...].astype(F32).T.astype(BF16)

    h = _rms(x_ref[...], g_ref[...]).astype(BF16)
    qt_ref[...] = (_dot_nt(wqvt_sc[0:2 * ATT_W, :], h) * (HEAD_DIM ** -0.5 * LOG2E)).astype(BF16)
    k_ref[...] = _dot(h, wk_ref[...]).astype(BF16)
    vt_ref[...] = _dot_nt(wqvt_sc[2 * ATT_W:, :], h).astype(BF16)
    z_ref[...] = _dot(h, wz_ref[...])
    xbc_ref[...] = _dot(h, wxbc_ref[...])
    sm_ref[...] = _dot(h, wsm_ref[...])


def _inproj(x, W, layer):
    t, d = x.shape
    tm = ROW_TILE
    row = lambda i: (i, 0)
    col = lambda i: (0, i)

    def wspec(name):
        start, width = IN_COLS[name]
        assert start % width == 0
        return pl.BlockSpec((None, d, width), lambda i: (layer, 0, start // width))
    outs = (
        jax.ShapeDtypeStruct((2 * ATT_W, t), BF16),
        jax.ShapeDtypeStruct((t, 2 * ATT_W), BF16),
        jax.ShapeDtypeStruct((2 * ATT_W, t), BF16),
        jax.ShapeDtypeStruct((t, SSD_INNER), F32),
        jax.ShapeDtypeStruct((t, SSD_XBC), F32),
        jax.ShapeDtypeStruct((t, SMALL_W), F32),
    )
    names = ("qv", "k", "z", "xbc", "sm")
    return pl.pallas_call(
        _inproj_kernel,
        out_shape=outs,
        grid=(t // tm,),
        in_specs=[pl.BlockSpec((tm, d), row), pl.BlockSpec((None, 1, d), lambda i: (layer, 0, 0))]
                 + [wspec(n) for n in names],
        out_specs=(pl.BlockSpec((2 * ATT_W, tm), col), pl.BlockSpec((tm, 2 * ATT_W), row),
                   pl.BlockSpec((2 * ATT_W, tm), col), pl.BlockSpec((tm, SSD_INNER), row),
                   pl.BlockSpec((tm, SSD_XBC), row), pl.BlockSpec((tm, SMALL_W), row)),
        scratch_shapes=[pltpu.VMEM((4 * ATT_W, d), BF16)],
        compiler_params=pltpu.CompilerParams(dimension_semantics=("arbitrary",), vmem_limit_bytes=VMEM_LIMIT),
        name="inproj",
    )(x, W["g_mix"], *([W["w_in"]] * len(names)))


class _Flash:
    def __init__(self, kaug_sc, qaug_sc, vt_ref, shift_of, s0_sc, s_sc, p_sc, acc_sc, m_sc, alpha_sc,
                 heads=range(ATT_HEADS)):
        self.kaug_sc, self.qaug_sc, self.vt_ref, self.shift_of = kaug_sc, qaug_sc, vt_ref, shift_of
        self.s0_sc, self.s_sc, self.p_sc, self.acc_sc = s0_sc, s_sc, p_sc, acc_sc
        self.m_sc, self.alpha_sc = m_sc, alpha_sc
        self.heads = tuple(heads)
        self.ones = jnp.ones((ACC_ROWS - HEAD_DIM, KEY_WIN), BF16)

    def reset(self):
        self.s_sc[...] = jnp.zeros(self.s_sc.shape, F32)
        self.p_sc[...] = jnp.zeros(self.p_sc.shape, BF16)
        self.acc_sc[...] = jnp.ones(self.acc_sc.shape, F32)
        self.m_sc[...] = jnp.zeros(self.m_sc.shape, F32)
        self.alpha_sc[...] = jnp.ones(self.alpha_sc.shape, F32)

    def scores(self, h, win):
        c0 = pl.multiple_of(win * KEY_WIN, KEY_WIN)
        return _dot(self.kaug_sc[pl.ds(c0, KEY_WIN), h * PAIR_W:(h + 1) * PAIR_W], self.qaug_sc[h])

    @staticmethod
    def plan(tile, first_win, live=True):
        own_win = tile // WIN_BLOCKS
        n_past = jnp.where(live, own_win - first_win, 0)
        dead = jnp.full((1, KV_BLOCK), n_past, jnp.int32) == 0
        return own_win, first_win, jnp.maximum(n_past, 1), jnp.where(dead, 1.0, 0.0)

    @staticmethod
    def tile_win(t, plan):
        own_win, first_win, _, _ = plan
        return jnp.where(t < 1, own_win, first_win + t - 1)

    def pv_stage(self, t, plan, alpha):
        c0 = pl.multiple_of(self.tile_win(t, plan) * KEY_WIN, KEY_WIN)
        for h in self.heads:
            vt_h = jnp.concatenate([self.vt_ref[h * HEAD_DIM:(h + 1) * HEAD_DIM, pl.ds(c0, KEY_WIN)], self.ones],
                                   axis=0)
            self.acc_sc[h] = alpha[h] * self.acc_sc[h] + _dot(vt_h, self.p_sc[h])

    def softmax_stage(self, t, tile, plan, m, src, refill):
        m_out, alpha = {}, {}
        for h in self.heads:
            chunks = [pl.ds(r, SOFTMAX_ROWS) for r in range(0, KEY_WIN, SOFTMAX_ROWS)]
            top = src[h, chunks[0], :]
            for rows in chunks[1:]:
                top = jnp.maximum(top, src[h, rows, :])
            shift = self.shift_of(h, self.tile_win(t, plan), tile)
            if src is self.s_sc:
                shift = shift + plan[3] * NEG
            m_new = jnp.maximum(m[h], jnp.max(top, axis=0, keepdims=True) + shift)
            base = m_new - shift
            for rows in chunks:
                self.p_sc[h, rows, :] = jnp.exp2(src[h, rows, :] - base).astype(BF16)
            if refill:
                self.s_sc[h] = self.scores(h, self.tile_win(t + 1, plan))
            m_out[h] = m_new
            alpha[h] = jnp.exp2(m[h] - m_new)
        return m_out, alpha

    def own_scores(self, tile):
        own_win = tile // WIN_BLOCKS
        key = lax.broadcasted_iota(jnp.int32, (KEY_WIN, KV_BLOCK), 0) + (own_win * KEY_WIN - tile * KV_BLOCK)
        causal = key <= lax.broadcasted_iota(jnp.int32, (KEY_WIN, KV_BLOCK), 1)
        for h in self.heads:
            self.s0_sc[h] = jnp.where(causal, self.scores(h, own_win), NEG)

    def finish(self, tile, first_win):
        plan = self.plan(tile, first_win)
        n = plan[2]
        m = {h: self.m_sc[h] for h in self.heads}
        alpha = {h: self.alpha_sc[h] for h in self.heads}
        self.pv_stage(n - 1, plan, alpha)
        m, alpha = self.softmax_stage(n, tile, plan, m, self.s_sc, refill=False)
        self.pv_stage(n, plan, alpha)

    def save(self, carry):
        m, alpha = carry
        for h in self.heads:
            self.m_sc[h] = m[h]
            self.alpha_sc[h] = alpha[h]

    def first_softmax(self, tile, plan):
        start = {h: jnp.full((1, KV_BLOCK), NEG, F32) for h in self.heads}
        return self.softmax_stage(0, tile, plan, start, self.s0_sc, refill=True)

    def step(self, t, tile, plan, carry):
        m, alpha = carry
        self.pv_stage(t - 1, plan, alpha)
        return self.softmax_stage(t, tile, plan, m, self.s_sc, refill=True)

    def run(self, tile, first_win, live):
        plan = self.plan(tile, first_win, live)
        for h in self.heads:
            self.acc_sc[h] = jnp.zeros(self.acc_sc.shape[1:], F32)
        carry = self.first_softmax(tile, plan)
        self.save(lax.fori_loop(1, plan[2], lambda t, c: self.step(t, tile, plan, c), carry))


def _flash_output(o_ref, acc_sc):
    out_t = jnp.concatenate([acc_sc[h][0:HEAD_DIM, :] / acc_sc[h][HEAD_DIM:HEAD_DIM + 1, :]
                             for h in range(ATT_HEADS)], axis=0)
    o_ref[...] = out_t.T.astype(o_ref.dtype)


def _flash_scratch():
    return [pltpu.VMEM((ATT_HEADS, PAIR_W, KV_BLOCK), BF16),
            pltpu.VMEM((ATT_HEADS, KEY_WIN, KV_BLOCK), F32),
            pltpu.VMEM((ATT_HEADS, KEY_WIN, KV_BLOCK), F32),
            pltpu.VMEM((ATT_HEADS, KEY_WIN, KV_BLOCK), BF16),
            pltpu.VMEM((ATT_HEADS, ACC_ROWS, KV_BLOCK), F32),
            pltpu.VMEM((ATT_HEADS, 1, KV_BLOCK), F32),
            pltpu.VMEM((ATT_HEADS, 1, KV_BLOCK), F32)]


def _f_lane(h):
    return (HEAD_DIM if h % 2 == 0 else 0) + AUG_POS + 3 * (h // 2)


def _own_lanes(h, shape):
    lane = lax.broadcasted_iota(jnp.int32, shape, 1)
    if h % 2 == 0:
        return lane < HEAD_DIM, lane - HEAD_DIM
    return lane >= HEAD_DIM, lane


def _stack_q(h, q_h, aug_rows):
    return jnp.concatenate([q_h, aug_rows] if h % 2 == 0 else [aug_rows, q_h], axis=0)


def _moba_kernel(qt_ref, k_ref, vt_ref, o_ref, kaug_sc, km_sc, knorm_sc, first_sc, qaug_sc, s0_sc, s_sc, p_sc, acc_sc,
                 m_sc, alpha_sc, *, nb):
    step = pl.program_id(1)
    i = jnp.minimum(step, nb - 1)
    slopes = [LOG2E * 2.0 ** (-8.0 * (h + 1) / ATT_HEADS) for h in range(ATT_HEADS)]

    def shift_of(h, win, tile):
        return (slopes[h] * KEY_WIN) * jnp.full((1, KV_BLOCK), win - tile // WIN_BLOCKS, jnp.int32).astype(F32)

    groups = ((2, 3), (1,), (0,))
    near_heads = tuple(h for g in groups[1:] for h in g)
    scratch = (kaug_sc, qaug_sc, vt_ref, shift_of, s0_sc, s_sc, p_sc, acc_sc, m_sc, alpha_sc)
    pipes = [_Flash(*scratch, heads=g) for g in groups]

    @pl.when(step == 0)
    def _():
        pipes[0].reset()
        for g in range(1, len(groups)):
            first_sc[g] = 0
        knorm_sc[...] = jnp.zeros(knorm_sc.shape, F32)
        km_sc[...] = jnp.zeros(km_sc.shape, F32)
        offs = lax.broadcasted_iota(jnp.int32, (KV_BLOCK, PAIR_W), 0).astype(F32)
        km_row = lax.broadcasted_iota(jnp.int32, (HEAD_DIM, PAIR_W), 0)

        def build(n, carry):
            c0 = pl.multiple_of(n * KV_BLOCK, KV_BLOCK)
            in_win = jnp.full((KV_BLOCK, PAIR_W), n % WIN_BLOCKS, jnp.int32).astype(F32)
            for j in range(ATT_HEADS // 2):
                kp = k_ref[pl.ds(c0, KV_BLOCK), j * PAIR_W:(j + 1) * PAIR_W]
                mean = jnp.sum(kp.astype(F32), axis=0, keepdims=True) * (1.0 / KV_BLOCK)
                for h in (2 * j, 2 * j + 1):
                    own, slot = _own_lanes(h, (KV_BLOCK, PAIR_W))
                    aug = jnp.where(slot == n, 1.0,
                                    jnp.where((slot >= AUG_POS) & (slot < AUG_POS + 3), offs,
                                              jnp.where((slot >= AUG_POS + 3) & (slot < AUG_POS + 6), in_win, 0.0)))
                    kaug_sc[pl.ds(c0, KV_BLOCK), h * PAIR_W:(h + 1) * PAIR_W] = jnp.where(own, kp, aug.astype(BF16))
                    own_km, _ = _own_lanes(h, (HEAD_DIM, PAIR_W))
                    km_sc[h] = jnp.where(km_row == n, jnp.where(own_km, mean, 0.0), km_sc[h])
                    if h in near_heads:
                        kf = jnp.where(own, kp.astype(F32), 0.0)
                        sq = jnp.max(jnp.sum(kf * kf, axis=1, keepdims=True), axis=0, keepdims=True)
                        knorm_sc[h:h + 1, :] = jnp.maximum(knorm_sc[h:h + 1, :], sq)
            return carry

        lax.fori_loop(0, nb, build, 0)

    blk_id = lax.broadcasted_iota(jnp.int32, (HEAD_DIM, KV_BLOCK), 0)
    blk_idf = blk_id.astype(F32)
    for h in range(ATT_HEADS):
        j = h // 2
        qp = qt_ref[j * PAIR_W:(j + 1) * PAIR_W, :]
        km_hi, km_mid, km_lo = _split3(km_sc[h])
        score = jnp.where(blk_id < i, _dot(km_hi, qp) + _dot(km_mid, qp) + _dot(km_lo, qp), NEG)
        sel = blk_id == i
        for _ in range(MOBA_TOPK):
            top = jnp.max(score, axis=0, keepdims=True)
            first = jnp.min(jnp.where(score == top, blk_idf, 1e9), axis=0, keepdims=True)
            hit = blk_idf == first
            sel = sel | (hit & (top > 0.5 * NEG))
            score = jnp.where(hit, 3.0 * NEG, score)
        terms = [t.astype(F32) for t in _split3(jnp.full((1, 1), slopes[h], F32))]
        terms += [t * KV_BLOCK for t in terms]
        aug = jnp.where(blk_id < AUG_POS, jnp.where(sel, 0.0, NEG), 0.0)
        for k, term in enumerate(terms):
            aug = jnp.where(blk_id == AUG_POS + k, term, aug)
        qaug_sc[h] = _stack_q(h, qt_ref[h * HEAD_DIM:(h + 1) * HEAD_DIM, :], aug.astype(BF16))

    own_win = i // WIN_BLOCKS
    firsts = [0]
    for g in groups[1:]:
        count = jnp.zeros((1, KV_BLOCK), F32)
        reach = {}
        for h in g:
            q2 = jnp.max(jnp.sum(jnp.square(qt_ref[h * HEAD_DIM:(h + 1) * HEAD_DIM, :].astype(F32)), axis=0,
                                 keepdims=True), axis=1, keepdims=True)
            reach[h] = (2.0 * BOUND_SLACK) * jnp.sqrt(q2 * knorm_sc[h:h + 1, :]) + UNDERFLOW
        for w in range(nb // WIN_BLOCKS):
            gap = jnp.full((1, KV_BLOCK), own_win - 1 - w, jnp.int32).astype(F32) * KEY_WIN
            matters = functools.reduce(jnp.maximum, [jnp.where(gap * slopes[h] < reach[h], 1.0, 0.0) for h in g])
            count = count + jnp.where(w < own_win, matters, 0.0)
        firsts.append(jnp.maximum(own_win - count[0, 0].astype(jnp.int32), firsts[-1]))

    prev = jnp.maximum(step - 1, 0)
    for g, pipe in enumerate(pipes):
        pipe.finish(prev, first_sc[g] if g else 0)
    _flash_output(o_ref, acc_sc)

    live = step < nb
    plans = [pipe.plan(i, first, live) for pipe, first in zip(pipes, firsts)]
    for pipe in pipes:
        pipe.own_scores(i)
    for h in range(ATT_HEADS):
        acc_sc[h] = jnp.zeros(acc_sc.shape[1:], F32)
    carries = [pipe.first_softmax(i, plan) for pipe, plan in zip(pipes, plans)]
    lag = [plans[0][2] - plan[2] for plan in plans]
    for k in range(len(groups)):
        lo = lag[k] + 1
        hi = lag[k + 1] + 1 if k + 1 < len(groups) else plans[0][2]

        def phase(t, active, k=k):
            return [pipes[g].step(t - lag[g], i, plans[g], active[g]) for g in range(k + 1)]

        carries[:k + 1] = lax.fori_loop(lo, hi, phase, carries[:k + 1])
    for g, pipe in enumerate(pipes):
        pipe.save(carries[g])
        if g:
            first_sc[g] = firsts[g]


def _moba(qt, k, vt, batch, seq):
    nb = seq // KV_BLOCK
    return pl.pallas_call(
        functools.partial(_moba_kernel, nb=nb),
        out_shape=jax.ShapeDtypeStruct((batch * seq, ATT_W), BF16),
        grid=(batch, nb + 1),
        in_specs=[pl.BlockSpec((ATT_W, KV_BLOCK), lambda b, i: (0, b * nb + jnp.minimum(i, nb - 1))),
                  pl.BlockSpec((seq, ATT_W), lambda b, i: (b, 0)),
                  pl.BlockSpec((ATT_W, seq), lambda b, i: (0, b))],
        out_specs=pl.BlockSpec((KV_BLOCK, ATT_W), lambda b, i: (b * nb + jnp.maximum(i - 1, 0), 0)),
        scratch_shapes=[pltpu.VMEM((seq, ATT_HEADS * PAIR_W), BF16),
                        pltpu.VMEM((ATT_HEADS, HEAD_DIM, PAIR_W), F32),
                        pltpu.VMEM((ATT_HEADS, KV_BLOCK), F32),
                        pltpu.SMEM((ATT_HEADS,), jnp.int32)]
                       + _flash_scratch(),
        compiler_params=pltpu.CompilerParams(dimension_semantics=("arbitrary", "arbitrary"),
                                             vmem_limit_bytes=VMEM_LIMIT),
        name="moba",
    )(qt, k, vt)


def _fox_kernel(qt_ref, k_ref, vt_ref, sm_ref, fb_ref, o_ref, kaug_sc, cstart_sc, cend_sc, knorm_sc, first_sc,
                qaug_sc, s0_sc, s_sc, p_sc, acc_sc, m_sc, alpha_sc, *, nb):
    step = pl.program_id(1)
    i = jnp.minimum(step, nb - 1)

    def shift_of(h, win, tile):
        return -cstart_sc[win][h:h + 1, :]

    flash = _Flash(kaug_sc, qaug_sc, vt_ref, shift_of, s0_sc, s_sc, p_sc, acc_sc, m_sc, alpha_sc)

    @pl.when(step == 0)
    def _():
        flash.reset()
        first_sc[0] = 0
        r = lax.broadcasted_iota(jnp.int32, (KV_BLOCK, KV_BLOCK), 0)
        c = lax.broadcasted_iota(jnp.int32, (KV_BLOCK, KV_BLOCK), 1)
        tril = (c <= r).astype(BF16)
        lane = lax.broadcasted_iota(jnp.int32, (KV_BLOCK, PAIR_W), 1)
        term = (lane % HEAD_DIM - AUG_POS) % 3

        def head_rows(v):
            return jnp.concatenate(
                [jnp.broadcast_to(v[0:1, _f_lane(h):_f_lane(h) + 1], (1, KV_BLOCK)) for h in range(ATT_HEADS)], axis=0)

        def scan(n, carry):
            c0 = pl.multiple_of(n * KV_BLOCK, KV_BLOCK)
            first = jnp.full((8, SMALL_W), n % WIN_BLOCKS, jnp.int32) == 0
            blk_start, win_start = carry[0], jnp.where(first, carry[0], carry[1])
            logf = -LOG2E * _softplus(-(sm_ref[pl.ds(c0, KV_BLOCK), :] + fb_ref[...]))
            hi, mid, lo = _split3(logf)
            cs = _dot(tril, hi) + _dot(tril, mid) + _dot(tril, lo)
            in_win = cs + (blk_start - win_start)[0:1, :]
            t_hi, t_mid, t_lo = (t.astype(F32) for t in _split3(in_win))
            terms = jnp.where(term == 0, t_hi, jnp.where(term == 1, t_mid, t_lo))
            blk_end = blk_start + jnp.broadcast_to(cs[KV_BLOCK - 1:KV_BLOCK, :], (8, SMALL_W))
            cstart_sc[n // WIN_BLOCKS] = head_rows(win_start)
            cend_sc[n // WIN_BLOCKS] = head_rows(blk_end)
            knorm = list(carry[2:])
            for h in range(ATT_HEADS):
                j = h // 2
                own, _ = _own_lanes(h, (KV_BLOCK, PAIR_W))
                mine = (lane >= _f_lane(h)) & (lane < _f_lane(h) + 3)
                kp = k_ref[pl.ds(c0, KV_BLOCK), j * PAIR_W:(j + 1) * PAIR_W]
                kaug_sc[pl.ds(c0, KV_BLOCK), h * PAIR_W:(h + 1) * PAIR_W] = jnp.where(
                    own, kp, jnp.where(mine, terms, 0.0).astype(BF16))
                kf = jnp.where(own, kp.astype(F32), 0.0)
                sq = jnp.max(jnp.sum(kf * kf, axis=1, keepdims=True), axis=0, keepdims=True)
                knorm[h] = jnp.maximum(knorm[h], jnp.broadcast_to(sq, (8, SMALL_W)))
            return (blk_end, win_start, *knorm)

        zero = jnp.broadcast_to(fb_ref[...] * 0.0, (8, SMALL_W))
        done = lax.fori_loop(0, nb, scan, (zero,) * (2 + ATT_HEADS))
        knorm_sc[...] = jnp.concatenate(
            [jnp.concatenate([v[0:1, :]] * (KV_BLOCK // SMALL_W), axis=1) for v in done[2:]], axis=0)

    aug_id = lax.broadcasted_iota(jnp.int32, (HEAD_DIM, KV_BLOCK), 0)
    for h in range(ATT_HEADS):
        slot = _f_lane(h) % HEAD_DIM
        aug = jnp.where((aug_id >= slot) & (aug_id < slot + 3), -1.0, 0.0).astype(BF16)
        qaug_sc[h] = _stack_q(h, qt_ref[h * HEAD_DIM:(h + 1) * HEAD_DIM, :], aug)

    own_win = i // WIN_BLOCKS
    q2 = [jnp.max(jnp.sum(jnp.square(qt_ref[h * HEAD_DIM:(h + 1) * HEAD_DIM, :].astype(F32)), axis=0, keepdims=True),
                  axis=1, keepdims=True) for h in range(ATT_HEADS)]
    bound = jnp.sqrt(jnp.concatenate(q2, axis=0) * knorm_sc[...])
    reach = (2.0 * BOUND_SLACK) * bound + UNDERFLOW
    c_own = cstart_sc[own_win]
    count = jnp.zeros((1, KV_BLOCK), F32)
    for w in range(nb // WIN_BLOCKS):
        matters = jnp.max(jnp.where(cend_sc[w] - c_own < reach, 1.0, 0.0), axis=0, keepdims=True)
        count = count + jnp.where(w < own_win, matters, 0.0)
    first_win = own_win - count[0, 0].astype(jnp.int32)

    flash.finish(jnp.maximum(step - 1, 0), first_sc[0])
    _flash_output(o_ref, acc_sc)
    flash.own_scores(i)
    flash.run(i, first_win, step < nb)
    first_sc[0] = first_win


def _fox(qt, k, vt, sm, W, layer, batch, seq):
    nb = seq // KV_BLOCK
    return pl.pallas_call(
        functools.partial(_fox_kernel, nb=nb),
        out_shape=jax.ShapeDtypeStruct((batch * seq, ATT_W), BF16),
        grid=(batch, nb + 1),
        in_specs=[pl.BlockSpec((ATT_W, KV_BLOCK), lambda b, i: (1, b * nb + jnp.minimum(i, nb - 1))),
                  pl.BlockSpec((seq, ATT_W), lambda b, i: (b, 1)),
                  pl.BlockSpec((ATT_W, seq), lambda b, i: (1, b)),
                  pl.BlockSpec((seq, SMALL_W), lambda b, i: (b, 0)),
                  pl.BlockSpec((None, 1, SMALL_W), lambda b, i: (layer, 0, 0))],
        out_specs=pl.BlockSpec((KV_BLOCK, ATT_W), lambda b, i: (b * nb + jnp.maximum(i - 1, 0), 0)),
        scratch_shapes=[pltpu.VMEM((seq, ATT_HEADS * PAIR_W), BF16),
                        pltpu.VMEM((nb // WIN_BLOCKS, ATT_HEADS, KV_BLOCK), F32),
                        pltpu.VMEM((nb // WIN_BLOCKS, ATT_HEADS, KV_BLOCK), F32),
                        pltpu.VMEM((ATT_HEADS, KV_BLOCK), F32),
                        pltpu.SMEM((1,), jnp.int32)]
                       + _flash_scratch(),
        compiler_params=pltpu.CompilerParams(dimension_semantics=("arbitrary", "arbitrary"),
                                             vmem_limit_bytes=VMEM_LIMIT),
        name="fox",
    )(qt, k, vt, sm, W["fox_fb"])


def _ssd_kernel(xbc_ref, z_ref, sm_ref, cw_ref, cb_ref, dtb_ref, alog_ref, dskip_ref, nw_ref, o_ref,
                ext_sc, state_sc, y_sc):
    c = pl.program_id(1)
    L = SSD_CHUNK
    tail = 8

    @pl.when(c == 0)
    def _():
        ext_sc[0:tail, :] = jnp.zeros((tail, SSD_XBC), F32)
        state_sc[...] = jnp.zeros(state_sc.shape, F32)

    ext_sc[tail:tail + L, :] = xbc_ref[...]
    u = cb_ref[...]
    for k in range(SSD_CONV):
        back = SSD_CONV - 1 - k
        u = u + cw_ref[k:k + 1, :] * ext_sc[tail - back:tail - back + L, :]
    ext_sc[0:tail, :] = ext_sc[L:L + tail, :]
    xc = _silu(u)

    dt = _softplus(sm_ref[...] + dtb_ref[...])
    a = dt * (-jnp.exp(alog_ref[...]))
    r = lax.broadcasted_iota(jnp.int32, (L, L), 0)
    cidx = lax.broadcasted_iota(jnp.int32, (L, L), 1)
    causal = cidx <= r
    tril = causal.astype(BF16)
    a_hi, a_mid, a_lo = _split3(a)
    acs = (_dot(tril, a_hi) + _dot(tril, a_mid) + _dot(tril, a_lo)) * LOG2E
    acs_t = acs.T

    widen_m = (lax.broadcasted_iota(jnp.int32, (SMALL_W, SSD_INNER), 0)
               == lax.broadcasted_iota(jnp.int32, (SMALL_W, SSD_INNER), 1) // SSD_P).astype(BF16)

    def widen(v):
        hi, mid, lo = _split3(v)
        return _dot(hi, widen_m) + _dot(mid, widen_m) + _dot(lo, widen_m)

    total = acs[L - 1:L, :]
    xs = xc[:, 0:SSD_INNER]
    xh = xs * widen(dt)
    x_end = (xh * widen(jnp.exp2(total - acs))).astype(BF16)
    from_start = widen(jnp.exp2(acs))
    keep = widen(jnp.broadcast_to(jnp.exp2(total), (8, SMALL_W)))[0:1, :]
    lane = lax.broadcasted_iota(jnp.int32, (L, PAIR_W), 1)
    heads_per_group = SSD_HEADS // SSD_GROUPS
    gw = heads_per_group * SSD_P

    for g in range(SSD_GROUPS):
        bg = xc[:, SSD_INNER + g * SSD_N:SSD_INNER + (g + 1) * SSD_N].astype(BF16)
        cg = xc[:, SSD_INNER + (SSD_GROUPS + g) * SSD_N:SSD_INNER + (SSD_GROUPS + g + 1) * SSD_N].astype(BF16)
        bgt = xc[:, SSD_INNER + g * SSD_N:SSD_INNER + (g + 1) * SSD_N].T.astype(BF16)
        cb = jnp.where(causal, _dot_nt(cg, bg), 0.0)
        gcols = slice(g * gw, (g + 1) * gw)
        st = state_sc[:, gcols]
        y_off = from_start[:, gcols] * _dot(cg, st.astype(BF16))
        state_sc[:, gcols] = keep[:, gcols] * st + _dot(bgt, x_end[:, gcols])
        for pair in range(heads_per_group // 2):
            h0 = g * heads_per_group + 2 * pair
            pcols = slice(h0 * SSD_P, (h0 + 2) * SSD_P)
            xp = xh[:, pcols].astype(BF16)
            y_pair = []
            for h in (h0, h0 + 1):
                decay = jnp.exp2(jnp.minimum(acs[:, h:h + 1] - acs_t[h:h + 1, :], 0.0))
                y_pair.append(_dot((cb * decay).astype(BF16), xp))
            y_diag = jnp.where(lane < SSD_P, y_pair[0], y_pair[1])
            y_sc[:, pcols] = (y_diag + y_off[:, pair * PAIR_W:(pair + 1) * PAIR_W]
                              + dskip_ref[:, pcols] * xs[:, pcols])

    y = y_sc[...] * _silu(z_ref[...])
    o_ref[...] = _rms(y, nw_ref[...]).astype(o_ref.dtype)


def _ssd(xbc, z, sm, W, layer, batch, seq):
    nc = seq // SSD_CHUNK
    L = SSD_CHUNK
    row = lambda b, c: (b * nc + c, 0)
    wspec = lambda a: pl.BlockSpec((None,) + a.shape[1:], lambda b, c: (layer, 0, 0))
    ws = tuple(W[n] for n in ("conv_w", "conv_b", "dt_bias", "a_log", "d_skip", "ssd_norm_w"))
    return pl.pallas_call(
        _ssd_kernel,
        out_shape=jax.ShapeDtypeStruct((batch * seq, SSD_INNER), BF16),
        grid=(batch, nc),
        in_specs=[pl.BlockSpec((L, SSD_XBC), row), pl.BlockSpec((L, SSD_INNER), row),
                  pl.BlockSpec((L, SMALL_W), row)] + [wspec(a) for a in ws],
        out_specs=pl.BlockSpec((L, SSD_INNER), row),
        scratch_shapes=[pltpu.VMEM((L + 8, SSD_XBC), F32),
                        pltpu.VMEM((SSD_N, SSD_INNER), F32),
                        pltpu.VMEM((L, SSD_INNER), F32)],
        compiler_params=pltpu.CompilerParams(dimension_semantics=("arbitrary", "arbitrary"),
                                             vmem_limit_bytes=VMEM_LIMIT),
        name="ssd",
    )(xbc, z, sm, *ws)


def _post_kernel(x_ref, a_ref, b_ref, c_ref, p_ref, wo_ref, gm_ref, w1_ref, w2_ref, gp_ref, wg_ref, wp_ref,
                 gf_ref, o_ref, acc_sc, h_sc, *, final):
    j = pl.program_id(1)

    @pl.when(j == 0)
    def _():
        wa = wo_ref[0:ATT_W, :]
        wb = wo_ref[ATT_W:ATT_W + SSD_INNER, :]
        wc = wo_ref[ATT_W + SSD_INNER:, :]
        x1 = x_ref[...] + _dot(a_ref[...], wa) + _dot(b_ref[...], wb) + _dot(c_ref[...], wc)
        acc_sc[...] = x1
        h_sc[...] = _rms(x1, gm_ref[...]).astype(BF16)

    u = jnp.maximum(_dot(h_sc[...], w1_ref[...]), 0.0)
    acc_sc[...] += _dot((u * u).astype(BF16), w2_ref[...])

    @pl.when(j == pl.num_programs(1) - 1)
    def _():
        x2 = acc_sc[...]
        gate = 1.0 / (1.0 + jnp.exp(-_dot(_rms(x2, gp_ref[...]).astype(BF16), wg_ref[...])))
        x3 = x2 + gate * _dot(p_ref[...].astype(BF16), wp_ref[...])
        o_ref[...] = _rms(x3, gf_ref[...]) if final else x3


def _post(x, a, b, c, W, layer, final):
    t, d = x.shape
    tm = POST_ROW_TILE
    d_ff = W["w_ff1"].shape[2]
    row = lambda i, j: (i, 0)
    wspec = lambda n: pl.BlockSpec((None,) + W[n].shape[1:], lambda i, j: (layer, 0, 0))
    return pl.pallas_call(
        functools.partial(_post_kernel, final=final),
        out_shape=jax.ShapeDtypeStruct((t, d), F32),
        grid=(t // tm, d_ff // FF_CHUNK),
        in_specs=[pl.BlockSpec((tm, d), row), pl.BlockSpec((tm, ATT_W), row), pl.BlockSpec((tm, SSD_INNER), row),
                  pl.BlockSpec((tm, ATT_W), row), pl.BlockSpec((None, tm, PLE_DIM), lambda i, j: (layer, i, 0)),
                  wspec("w_out"), wspec("g_mlp"),
                  pl.BlockSpec((None, d, FF_CHUNK), lambda i, j: (layer, 0, j)),
                  pl.BlockSpec((None, FF_CHUNK, d), lambda i, j: (layer, j, 0)),
                  wspec("g_ple"), wspec("w_ple_gate"), wspec("w_ple_proj"),
                  pl.BlockSpec(W["g_final"].shape, lambda i, j: (0, 0))],
        out_specs=pl.BlockSpec((tm, d), row),
        scratch_shapes=[pltpu.VMEM((tm, d), F32), pltpu.VMEM((tm, d), BF16)],
        compiler_params=pltpu.CompilerParams(dimension_semantics=("arbitrary", "arbitrary"),
                                             vmem_limit_bytes=VMEM_LIMIT),
        name="post",
    )(x, a, b, c, W["p"], W["w_out"], W["g_mlp"], W["w_ff1"], W["w_ff2"], W["g_ple"], W["w_ple_gate"],
      W["w_ple_proj"], W["g_final"])


def _prepare(p, g_mix, w_in, conv_w, conv_b, dt_bias, a_log, d_skip, ssd_norm_w, fox_f_bias, w_out,
             g_mlp, w_ff1, w_ff2, g_ple, w_ple_gate, w_ple_proj, g_final):
    depth, d, _ = w_in.shape
    o = 0
    mq, mk, mv = (w_in[:, :, o + k * ATT_W:o + (k + 1) * ATT_W] for k in range(3)); o += 3 * ATT_W
    wz = w_in[:, :, o:o + SSD_INNER]; o += SSD_INNER
    wxbc = w_in[:, :, o:o + SSD_XBC]; o += SSD_XBC
    wdt = w_in[:, :, o:o + SSD_HEADS]; o += SSD_HEADS
    fq, fk, fv = (w_in[:, :, o + k * ATT_W:o + (k + 1) * ATT_W] for k in range(3)); o += 3 * ATT_W
    wf = w_in[:, :, o:o + ATT_HEADS]
    src = [SSD_HEADS + ATT_HEADS] * SMALL_W
    for j in range(SSD_HEADS):
        src[j] = j
    for h in range(ATT_HEADS):
        for k in range(3):
            src[_f_lane(h) + k] = SSD_HEADS + h
    src = jnp.asarray(src, jnp.int32)
    small = jnp.take(jnp.concatenate([wdt, wf, jnp.zeros_like(wf[:, :, :1])], axis=2), src, axis=2)
    fb = jnp.take(jnp.concatenate([jnp.zeros((depth, SSD_HEADS), F32), fox_f_bias, jnp.zeros((depth, 1), F32)],
                                  axis=1), src, axis=1)
    w_all = jnp.concatenate([mq, fq, mv, fv, mk, fk, wz, wxbc, small], axis=2).astype(BF16)
    assert w_all.shape[2] == IN_COLS["sm"][0] + SMALL_W
    row = lambda a: a[:, None, :]
    pad = lambda a: jnp.pad(a, ((0, 0), (0, SMALL_W - a.shape[1])))
    return {
        "w_in": w_all, "g_mix": row(g_mix),
        "conv_w": conv_w, "conv_b": row(conv_b), "dt_bias": row(pad(dt_bias)), "a_log": row(pad(a_log)),
        "d_skip": row(jnp.repeat(d_skip, SSD_P, axis=1)), "ssd_norm_w": row(ssd_norm_w),
        "fox_fb": row(fb),
        "p": p.reshape(depth, -1, p.shape[-1]),
        "w_out": w_out.astype(BF16), "g_mlp": row(g_mlp), "w_ff1": w_ff1.astype(BF16), "w_ff2": w_ff2.astype(BF16),
        "g_ple": row(g_ple), "w_ple_gate": w_ple_gate.astype(BF16), "w_ple_proj": w_ple_proj.astype(BF16),
        "g_final": g_final[None, :],
    }


def kernel(x, p, g_mix, w_in, conv_w, conv_b, dt_bias, a_log, d_skip, ssd_norm_w, fox_f_bias, w_out, g_mlp, w_ff1, w_ff2, g_ple, w_ple_gate, w_ple_proj, g_final):
    batch, seq, d = x.shape
    depth = w_in.shape[0]
    assert seq % KEY_WIN == 0 and seq % SSD_CHUNK == 0
    assert (batch * seq) % ROW_TILE == 0 and (batch * seq) % POST_ROW_TILE == 0 and w_ff1.shape[2] % FF_CHUNK == 0
    assert seq // KV_BLOCK <= AUG_POS, "one-hot block ids use the augmentation slots below AUG_POS"
    xt = x.reshape(batch * seq, d)
    W = _prepare(p, g_mix, w_in, conv_w, conv_b, dt_bias, a_log, d_skip, ssd_norm_w, fox_f_bias, w_out,
                 g_mlp, w_ff1, w_ff2, g_ple, w_ple_gate, w_ple_proj, g_final)
    for i in range(depth):
        qt, k, vt, z, xbc, sm = _inproj(xt, W, i)
        out_a = _moba(qt, k, vt, batch, seq)
        out_b = _ssd(xbc, z, sm, W, i, batch, seq)
        out_c = _fox(qt, k, vt, sm, W, i, batch, seq)
        xt = _post(xt, out_a, out_b, out_c, W, i, final=(i == depth - 1))
    return xt.reshape(batch, seq, d)
```

```python
import functools

import jax
import jax.numpy as jnp
from jax import lax
from jax.experimental import pallas as pl
from jax.experimental.pallas import tpu as pltpu

F32 = jnp.float32
BF16 = jnp.bfloat16

HEAD_DIM = 64
ATT_HEADS = 4
ATT_W = ATT_HEADS * HEAD_DIM
PAIR_W = 2 * HEAD_DIM
KV_BLOCK = 256
WIN_BLOCKS = 2
KEY_WIN = WIN_BLOCKS * KV_BLOCK
SOFTMAX_ROWS = 64
MOBA_TOPK = 3
AUG_POS = 32
ACC_ROWS = HEAD_DIM + 16
LOG2E = 1.4426950408889634
UNDERFLOW = 152.0
BOUND_SLACK = 1.01
SSD_HEADS = 8
SSD_P = 64
SSD_INNER = SSD_HEADS * SSD_P
SSD_GROUPS = 2
SSD_N = 128
SSD_CONV = 4
SSD_CHUNK = 256
SSD_XBC = SSD_INNER + 2 * SSD_GROUPS * SSD_N
SMALL_W = 128
IN_COLS = {"qv": (0, 4 * ATT_W), "k": (4 * ATT_W, 2 * ATT_W), "z": (6 * ATT_W, SSD_INNER),
           "xbc": (6 * ATT_W + SSD_INNER, SSD_XBC), "sm": (6 * ATT_W + SSD_INNER + SSD_XBC, SMALL_W)}
PLE_DIM = 256
RMS_EPS = 1e-6
NEG = -1e30
LANES = 128
VMEM_LIMIT = 56 * 1024 * 1024

ROW_TILE = 1024
POST_ROW_TILE = 1024
FF_CHUNK = 1024


def _rms(x, g):
    return x * lax.rsqrt(jnp.mean(x * x, axis=-1, keepdims=True) + RMS_EPS) * g


def _dot(a, b):
    return jnp.dot(a, b, preferred_element_type=F32)


def _dot_nt(a, b):
    return lax.dot_general(a, b, (((1,), (1,)), ((), ())), preferred_element_type=F32)


def _split3(a):
    hi = a.astype(BF16)
    r1 = a - hi.astype(F32)
    mid = r1.astype(BF16)
    lo = (r1 - mid.astype(F32)).astype(BF16)
    return hi, mid, lo


def _softplus(x):
    return jnp.maximum(x, 0.0) + jnp.log(1.0 + jnp.exp(-jnp.abs(x)))


def _silu(x):
    return x * (1.0 / (1.0 + jnp.exp(-x)))


def _inproj_kernel(x_ref, g_ref, wqv_ref, wk_ref, wz_ref, wxbc_ref, wsm_ref,
                   qt_ref, k_ref, vt_ref, z_ref, xbc_ref, sm_ref, wqvt_sc):
    @pl.when(pl.program_id(0) == 0)
    def _():
        wqvt_sc[...] = wqv_ref[...].astype(F32).T.astype(BF16)

    h = _rms(x_ref[...], g_ref[...]).astype(BF16)
    qt_ref[...] = (_dot_nt(wqvt_sc[0:2 * ATT_W, :], h) * (HEAD_DIM ** -0.5 * LOG2E)).astype(BF16)
    k_ref[...] = _dot(h, wk_ref[...]).astype(BF16)
    vt_ref[...] = _dot_nt(wqvt_sc[2 * ATT_W:, :], h).astype(BF16)
    z_ref[...] = _dot(h, wz_ref[...])
    xbc_ref[...] = _dot(h, wxbc_ref[...])
    sm_ref[...] = _dot(h, wsm_ref[...])


def _inproj(x, W, layer):
    t, d = x.shape
    tm = ROW_TILE
    row = lambda i: (i, 0)
    col = lambda i: (0, i)

    def wspec(name):
        start, width = IN_COLS[name]
        assert start % width == 0
        return pl.BlockSpec((None, d, width), lambda i: (layer, 0, start // width))
    outs = (
        jax.ShapeDtypeStruct((2 * ATT_W, t), BF16),
        jax.ShapeDtypeStruct((t, 2 * ATT_W), BF16),
        jax.ShapeDtypeStruct((2 * ATT_W, t), BF16),
        jax.ShapeDtypeStruct((t, SSD_INNER), F32),
        jax.ShapeDtypeStruct((t, SSD_XBC), F32),
        jax.ShapeDtypeStruct((t, SMALL_W), F32),
    )
    names = ("qv", "k", "z", "xbc", "sm")
    return pl.pallas_call(
        _inproj_kernel,
        out_shape=outs,
        grid=(t // tm,),
        in_specs=[pl.BlockSpec((tm, d), row), pl.BlockSpec((None, 1, d), lambda i: (layer, 0, 0))]
                 + [wspec(n) for n in names],
        out_specs=(pl.BlockSpec((2 * ATT_W, tm), col), pl.BlockSpec((tm, 2 * ATT_W), row),
                   pl.BlockSpec((2 * ATT_W, tm), col), pl.BlockSpec((tm, SSD_INNER), row),
                   pl.BlockSpec((tm, SSD_XBC), row), pl.BlockSpec((tm, SMALL_W), row)),
        scratch_shapes=[pltpu.VMEM((4 * ATT_W, d), BF16)],
        compiler_params=pltpu.CompilerParams(dimension_semantics=("arbitrary",), vmem_limit_bytes=VMEM_LIMIT),
        name="inproj",
    )(x, W["g_mix"], *([W["w_in"]] * len(names)))


class _Flash:
    def __init__(self, kaug_sc, qaug_sc, vt_ref, shift_of, s0_sc, s_sc, p_sc, acc_sc, m_sc, alpha_sc,
                 heads=range(ATT_HEADS)):
        self.kaug_sc, self.qaug_sc, self.vt_ref, self.shift_of = kaug_sc, qaug_sc, vt_ref, shift_of
        self.s0_sc, self.s_sc, self.p_sc, self.acc_sc = s0_sc, s_sc, p_sc, acc_sc
        self.m_sc, self.alpha_sc = m_sc, alpha_sc
        self.heads = tuple(heads)
        self.ones = jnp.ones((ACC_ROWS - HEAD_DIM, KEY_WIN), BF16)

    def reset(self):
        self.s_sc[...] = jnp.zeros(self.s_sc.shape, F32)
        self.p_sc[...] = jnp.zeros(self.p_sc.shape, BF16)
        self.acc_sc[...] = jnp.ones(self.acc_sc.shape, F32)
        self.m_sc[...] = jnp.zeros(self.m_sc.shape, F32)
        self.alpha_sc[...] = jnp.ones(self.alpha_sc.shape, F32)

    def scores(self, h, win):
        c0 = pl.multiple_of(win * KEY_WIN, KEY_WIN)
        return _dot(self.kaug_sc[pl.ds(c0, KEY_WIN), h * PAIR_W:(h + 1) * PAIR_W], self.qaug_sc[h])

    @staticmethod
    def plan(tile, first_win, live=True):
        own_win = tile // WIN_BLOCKS
        n_past = jnp.where(live, own_win - first_win, 0)
        dead = jnp.full((1, KV_BLOCK), n_past, jnp.int32) == 0
        return own_win, first_win, jnp.maximum(n_past, 1), jnp.where(dead, 1.0, 0.0)

    @staticmethod
    def tile_win(t, plan):
        own_win, first_win, _, _ = plan
        return jnp.where(t < 1, own_win, first_win + t - 1)

    def pv_stage(self, t, plan, alpha):
        c0 = pl.multiple_of(self.tile_win(t, plan) * KEY_WIN, KEY_WIN)
        for h in self.heads:
            vt_h = jnp.concatenate([self.vt_ref[h * HEAD_DIM:(h + 1) * HEAD_DIM, pl.ds(c0, KEY_WIN)], self.ones],
                                   axis=0)
            self.acc_sc[h] = alpha[h] * self.acc_sc[h] + _dot(vt_h, self.p_sc[h])

    def softmax_stage(self, t, tile, plan, m, src, refill):
        m_out, alpha = {}, {}
        for h in self.heads:
            chunks = [pl.ds(r, SOFTMAX_ROWS) for r in range(0, KEY_WIN, SOFTMAX_ROWS)]
            top = src[h, chunks[0], :]
            for rows in chunks[1:]:
                top = jnp.maximum(top, src[h, rows, :])
            shift = self.shift_of(h, self.tile_win(t, plan), tile)
            if src is self.s_sc:
                shift = shift + plan[3] * NEG
            m_new = jnp.maximum(m[h], jnp.max(top, axis=0, keepdims=True) + shift)
            base = m_new - shift
            for rows in chunks:
                self.p_sc[h, rows, :] = jnp.exp2(src[h, rows, :] - base).astype(BF16)
            if refill:
                self.s_sc[h] = self.scores(h, self.tile_win(t + 1, plan))
            m_out[h] = m_new
            alpha[h] = jnp.exp2(m[h] - m_new)
        return m_out, alpha

    def own_scores(self, tile):
        own_win = tile // WIN_BLOCKS
        key = lax.broadcasted_iota(jnp.int32, (KEY_WIN, KV_BLOCK), 0) + (own_win * KEY_WIN - tile * KV_BLOCK)
        causal = key <= lax.broadcasted_iota(jnp.int32, (KEY_WIN, KV_BLOCK), 1)
        for h in self.heads:
            self.s0_sc[h] = jnp.where(causal, self.scores(h, own_win), NEG)

    def finish(self, tile, first_win):
        plan = self.plan(tile, first_win)
        n = plan[2]
        m = {h: self.m_sc[h] for h in self.heads}
        alpha = {h: self.alpha_sc[h] for h in self.heads}
        self.pv_stage(n - 1, plan, alpha)
        m, alpha = self.softmax_stage(n, tile, plan, m, self.s_sc, refill=False)
        self.pv_stage(n, plan, alpha)

    def save(self, carry):
        m, alpha = carry
        for h in self.heads:
            self.m_sc[h] = m[h]
            self.alpha_sc[h] = alpha[h]

    def first_softmax(self, tile, plan):
        start = {h: jnp.full((1, KV_BLOCK), NEG, F32) for h in self.heads}
        return self.softmax_stage(0, tile, plan, start, self.s0_sc, refill=True)

    def step(self, t, tile, plan, carry):
        m, alpha = carry
        self.pv_stage(t - 1, plan, alpha)
        return self.softmax_stage(t, tile, plan, m, self.s_sc, refill=True)

    def run(self, tile, first_win, live):
        plan = self.plan(tile, first_win, live)
        for h in self.heads:
            self.acc_sc[h] = jnp.zeros(self.acc_sc.shape[1:], F32)
        carry = self.first_softmax(tile, plan)
        self.save(lax.fori_loop(1, plan[2], lambda t, c: self.step(t, tile, plan, c), carry))


def _flash_output(o_ref, acc_sc):
    out_t = jnp.concatenate([acc_sc[h][0:HEAD_DIM, :] / acc_sc[h][HEAD_DIM:HEAD_DIM + 1, :]
                             for h in range(ATT_HEADS)], axis=0)
    o_ref[...] = out_t.T.astype(o_ref.dtype)


def _flash_scratch():
    return [pltpu.VMEM((ATT_HEADS, PAIR_W, KV_BLOCK), BF16),
            pltpu.VMEM((ATT_HEADS, KEY_WIN, KV_BLOCK), F32),
            pltpu.VMEM((ATT_HEADS, KEY_WIN, KV_BLOCK), F32),
            pltpu.VMEM((ATT_HEADS, KEY_WIN, KV_BLOCK), BF16),
            pltpu.VMEM((ATT_HEADS, ACC_ROWS, KV_BLOCK), F32),
            pltpu.VMEM((ATT_HEADS, 1, KV_BLOCK), F32),
            pltpu.VMEM((ATT_HEADS, 1, KV_BLOCK), F32)]


def _f_lane(h):
    return (HEAD_DIM if h % 2 == 0 else 0) + AUG_POS + 3 * (h // 2)


def _own_lanes(h, shape):
    lane = lax.broadcasted_iota(jnp.int32, shape, 1)
    if h % 2 == 0:
        return lane < HEAD_DIM, lane - HEAD_DIM
    return lane >= HEAD_DIM, lane


def _stack_q(h, q_h, aug_rows):
    return jnp.concatenate([q_h, aug_rows] if h % 2 == 0 else [aug_rows, q_h], axis=0)


def _moba_kernel(qt_ref, k_ref, vt_ref, o_ref, kaug_sc, km_sc, knorm_sc, first_sc, qaug_sc, s0_sc, s_sc, p_sc, acc_sc,
                 m_sc, alpha_sc, *, nb):
    step = pl.program_id(1)
    i = jnp.minimum(step, nb - 1)
    slopes = [LOG2E * 2.0 ** (-8.0 * (h + 1) / ATT_HEADS) for h in range(ATT_HEADS)]

    def shift_of(h, win, tile):
        return (slopes[h] * KEY_WIN) * jnp.full((1, KV_BLOCK), win - tile // WIN_BLOCKS, jnp.int32).astype(F32)

    groups = ((2, 3), (1,), (0,))
    near_heads = tuple(h for g in groups[1:] for h in g)
    scratch = (kaug_sc, qaug_sc, vt_ref, shift_of, s0_sc, s_sc, p_sc, acc_sc, m_sc, alpha_sc)
    pipes = [_Flash(*scratch, heads=g) for g in groups]

    @pl.when(step == 0)
    def _():
        pipes[0].reset()
        for g in range(1, len(groups)):
            first_sc[g] = 0
        knorm_sc[...] = jnp.zeros(knorm_sc.shape, F32)
        km_sc[...] = jnp.zeros(km_sc.shape, F32)
        offs = lax.broadcasted_iota(jnp.int32, (KV_BLOCK, PAIR_W), 0).astype(F32)
        km_row = lax.broadcasted_iota(jnp.int32, (HEAD_DIM, PAIR_W), 0)

        def build(n, carry):
            c0 = pl.multiple_of(n * KV_BLOCK, KV_BLOCK)
            in_win = jnp.full((KV_BLOCK, PAIR_W), n % WIN_BLOCKS, jnp.int32).astype(F32)
            for j in range(ATT_HEADS // 2):
                kp = k_ref[pl.ds(c0, KV_BLOCK), j * PAIR_W:(j + 1) * PAIR_W]
                mean = jnp.sum(kp.astype(F32), axis=0, keepdims=True) * (1.0 / KV_BLOCK)
                for h in (2 * j, 2 * j + 1):
                    own, slot = _own_lanes(h, (KV_BLOCK, PAIR_W))
                    aug = jnp.where(slot == n, 1.0,
                                    jnp.where((slot >= AUG_POS) & (slot < AUG_POS + 3), offs,
                                              jnp.where((slot >= AUG_POS + 3) & (slot < AUG_POS + 6), in_win, 0.0)))
                    kaug_sc[pl.ds(c0, KV_BLOCK), h * PAIR_W:(h + 1) * PAIR_W] = jnp.where(own, kp, aug.astype(BF16))
                    own_km, _ = _own_lanes(h, (HEAD_DIM, PAIR_W))
                    km_sc[h] = jnp.where(km_row == n, jnp.where(own_km, mean, 0.0), km_sc[h])
                    if h in near_heads:
                        kf = jnp.where(own, kp.astype(F32), 0.0)
                        sq = jnp.max(jnp.sum(kf * kf, axis=1, keepdims=True), axis=0, keepdims=True)
                        knorm_sc[h:h + 1, :] = jnp.maximum(knorm_sc[h:h + 1, :], sq)
            return carry

        lax.fori_loop(0, nb, build, 0)

    blk_id = lax.broadcasted_iota(jnp.int32, (HEAD_DIM, KV_BLOCK), 0)
    blk_idf = blk_id.astype(F32)
    for h in range(ATT_HEADS):
        j = h // 2
        qp = qt_ref[j * PAIR_W:(j + 1) * PAIR_W, :]
        km_hi, km_mid, km_lo = _split3(km_sc[h])
        score = jnp.where(blk_id < i, _dot(km_hi, qp) + _dot(km_mid, qp) + _dot(km_lo, qp), NEG)
        sel = blk_id == i
        for _ in range(MOBA_TOPK):
            top = jnp.max(score, axis=0, keepdims=True)
            first = jnp.min(jnp.where(score == top, blk_idf, 1e9), axis=0, keepdims=True)
            hit = blk_idf == first
            sel = sel | (hit & (top > 0.5 * NEG))
            score = jnp.where(hit, 3.0 * NEG, score)
        terms = [t.astype(F32) for t in _split3(jnp.full((1, 1), slopes[h], F32))]
        terms += [t * KV_BLOCK for t in terms]
        aug = jnp.where(blk_id < AUG_POS, jnp.where(sel, 0.0, NEG), 0.0)
        for k, term in enumerate(terms):
            aug = jnp.where(blk_id == AUG_POS + k, term, aug)
        qaug_sc[h] = _stack_q(h, qt_ref[h * HEAD_DIM:(h + 1) * HEAD_DIM, :], aug.astype(BF16))

    own_win = i // WIN_BLOCKS
    firsts = [0]
    for g in groups[1:]:
        count = jnp.zeros((1, KV_BLOCK), F32)
        reach = {}
        for h in g:
            q2 = jnp.max(jnp.sum(jnp.square(qt_ref[h * HEAD_DIM:(h + 1) * HEAD_DIM, :].astype(F32)), axis=0,
                                 keepdims=True), axis=1, keepdims=True)
            reach[h] = (2.0 * BOUND_SLACK) * jnp.sqrt(q2 * knorm_sc[h:h + 1, :]) + UNDERFLOW
        for w in range(nb // WIN_BLOCKS):
            gap = jnp.full((1, KV_BLOCK), own_win - 1 - w, jnp.int32).astype(F32) * KEY_WIN
            matters = functools.reduce(jnp.maximum, [jnp.where(gap * slopes[h] < reach[h], 1.0, 0.0) for h in g])
            count = count + jnp.where(w < own_win, matters, 0.0)
        firsts.append(jnp.maximum(own_win - count[0, 0].astype(jnp.int32), firsts[-1]))

    prev = jnp.maximum(step - 1, 0)
    for g, pipe in enumerate(pipes):
        pipe.finish(prev, first_sc[g] if g else 0)
    _flash_output(o_ref, acc_sc)

    live = step < nb
    plans = [pipe.plan(i, first, live) for pipe, first in zip(pipes, firsts)]
    for pipe in pipes:
        pipe.own_scores(i)
    for h in range(ATT_HEADS):
        acc_sc[h] = jnp.zeros(acc_sc.shape[1:], F32)
    carries = [pipe.first_softmax(i, plan) for pipe, plan in zip(pipes, plans)]
    lag = [plans[0][2] - plan[2] for plan in plans]
    for k in range(len(groups)):
        lo = lag[k] + 1
        hi = lag[k + 1] + 1 if k + 1 < len(groups) else plans[0][2]

        def phase(t, active, k=k):
            return [pipes[g].step(t - lag[g], i, plans[g], active[g]) for g in range(k + 1)]

        carries[:k + 1] = lax.fori_loop(lo, hi, phase, carries[:k + 1])
    for g, pipe in enumerate(pipes):
        pipe.save(carries[g])
        if g:
            first_sc[g] = firsts[g]


def _moba(qt, k, vt, batch, seq):
    nb = seq // KV_BLOCK
    return pl.pallas_call(
        functools.partial(_moba_kernel, nb=nb),
        out_shape=jax.ShapeDtypeStruct((batch * seq, ATT_W), BF16),
        grid=(batch, nb + 1),
        in_specs=[pl.BlockSpec((ATT_W, KV_BLOCK), lambda b, i: (0, b * nb + jnp.minimum(i, nb - 1))),
                  pl.BlockSpec((seq, ATT_W), lambda b, i: (b, 0)),
                  pl.BlockSpec((ATT_W, seq), lambda b, i: (0, b))],
        out_specs=pl.BlockSpec((KV_BLOCK, ATT_W), lambda b, i: (b * nb + jnp.maximum(i - 1, 0), 0)),
        scratch_shapes=[pltpu.VMEM((seq, ATT_HEADS * PAIR_W), BF16),
                        pltpu.VMEM((ATT_HEADS, HEAD_DIM, PAIR_W), F32),
                        pltpu.VMEM((ATT_HEADS, KV_BLOCK), F32),
                        pltpu.SMEM((ATT_HEADS,), jnp.int32)]
                       + _flash_scratch(),
        compiler_params=pltpu.CompilerParams(dimension_semantics=("arbitrary", "arbitrary"),
                                             vmem_limit_bytes=VMEM_LIMIT),
        name="moba",
    )(qt, k, vt)


def _fox_kernel(qt_ref, k_ref, vt_ref, sm_ref, fb_ref, o_ref, kaug_sc, cstart_sc, cend_sc, knorm_sc, first_sc,
                qaug_sc, s0_sc, s_sc, p_sc, acc_sc, m_sc, alpha_sc, *, nb):
    step = pl.program_id(1)
    i = jnp.minimum(step, nb - 1)

    def shift_of(h, win, tile):
        return -cstart_sc[win][h:h + 1, :]

    flash = _Flash(kaug_sc, qaug_sc, vt_ref, shift_of, s0_sc, s_sc, p_sc, acc_sc, m_sc, alpha_sc)

    @pl.when(step == 0)
    def _():
        flash.reset()
        first_sc[0] = 0
        r = lax.broadcasted_iota(jnp.int32, (KV_BLOCK, KV_BLOCK), 0)
        c = lax.broadcasted_iota(jnp.int32, (KV_BLOCK, KV_BLOCK), 1)
        tril = (c <= r).astype(BF16)
        lane = lax.broadcasted_iota(jnp.int32, (KV_BLOCK, PAIR_W), 1)
        term = (lane % HEAD_DIM - AUG_POS) % 3

        def head_rows(v):
            return jnp.concatenate(
                [jnp.broadcast_to(v[0:1, _f_lane(h):_f_lane(h) + 1], (1, KV_BLOCK)) for h in range(ATT_HEADS)], axis=0)

        def scan(n, carry):
            c0 = pl.multiple_of(n * KV_BLOCK, KV_BLOCK)
            first = jnp.full((8, SMALL_W), n % WIN_BLOCKS, jnp.int32) == 0
            blk_start, win_start = carry[0], jnp.where(first, carry[0], carry[1])
            logf = -LOG2E * _softplus(-(sm_ref[pl.ds(c0, KV_BLOCK), :] + fb_ref[...]))
            hi, mid, lo = _split3(logf)
            cs = _dot(tril, hi) + _dot(tril, mid) + _dot(tril, lo)
            in_win = cs + (blk_start - win_start)[0:1, :]
            t_hi, t_mid, t_lo = (t.astype(F32) for t in _split3(in_win))
            terms = jnp.where(term == 0, t_hi, jnp.where(term == 1, t_mid, t_lo))
            blk_end = blk_start + jnp.broadcast_to(cs[KV_BLOCK - 1:KV_BLOCK, :], (8, SMALL_W))
            cstart_sc[n // WIN_BLOCKS] = head_rows(win_start)
            cend_sc[n // WIN_BLOCKS] = head_rows(blk_end)
            knorm = list(carry[2:])
            for h in range(ATT_HEADS):
                j = h // 2
                own, _ = _own_lanes(h, (KV_BLOCK, PAIR_W))
                mine = (lane >= _f_lane(h)) & (lane < _f_lane(h) + 3)
                kp = k_ref[pl.ds(c0, KV_BLOCK), j * PAIR_W:(j + 1) * PAIR_W]
                kaug_sc[pl.ds(c0, KV_BLOCK), h * PAIR_W:(h + 1) * PAIR_W] = jnp.where(
                    own, kp, jnp.where(mine, terms, 0.0).astype(BF16))
                kf = jnp.where(own, kp.astype(F32), 0.0)
                sq = jnp.max(jnp.sum(kf * kf, axis=1, keepdims=True), axis=0, keepdims=True)
                knorm[h] = jnp.maximum(knorm[h], jnp.broadcast_to(sq, (8, SMALL_W)))
            return (blk_end, win_start, *knorm)

        zero = jnp.broadcast_to(fb_ref[...] * 0.0, (8, SMALL_W))
        done = lax.fori_loop(0, nb, scan, (zero,) * (2 + ATT_HEADS))
        knorm_sc[...] = jnp.concatenate(
            [jnp.concatenate([v[0:1, :]] * (KV_BLOCK // SMALL_W), axis=1) for v in done[2:]], axis=0)

    aug_id = lax.broadcasted_iota(jnp.int32, (HEAD_DIM, KV_BLOCK), 0)
    for h in range(ATT_HEADS):
        slot = _f_lane(h) % HEAD_DIM
        aug = jnp.where((aug_id >= slot) & (aug_id < slot + 3), -1.0, 0.0).astype(BF16)
        qaug_sc[h] = _stack_q(h, qt_ref[h * HEAD_DIM:(h + 1) * HEAD_DIM, :], aug)

    own_win = i // WIN_BLOCKS
    q2 = [jnp.max(jnp.sum(jnp.square(qt_ref[h * HEAD_DIM:(h + 1) * HEAD_DIM, :].astype(F32)), axis=0, keepdims=True),
                  axis=1, keepdims=True) for h in range(ATT_HEADS)]
    bound = jnp.sqrt(jnp.concatenate(q2, axis=0) * knorm_sc[...])
    reach = (2.0 * BOUND_SLACK) * bound + UNDERFLOW
    c_own = cstart_sc[own_win]
    count = jnp.zeros((1, KV_BLOCK), F32)
    for w in range(nb // WIN_BLOCKS):
        matters = jnp.max(jnp.where(cend_sc[w] - c_own < reach, 1.0, 0.0), axis=0, keepdims=True)
        count = count + jnp.where(w < own_win, matters, 0.0)
    first_win = own_win - count[0, 0].astype(jnp.int32)

    flash.finish(jnp.maximum(step - 1, 0), first_sc[0])
    _flash_output(o_ref, acc_sc)
    flash.own_scores(i)
    flash.run(i, first_win, step < nb)
    first_sc[0] = first_win


def _fox(qt, k, vt, sm, W, layer, batch, seq):
    nb = seq // KV_BLOCK
    return pl.pallas_call(
        functools.partial(_fox_kernel, nb=nb),
        out_shape=jax.ShapeDtypeStruct((batch * seq, ATT_W), BF16),
        grid=(batch, nb + 1),
        in_specs=[pl.BlockSpec((ATT_W, KV_BLOCK), lambda b, i: (1, b * nb + jnp.minimum(i, nb - 1))),
                  pl.BlockSpec((seq, ATT_W), lambda b, i: (b, 1)),
                  pl.BlockSpec((ATT_W, seq), lambda b, i: (1, b)),
                  pl.BlockSpec((seq, SMALL_W), lambda b, i: (b, 0)),
                  pl.BlockSpec((None, 1, SMALL_W), lambda b, i: (layer, 0, 0))],
        out_specs=pl.BlockSpec((KV_BLOCK, ATT_W), lambda b, i: (b * nb + jnp.maximum(i - 1, 0), 0)),
        scratch_shapes=[pltpu.VMEM((seq, ATT_HEADS * PAIR_W), BF16),
                        pltpu.VMEM((nb // WIN_BLOCKS, ATT_HEADS, KV_BLOCK), F32),
                        pltpu.VMEM((nb // WIN_BLOCKS, ATT_HEADS, KV_BLOCK), F32),
                        pltpu.VMEM((ATT_HEADS, KV_BLOCK), F32),
                        pltpu.SMEM((1,), jnp.int32)]
                       + _flash_scratch(),
        compiler_params=pltpu.CompilerParams(dimension_semantics=("arbitrary", "arbitrary"),
                                             vmem_limit_bytes=VMEM_LIMIT),
        name="fox",
    )(qt, k, vt, sm, W["fox_fb"])


def _ssd_kernel(xbc_ref, z_ref, sm_ref, cw_ref, cb_ref, dtb_ref, alog_ref, dskip_ref, nw_ref, o_ref,
                ext_sc, state_sc, y_sc):
    c = pl.program_id(1)
    L = SSD_CHUNK
    tail = 8

    @pl.when(c == 0)
    def _():
        ext_sc[0:tail, :] = jnp.zeros((tail, SSD_XBC), F32)
        state_sc[...] = jnp.zeros(state_sc.shape, F32)

    ext_sc[tail:tail + L, :] = xbc_ref[...]
    u = cb_ref[...]
    for k in range(SSD_CONV):
        back = SSD_CONV - 1 - k
        u = u + cw_ref[k:k + 1, :] * ext_sc[tail - back:tail - back + L, :]
    ext_sc[0:tail, :] = ext_sc[L:L + tail, :]
    xc = _silu(u)

    dt = _softplus(sm_ref[...] + dtb_ref[...])
    a = dt * (-jnp.exp(alog_ref[...]))
    r = lax.broadcasted_iota(jnp.int32, (L, L), 0)
    cidx = lax.broadcasted_iota(jnp.int32, (L, L), 1)
    causal = cidx <= r
    tril = causal.astype(BF16)
    a_hi, a_mid, a_lo = _split3(a)
    acs = (_dot(tril, a_hi) + _dot(tril, a_mid) + _dot(tril, a_lo)) * LOG2E
    acs_t = acs.T

    widen_m = (lax.broadcasted_iota(jnp.int32, (SMALL_W, SSD_INNER), 0)
               == lax.broadcasted_iota(jnp.int32, (SMALL_W, SSD_INNER), 1) // SSD_P).astype(BF16)

    def widen(v):
        hi, mid, lo = _split3(v)
        return _dot(hi, widen_m) + _dot(mid, widen_m) + _dot(lo, widen_m)

    total = acs[L - 1:L, :]
    xs = xc[:, 0:SSD_INNER]
    xh = xs * widen(dt)
    x_end = (xh * widen(jnp.exp2(total - acs))).astype(BF16)
    from_start = widen(jnp.exp2(acs))
    keep = widen(jnp.broadcast_to(jnp.exp2(total), (8, SMALL_W)))[0:1, :]
    lane = lax.broadcasted_iota(jnp.int32, (L, PAIR_W), 1)
    heads_per_group = SSD_HEADS // SSD_GROUPS
    gw = heads_per_group * SSD_P

    for g in range(SSD_GROUPS):
        bg = xc[:, SSD_INNER + g * SSD_N:SSD_INNER + (g + 1) * SSD_N].astype(BF16)
        cg = xc[:, SSD_INNER + (SSD_GROUPS + g) * SSD_N:SSD_INNER + (SSD_GROUPS + g + 1) * SSD_N].astype(BF16)
        bgt = xc[:, SSD_INNER + g * SSD_N:SSD_INNER + (g + 1) * SSD_N].T.astype(BF16)
        cb = jnp.where(causal, _dot_nt(cg, bg), 0.0)
        gcols = slice(g * gw, (g + 1) * gw)
        st = state_sc[:, gcols]
        y_off = from_start[:, gcols] * _dot(cg, st.astype(BF16))
        state_sc[:, gcols] = keep[:, gcols] * st + _dot(bgt, x_end[:, gcols])
        for pair in range(heads_per_group // 2):
            h0 = g * heads_per_group + 2 * pair
            pcols = slice(h0 * SSD_P, (h0 + 2) * SSD_P)
            xp = xh[:, pcols].astype(BF16)
            y_pair = []
            for h in (h0, h0 + 1):
                decay = jnp.exp2(jnp.minimum(acs[:, h:h + 1] - acs_t[h:h + 1, :], 0.0))
                y_pair.append(_dot((cb * decay).astype(BF16), xp))
            y_diag = jnp.where(lane < SSD_P, y_pair[0], y_pair[1])
            y_sc[:, pcols] = (y_diag + y_off[:, pair * PAIR_W:(pair + 1) * PAIR_W]
                              + dskip_ref[:, pcols] * xs[:, pcols])

    y = y_sc[...] * _silu(z_ref[...])
    o_ref[...] = _rms(y, nw_ref[...]).astype(o_ref.dtype)


def _ssd(xbc, z, sm, W, layer, batch, seq):
    nc = seq // SSD_CHUNK
    L = SSD_CHUNK
    row = lambda b, c: (b * nc + c, 0)
    wspec = lambda a: pl.BlockSpec((None,) + a.shape[1:], lambda b, c: (layer, 0, 0))
    ws = tuple(W[n] for n in ("conv_w", "conv_b", "dt_bias", "a_log", "d_skip", "ssd_norm_w"))
    return pl.pallas_call(
        _ssd_kernel,
        out_shape=jax.ShapeDtypeStruct((batch * seq, SSD_INNER), BF16),
        grid=(batch, nc),
        in_specs=[pl.BlockSpec((L, SSD_XBC), row), pl.BlockSpec((L, SSD_INNER), row),
                  pl.BlockSpec((L, SMALL_W), row)] + [wspec(a) for a in ws],
        out_specs=pl.BlockSpec((L, SSD_INNER), row),
        scratch_shapes=[pltpu.VMEM((L + 8, SSD_XBC), F32),
                        pltpu.VMEM((SSD_N, SSD_INNER), F32),
                        pltpu.VMEM((L, SSD_INNER), F32)],
        compiler_params=pltpu.CompilerParams(dimension_semantics=("arbitrary", "arbitrary"),
                                             vmem_limit_bytes=VMEM_LIMIT),
        name="ssd",
    )(xbc, z, sm, *ws)


def _post_kernel(x_ref, a_ref, b_ref, c_ref, p_ref, wo_ref, gm_ref, w1_ref, w2_ref, gp_ref, wg_ref, wp_ref,
                 gf_ref, o_ref, acc_sc, h_sc, *, final):
    j = pl.program_id(1)

    @pl.when(j == 0)
    def _():
        mixed = jnp.concatenate([a_ref[...], b_ref[...], c_ref[...]], axis=1)
        x1 = x_ref[...] + _dot(mixed, wo_ref[...])
        acc_sc[...] = x1
        h_sc[...] = _rms(x1, gm_ref[...]).astype(BF16)

    u = jnp.maximum(_dot(h_sc[...], w1_ref[...]), 0.0)
    acc_sc[...] += _dot((u * u).astype(BF16), w2_ref[...])

    @pl.when(j == pl.num_programs(1) - 1)
    def _():
        x2 = acc_sc[...]
        gate = 1.0 / (1.0 + jnp.exp(-_dot(_rms(x2, gp_ref[...]).astype(BF16), wg_ref[...])))
        x3 = x2 + gate * _dot(p_ref[...].astype(BF16), wp_ref[...])
        o_ref[...] = _rms(x3, gf_ref[...]) if final else x3


def _post(x, a, b, c, W, layer, final):
    t, d = x.shape
    tm = POST_ROW_TILE
    d_ff = W["w_ff1"].shape[2]
    row = lambda i, j: (i, 0)
    wspec = lambda n: pl.BlockSpec((None,) + W[n].shape[1:], lambda i, j: (layer, 0, 0))
    return pl.pallas_call(
        functools.partial(_post_kernel, final=final),
        out_shape=jax.ShapeDtypeStruct((t, d), F32),
        grid=(t // tm, d_ff // FF_CHUNK),
        in_specs=[pl.BlockSpec((tm, d), row), pl.BlockSpec((tm, ATT_W), row), pl.BlockSpec((tm, SSD_INNER), row),
                  pl.BlockSpec((tm, ATT_W), row), pl.BlockSpec((None, tm, PLE_DIM), lambda i, j: (layer, i, 0)),
                  wspec("w_out"), wspec("g_mlp"),
                  pl.BlockSpec((None, d, FF_CHUNK), lambda i, j: (layer, 0, j)),
                  pl.BlockSpec((None, FF_CHUNK, d), lambda i, j: (layer, j, 0)),
                  wspec("g_ple"), wspec("w_ple_gate"), wspec("w_ple_proj"),
                  pl.BlockSpec(W["g_final"].shape, lambda i, j: (0, 0))],
        out_specs=pl.BlockSpec((tm, d), row),
        scratch_shapes=[pltpu.VMEM((tm, d), F32), pltpu.VMEM((tm, d), BF16)],
        compiler_params=pltpu.CompilerParams(dimension_semantics=("arbitrary", "arbitrary"),
                                             vmem_limit_bytes=VMEM_LIMIT),
        name="post",
    )(x, a, b, c, W["p"], W["w_out"], W["g_mlp"], W["w_ff1"], W["w_ff2"], W["g_ple"], W["w_ple_gate"],
      W["w_ple_proj"], W["g_final"])


def _prepare(p, g_mix, w_in, conv_w, conv_b, dt_bias, a_log, d_skip, ssd_norm_w, fox_f_bias, w_out,
             g_mlp, w_ff1, w_ff2, g_ple, w_ple_gate, w_ple_proj, g_final):
    depth, d, _ = w_in.shape
    o = 0
    mq, mk, mv = (w_in[:, :, o + k * ATT_W:o + (k + 1) * ATT_W] for k in range(3)); o += 3 * ATT_W
    wz = w_in[:, :, o:o + SSD_INNER]; o += SSD_INNER
    wxbc = w_in[:, :, o:o + SSD_XBC]; o += SSD_XBC
    wdt = w_in[:, :, o:o + SSD_HEADS]; o += SSD_HEADS
    fq, fk, fv = (w_in[:, :, o + k * ATT_W:o + (k + 1) * ATT_W] for k in range(3)); o += 3 * ATT_W
    wf = w_in[:, :, o:o + ATT_HEADS]
    src = [SSD_HEADS + ATT_HEADS] * SMALL_W
    for j in range(SSD_HEADS):
        src[j] = j
    for h in range(ATT_HEADS):
        for k in range(3):
            src[_f_lane(h) + k] = SSD_HEADS + h
    src = jnp.asarray(src, jnp.int32)
    small = jnp.take(jnp.concatenate([wdt, wf, jnp.zeros_like(wf[:, :, :1])], axis=2), src, axis=2)
    fb = jnp.take(jnp.concatenate([jnp.zeros((depth, SSD_HEADS), F32), fox_f_bias, jnp.zeros((depth, 1), F32)],
                                  axis=1), src, axis=1)
    w_all = jnp.concatenate([mq, fq, mv, fv, mk, fk, wz, wxbc, small], axis=2).astype(BF16)
    assert w_all.shape[2] == IN_COLS["sm"][0] + SMALL_W
    row = lambda a: a[:, None, :]
    pad = lambda a: jnp.pad(a, ((0, 0), (0, SMALL_W - a.shape[1])))
    return {
        "w_in": w_all, "g_mix": row(g_mix),
        "conv_w": conv_w, "conv_b": row(conv_b), "dt_bias": row(pad(dt_bias)), "a_log": row(pad(a_log)),
        "d_skip": row(jnp.repeat(d_skip, SSD_P, axis=1)), "ssd_norm_w": row(ssd_norm_w),
        "fox_fb": row(fb),
        "p": p.reshape(depth, -1, p.shape[-1]),
        "w_out": w_out.astype(BF16), "g_mlp": row(g_mlp), "w_ff1": w_ff1.astype(BF16), "w_ff2": w_ff2.astype(BF16),
        "g_ple": row(g_ple), "w_ple_gate": w_ple_gate.astype(BF16), "w_ple_proj": w_ple_proj.astype(BF16),
        "g_final": g_final[None, :],
    }


def kernel(x, p, g_mix, w_in, conv_w, conv_b, dt_bias, a_log, d_skip, ssd_norm_w, fox_f_bias, w_out, g_mlp, w_ff1, w_ff2, g_ple, w_ple_gate, w_ple_proj, g_final):
    batch, seq, d = x.shape
    depth = w_in.shape[0]
    assert seq % KEY_WIN == 0 and seq % SSD_CHUNK == 0
    assert (batch * seq) % ROW_TILE == 0 and (batch * seq) % POST_ROW_TILE == 0 and w_ff1.shape[2] % FF_CHUNK == 0
    assert seq // KV_BLOCK <= AUG_POS, "one-hot block ids use the augmentation slots below AUG_POS"
    xt = x.reshape(batch * seq, d)
    W = _prepare(p, g_mix, w_in, conv_w, conv_b, dt_bias, a_log, d_skip, ssd_norm_w, fox_f_bias, w_out,
                 g_mlp, w_ff1, w_ff2, g_ple, w_ple_gate, w_ple_proj, g_final)
    for i in range(depth):
        qt, k, vt, z, xbc, sm = _inproj(xt, W, i)
        out_a = _moba(qt, k, vt, batch, seq)
        out_b = _ssd(xbc, z, sm, W, i, batch, seq)
        out_c = _fox(qt, k, vt, sm, W, i, batch, seq)
        xt = _post(xt, out_a, out_b, out_c, W, i, final=(i == depth - 1))
    return xt.reshape(batch, seq, d)
```

```python
import functools

import jax
import jax.numpy as jnp
from jax import lax
from jax.experimental import pallas as pl
from jax.experimental.pallas import tpu as pltpu

F32 = jnp.float32
BF16 = jnp.bfloat16

HEAD_DIM = 64
ATT_HEADS = 4
ATT_W = ATT_HEADS * HEAD_DIM
PAIR_W = 2 * HEAD_DIM
KV_BLOCK = 256
WIN_BLOCKS = 2
KEY_WIN = WIN_BLOCKS * KV_BLOCK
SOFTMAX_ROWS = 64
MOBA_TOPK = 3
AUG_POS = 32
ACC_ROWS = HEAD_DIM + 16
LOG2E = 1.4426950408889634
UNDERFLOW = 152.0
BOUND_SLACK = 1.01
SSD_HEADS = 8
SSD_P = 64
SSD_INNER = SSD_HEADS * SSD_P
SSD_GROUPS = 2
SSD_N = 128
SSD_CONV = 4
SSD_CHUNK = 256
SSD_XBC = SSD_INNER + 2 * SSD_GROUPS * SSD_N
SMALL_W = 128
IN_COLS = {"qv": (0, 4 * ATT_W), "k": (4 * ATT_W, 2 * ATT_W), "z": (6 * ATT_W, SSD_INNER),
           "xbc": (6 * ATT_W + SSD_INNER, SSD_XBC), "sm": (6 * ATT_W + SSD_INNER + SSD_XBC, SMALL_W)}
PLE_DIM = 256
RMS_EPS = 1e-6
NEG = -1e30
LANES = 128
VMEM_LIMIT = 56 * 1024 * 1024

ROW_TILE = 1024
POST_ROW_TILE = 1024
FF_CHUNK = 1024


def _rms(x, g):
    return x * lax.rsqrt(jnp.mean(x * x, axis=-1, keepdims=True) + RMS_EPS) * g


def _dot(a, b):
    return jnp.dot(a, b, preferred_element_type=F32)


def _dot_nt(a, b):
    return lax.dot_general(a, b, (((1,), (1,)), ((), ())), preferred_element_type=F32)


def _split3(a):
    hi = a.astype(BF16)
    r1 = a - hi.astype(F32)
    mid = r1.astype(BF16)
    lo = (r1 - mid.astype(F32)).astype(BF16)
    return hi, mid, lo


def _softplus(x):
    return jnp.maximum(x, 0.0) + jnp.log(1.0 + jnp.exp(-jnp.abs(x)))


def _silu(x):
    return x * (1.0 / (1.0 + jnp.exp(-x)))


def _inproj_kernel(x_ref, g_ref, wqv_ref, wk_ref, wz_ref, wxbc_ref, wsm_ref,
                   qt_ref, k_ref, vt_ref, z_ref, xbc_ref, sm_ref, wqvt_sc):
    @pl.when(pl.program_id(0) == 0)
    def _():
        wqvt_sc[...] = wqv_ref[...].astype(F32).T.astype(BF16)

    h = _rms(x_ref[...], g_ref[...]).astype(BF16)
    qt_ref[...] = (_dot_nt(wqvt_sc[0:2 * ATT_W, :], h) * (HEAD_DIM ** -0.5 * LOG2E)).astype(BF16)
    k_ref[...] = _dot(h, wk_ref[...]).astype(BF16)
    vt_ref[...] = _dot_nt(wqvt_sc[2 * ATT_W:, :], h).astype(BF16)
    z_ref[...] = _dot(h, wz_ref[...])
    xbc_ref[...] = _dot(h, wxbc_ref[...])
    sm_ref[...] = _dot(h, wsm_ref[...])


def _inproj(x, W, layer):
    t, d = x.shape
    tm = ROW_TILE
    row = lambda i: (i, 0)
    col = lambda i: (0, i)

    def wspec(name):
        start, width = IN_COLS[name]
        assert start % width == 0
        return pl.BlockSpec((None, d, width), lambda i: (layer, 0, start // width))
    outs = (
        jax.ShapeDtypeStruct((2 * ATT_W, t), BF16),
        jax.ShapeDtypeStruct((t, 2 * ATT_W), BF16),
        jax.ShapeDtypeStruct((2 * ATT_W, t), BF16),
        jax.ShapeDtypeStruct((t, SSD_INNER), F32),
        jax.ShapeDtypeStruct((t, SSD_XBC), F32),
        jax.ShapeDtypeStruct((t, SMALL_W), F32),
    )
    names = ("qv", "k", "z", "xbc", "sm")
    return pl.pallas_call(
        _inproj_kernel,
        out_shape=outs,
        grid=(t // tm,),
        in_specs=[pl.BlockSpec((tm, d), row), pl.BlockSpec((None, 1, d), lambda i: (layer, 0, 0))]
                 + [wspec(n) for n in names],
        out_specs=(pl.BlockSpec((2 * ATT_W, tm), col), pl.BlockSpec((tm, 2 * ATT_W), row),
                   pl.BlockSpec((2 * ATT_W, tm), col), pl.BlockSpec((tm, SSD_INNER), row),
                   pl.BlockSpec((tm, SSD_XBC), row), pl.BlockSpec((tm, SMALL_W), row)),
        scratch_shapes=[pltpu.VMEM((4 * ATT_W, d), BF16)],
        compiler_params=pltpu.CompilerParams(dimension_semantics=("arbitrary",), vmem_limit_bytes=VMEM_LIMIT),
        name="inproj",
    )(x, W["g_mix"], *([W["w_in"]] * len(names)))


class _Flash:
    def __init__(self, kaug_sc, qaug_sc, vt_ref, shift_of, s0_sc, s_sc, p_sc, acc_sc, m_sc, alpha_sc,
                 heads=range(ATT_HEADS)):
        self.kaug_sc, self.qaug_sc, self.vt_ref, self.shift_of = kaug_sc, qaug_sc, vt_ref, shift_of
        self.s0_sc, self.s_sc, self.p_sc, self.acc_sc = s0_sc, s_sc, p_sc, acc_sc
        self.m_sc, self.alpha_sc = m_sc, alpha_sc
        self.heads = tuple(heads)
        self.ones = jnp.ones((ACC_ROWS - HEAD_DIM, KEY_WIN), BF16)

    def reset(self):
        self.s_sc[...] = jnp.zeros(self.s_sc.shape, F32)
        self.p_sc[...] = jnp.zeros(self.p_sc.shape, BF16)
        self.acc_sc[...] = jnp.ones(self.acc_sc.shape, F32)
        self.m_sc[...] = jnp.zeros(self.m_sc.shape, F32)
        self.alpha_sc[...] = jnp.ones(self.alpha_sc.shape, F32)

    def scores(self, h, win):
        c0 = pl.multiple_of(win * KEY_WIN, KEY_WIN)
        return _dot(self.kaug_sc[pl.ds(c0, KEY_WIN), h * PAIR_W:(h + 1) * PAIR_W], self.qaug_sc[h])

    @staticmethod
    def plan(tile, first_win, live=True):
        own_win = tile // WIN_BLOCKS
        n_past = jnp.where(live, own_win - first_win, 0)
        dead = jnp.full((1, KV_BLOCK), n_past, jnp.int32) == 0
        return own_win, first_win, jnp.maximum(n_past, 1), jnp.where(dead, 1.0, 0.0)

    @staticmethod
    def tile_win(t, plan):
        own_win, first_win, _, _ = plan
        return jnp.where(t < 1, own_win, first_win + t - 1)

    def pv_stage(self, t, plan, alpha):
        c0 = pl.multiple_of(self.tile_win(t, plan) * KEY_WIN, KEY_WIN)
        for h in self.heads:
            vt_h = jnp.concatenate([self.vt_ref[h * HEAD_DIM:(h + 1) * HEAD_DIM, pl.ds(c0, KEY_WIN)], self.ones],
                                   axis=0)
            self.acc_sc[h] = alpha[h] * self.acc_sc[h] + _dot(vt_h, self.p_sc[h])

    def softmax_stage(self, t, tile, plan, m, src, refill):
        m_out, alpha = {}, {}
        for h in self.heads:
            chunks = [pl.ds(r, SOFTMAX_ROWS) for r in range(0, KEY_WIN, SOFTMAX_ROWS)]
            top = src[h, chunks[0], :]
            for rows in chunks[1:]:
                top = jnp.maximum(top, src[h, rows, :])
            shift = self.shift_of(h, self.tile_win(t, plan), tile)
            if src is self.s_sc:
                shift = shift + plan[3] * NEG
            m_new = jnp.maximum(m[h], jnp.max(top, axis=0, keepdims=True) + shift)
            base = m_new - shift
            for rows in chunks:
                self.p_sc[h, rows, :] = jnp.exp2(src[h, rows, :] - base).astype(BF16)
            if refill:
                self.s_sc[h] = self.scores(h, self.tile_win(t + 1, plan))
            m_out[h] = m_new
            alpha[h] = jnp.exp2(m[h] - m_new)
        return m_out, alpha

    def own_scores(self, tile):
        own_win = tile // WIN_BLOCKS
        key = lax.broadcasted_iota(jnp.int32, (KEY_WIN, KV_BLOCK), 0) + (own_win * KEY_WIN - tile * KV_BLOCK)
        causal = key <= lax.broadcasted_iota(jnp.int32, (KEY_WIN, KV_BLOCK), 1)
        for h in self.heads:
            self.s0_sc[h] = jnp.where(causal, self.scores(h, own_win), NEG)

    def finish(self, tile, first_win):
        plan = self.plan(tile, first_win)
        n = plan[2]
        m = {h: self.m_sc[h] for h in self.heads}
        alpha = {h: self.alpha_sc[h] for h in self.heads}
        self.pv_stage(n - 1, plan, alpha)
        m, alpha = self.softmax_stage(n, tile, plan, m, self.s_sc, refill=False)
        self.pv_stage(n, plan, alpha)

    def save(self, carry):
        m, alpha = carry
        for h in self.heads:
            self.m_sc[h] = m[h]
            self.alpha_sc[h] = alpha[h]

    def first_softmax(self, tile, plan):
        start = {h: jnp.full((1, KV_BLOCK), NEG, F32) for h in self.heads}
        return self.softmax_stage(0, tile, plan, start, self.s0_sc, refill=True)

    def step(self, t, tile, plan, carry):
        m, alpha = carry
        self.pv_stage(t - 1, plan, alpha)
        return self.softmax_stage(t, tile, plan, m, self.s_sc, refill=True)

    def run(self, tile, first_win, live):
        plan = self.plan(tile, first_win, live)
        for h in self.heads:
            self.acc_sc[h] = jnp.zeros(self.acc_sc.shape[1:], F32)
        carry = self.first_softmax(tile, plan)
        self.save(lax.fori_loop(1, plan[2], lambda t, c: self.step(t, tile, plan, c), carry))


def _flash_output(o_ref, acc_sc):
    out_t = jnp.concatenate([acc_sc[h][0:HEAD_DIM, :] / acc_sc[h][HEAD_DIM:HEAD_DIM + 1, :]
                             for h in range(ATT_HEADS)], axis=0)
    o_ref[...] = out_t.T.astype(o_ref.dtype)


def _flash_groups(pipes, firsts, first_sc, tile, step, nb, o_ref, acc_sc):
    prev = jnp.maximum(step - 1, 0)
    for g, pipe in enumerate(pipes):
        pipe.finish(prev, first_sc[g])
    _flash_output(o_ref, acc_sc)

    plans = [pipe.plan(tile, first, step < nb) for pipe, first in zip(pipes, firsts)]
    for pipe in pipes:
        pipe.own_scores(tile)
    for h in range(ATT_HEADS):
        acc_sc[h] = jnp.zeros(acc_sc.shape[1:], F32)
    carries = [pipe.first_softmax(tile, plan) for pipe, plan in zip(pipes, plans)]
    lag = [plans[0][2] - plan[2] for plan in plans]
    for k in range(len(pipes)):
        lo = lag[k] + 1
        hi = lag[k + 1] + 1 if k + 1 < len(pipes) else plans[0][2]

        def phase(t, active, k=k):
            return [pipes[g].step(t - lag[g], tile, plans[g], active[g]) for g in range(k + 1)]

        carries[:k + 1] = lax.fori_loop(lo, hi, phase, carries[:k + 1])
    for g, pipe in enumerate(pipes):
        pipe.save(carries[g])
        first_sc[g] = firsts[g]


def _flash_scratch():
    return [pltpu.VMEM((ATT_HEADS, PAIR_W, KV_BLOCK), BF16),
            pltpu.VMEM((ATT_HEADS, KEY_WIN, KV_BLOCK), F32),
            pltpu.VMEM((ATT_HEADS, KEY_WIN, KV_BLOCK), F32),
            pltpu.VMEM((ATT_HEADS, KEY_WIN, KV_BLOCK), BF16),
            pltpu.VMEM((ATT_HEADS, ACC_ROWS, KV_BLOCK), F32),
            pltpu.VMEM((ATT_HEADS, 1, KV_BLOCK), F32),
            pltpu.VMEM((ATT_HEADS, 1, KV_BLOCK), F32)]


def _f_lane(h):
    return (HEAD_DIM if h % 2 == 0 else 0) + AUG_POS + 3 * (h // 2)


def _own_lanes(h, shape):
    lane = lax.broadcasted_iota(jnp.int32, shape, 1)
    if h % 2 == 0:
        return lane < HEAD_DIM, lane - HEAD_DIM
    return lane >= HEAD_DIM, lane


def _stack_q(h, q_h, aug_rows):
    return jnp.concatenate([q_h, aug_rows] if h % 2 == 0 else [aug_rows, q_h], axis=0)


def _moba_kernel(qt_ref, k_ref, vt_ref, o_ref, kaug_sc, km_sc, knorm_sc, first_sc, qaug_sc, s0_sc, s_sc, p_sc, acc_sc,
                 m_sc, alpha_sc, *, nb):
    step = pl.program_id(1)
    i = jnp.minimum(step, nb - 1)
    slopes = [LOG2E * 2.0 ** (-8.0 * (h + 1) / ATT_HEADS) for h in range(ATT_HEADS)]

    def shift_of(h, win, tile):
        return (slopes[h] * KEY_WIN) * jnp.full((1, KV_BLOCK), win - tile // WIN_BLOCKS, jnp.int32).astype(F32)

    groups = ((2, 3), (1,), (0,))
    near_heads = tuple(h for g in groups[1:] for h in g)
    scratch = (kaug_sc, qaug_sc, vt_ref, shift_of, s0_sc, s_sc, p_sc, acc_sc, m_sc, alpha_sc)
    pipes = [_Flash(*scratch, heads=g) for g in groups]

    @pl.when(step == 0)
    def _():
        pipes[0].reset()
        for g in range(len(groups)):
            first_sc[g] = 0
        knorm_sc[...] = jnp.zeros(knorm_sc.shape, F32)
        km_sc[...] = jnp.zeros(km_sc.shape, F32)
        offs = lax.broadcasted_iota(jnp.int32, (KV_BLOCK, PAIR_W), 0).astype(F32)
        km_row = lax.broadcasted_iota(jnp.int32, (HEAD_DIM, PAIR_W), 0)

        def build(n, carry):
            c0 = pl.multiple_of(n * KV_BLOCK, KV_BLOCK)
            in_win = jnp.full((KV_BLOCK, PAIR_W), n % WIN_BLOCKS, jnp.int32).astype(F32)
            for j in range(ATT_HEADS // 2):
                kp = k_ref[pl.ds(c0, KV_BLOCK), j * PAIR_W:(j + 1) * PAIR_W]
                mean = jnp.sum(kp.astype(F32), axis=0, keepdims=True) * (1.0 / KV_BLOCK)
                for h in (2 * j, 2 * j + 1):
                    own, slot = _own_lanes(h, (KV_BLOCK, PAIR_W))
                    aug = jnp.where(slot == n, 1.0,
                                    jnp.where((slot >= AUG_POS) & (slot < AUG_POS + 3), offs,
                                              jnp.where((slot >= AUG_POS + 3) & (slot < AUG_POS + 6), in_win, 0.0)))
                    kaug_sc[pl.ds(c0, KV_BLOCK), h * PAIR_W:(h + 1) * PAIR_W] = jnp.where(own, kp, aug.astype(BF16))
                    own_km, _ = _own_lanes(h, (HEAD_DIM, PAIR_W))
                    km_sc[h] = jnp.where(km_row == n, jnp.where(own_km, mean, 0.0), km_sc[h])
                    if h in near_heads:
                        kf = jnp.where(own, kp.astype(F32), 0.0)
                        sq = jnp.max(jnp.sum(kf * kf, axis=1, keepdims=True), axis=0, keepdims=True)
                        knorm_sc[h:h + 1, :] = jnp.maximum(knorm_sc[h:h + 1, :], sq)
            return carry

        lax.fori_loop(0, nb, build, 0)

    blk_id = lax.broadcasted_iota(jnp.int32, (HEAD_DIM, KV_BLOCK), 0)
    blk_idf = blk_id.astype(F32)
    for h in range(ATT_HEADS):
        j = h // 2
        qp = qt_ref[j * PAIR_W:(j + 1) * PAIR_W, :]
        km_hi, km_mid, km_lo = _split3(km_sc[h])
        score = jnp.where(blk_id < i, _dot(km_hi, qp) + _dot(km_mid, qp) + _dot(km_lo, qp), NEG)
        sel = blk_id == i
        for _ in range(MOBA_TOPK):
            top = jnp.max(score, axis=0, keepdims=True)
            first = jnp.min(jnp.where(score == top, blk_idf, 1e9), axis=0, keepdims=True)
            hit = blk_idf == first
            sel = sel | (hit & (top > 0.5 * NEG))
            score = jnp.where(hit, 3.0 * NEG, score)
        terms = [t.astype(F32) for t in _split3(jnp.full((1, 1), slopes[h], F32))]
        terms += [t * KV_BLOCK for t in terms]
        aug = jnp.where(blk_id < AUG_POS, jnp.where(sel, 0.0, NEG), 0.0)
        for k, term in enumerate(terms):
            aug = jnp.where(blk_id == AUG_POS + k, term, aug)
        qaug_sc[h] = _stack_q(h, qt_ref[h * HEAD_DIM:(h + 1) * HEAD_DIM, :], aug.astype(BF16))

    own_win = i // WIN_BLOCKS
    firsts = [0]
    for g in groups[1:]:
        count = jnp.zeros((1, KV_BLOCK), F32)
        reach = {}
        for h in g:
            q2 = jnp.max(jnp.sum(jnp.square(qt_ref[h * HEAD_DIM:(h + 1) * HEAD_DIM, :].astype(F32)), axis=0,
                                 keepdims=True), axis=1, keepdims=True)
            reach[h] = (2.0 * BOUND_SLACK) * jnp.sqrt(q2 * knorm_sc[h:h + 1, :]) + UNDERFLOW
        for w in range(nb // WIN_BLOCKS):
            gap = jnp.full((1, KV_BLOCK), own_win - 1 - w, jnp.int32).astype(F32) * KEY_WIN
            matters = functools.reduce(jnp.maximum, [jnp.where(gap * slopes[h] < reach[h], 1.0, 0.0) for h in g])
            count = count + jnp.where(w < own_win, matters, 0.0)
        firsts.append(jnp.maximum(own_win - count[0, 0].astype(jnp.int32), firsts[-1]))

    _flash_groups(pipes, firsts, first_sc, i, step, nb, o_ref, acc_sc)


def _moba(qt, k, vt, batch, seq):
    nb = seq // KV_BLOCK
    return pl.pallas_call(
        functools.partial(_moba_kernel, nb=nb),
        out_shape=jax.ShapeDtypeStruct((batch * seq, ATT_W), BF16),
        grid=(batch, nb + 1),
        in_specs=[pl.BlockSpec((ATT_W, KV_BLOCK), lambda b, i: (0, b * nb + jnp.minimum(i, nb - 1))),
                  pl.BlockSpec((seq, ATT_W), lambda b, i: (b, 0)),
                  pl.BlockSpec((ATT_W, seq), lambda b, i: (0, b))],
        out_specs=pl.BlockSpec((KV_BLOCK, ATT_W), lambda b, i: (b * nb + jnp.maximum(i - 1, 0), 0)),
        scratch_shapes=[pltpu.VMEM((seq, ATT_HEADS * PAIR_W), BF16),
                        pltpu.VMEM((ATT_HEADS, HEAD_DIM, PAIR_W), F32),
                        pltpu.VMEM((ATT_HEADS, KV_BLOCK), F32),
                        pltpu.SMEM((ATT_HEADS,), jnp.int32)]
                       + _flash_scratch(),
        compiler_params=pltpu.CompilerParams(dimension_semantics=("arbitrary", "arbitrary"),
                                             vmem_limit_bytes=VMEM_LIMIT),
        name="moba",
    )(qt, k, vt)


def _fox_kernel(qt_ref, k_ref, vt_ref, sm_ref, fb_ref, o_ref, kaug_sc, cstart_sc, cend_sc, knorm_sc, first_sc,
                qaug_sc, s0_sc, s_sc, p_sc, acc_sc, m_sc, alpha_sc, *, nb):
    step = pl.program_id(1)
    i = jnp.minimum(step, nb - 1)

    def shift_of(h, win, tile):
        return -cstart_sc[win][h:h + 1, :]

    groups = ((0,), (1,), (2, 3))
    scratch = (kaug_sc, qaug_sc, vt_ref, shift_of, s0_sc, s_sc, p_sc, acc_sc, m_sc, alpha_sc)
    pipes = [_Flash(*scratch, heads=g) for g in groups]

    @pl.when(step == 0)
    def _():
        pipes[0].reset()
        for g in range(len(groups)):
            first_sc[g] = 0
        r = lax.broadcasted_iota(jnp.int32, (KV_BLOCK, KV_BLOCK), 0)
        c = lax.broadcasted_iota(jnp.int32, (KV_BLOCK, KV_BLOCK), 1)
        tril = (c <= r).astype(BF16)
        lane = lax.broadcasted_iota(jnp.int32, (KV_BLOCK, PAIR_W), 1)
        term = (lane % HEAD_DIM - AUG_POS) % 3

        def head_rows(v):
            return jnp.concatenate(
                [jnp.broadcast_to(v[0:1, _f_lane(h):_f_lane(h) + 1], (1, KV_BLOCK)) for h in range(ATT_HEADS)], axis=0)

        def scan(n, carry):
            c0 = pl.multiple_of(n * KV_BLOCK, KV_BLOCK)
            first = jnp.full((8, SMALL_W), n % WIN_BLOCKS, jnp.int32) == 0
            blk_start, win_start = carry[0], jnp.where(first, carry[0], carry[1])
            logf = -LOG2E * _softplus(-(sm_ref[pl.ds(c0, KV_BLOCK), :] + fb_ref[...]))
            hi, mid, lo = _split3(logf)
            cs = _dot(tril, hi) + _dot(tril, mid) + _dot(tril, lo)
            in_win = cs + (blk_start - win_start)[0:1, :]
            t_hi, t_mid, t_lo = (t.astype(F32) for t in _split3(in_win))
            terms = jnp.where(term == 0, t_hi, jnp.where(term == 1, t_mid, t_lo))
            blk_end = blk_start + jnp.broadcast_to(cs[KV_BLOCK - 1:KV_BLOCK, :], (8, SMALL_W))
            cstart_sc[n // WIN_BLOCKS] = head_rows(win_start)
            cend_sc[n // WIN_BLOCKS] = head_rows(blk_end)
            knorm = list(carry[2:])
            for h in range(ATT_HEADS):
                j = h // 2
                own, _ = _own_lanes(h, (KV_BLOCK, PAIR_W))
                mine = (lane >= _f_lane(h)) & (lane < _f_lane(h) + 3)
                kp = k_ref[pl.ds(c0, KV_BLOCK), j * PAIR_W:(j + 1) * PAIR_W]
                kaug_sc[pl.ds(c0, KV_BLOCK), h * PAIR_W:(h + 1) * PAIR_W] = jnp.where(
                    own, kp, jnp.where(mine, terms, 0.0).astype(BF16))
                kf = jnp.where(own, kp.astype(F32), 0.0)
                sq = jnp.max(jnp.sum(kf * kf, axis=1, keepdims=True), axis=0, keepdims=True)
                knorm[h] = jnp.maximum(knorm[h], jnp.broadcast_to(sq, (8, SMALL_W)))
            return (blk_end, win_start, *knorm)

        zero = jnp.broadcast_to(fb_ref[...] * 0.0, (8, SMALL_W))
        done = lax.fori_loop(0, nb, scan, (zero,) * (2 + ATT_HEADS))
        knorm_sc[...] = jnp.concatenate(
            [jnp.concatenate([v[0:1, :]] * (KV_BLOCK // SMALL_W), axis=1) for v in done[2:]], axis=0)

    aug_id = lax.broadcasted_iota(jnp.int32, (HEAD_DIM, KV_BLOCK), 0)
    for h in range(ATT_HEADS):
        slot = _f_lane(h) % HEAD_DIM
        aug = jnp.where((aug_id >= slot) & (aug_id < slot + 3), -1.0, 0.0).astype(BF16)
        qaug_sc[h] = _stack_q(h, qt_ref[h * HEAD_DIM:(h + 1) * HEAD_DIM, :], aug)

    own_win = i // WIN_BLOCKS
    q2 = [jnp.max(jnp.sum(jnp.square(qt_ref[h * HEAD_DIM:(h + 1) * HEAD_DIM, :].astype(F32)), axis=0, keepdims=True),
                  axis=1, keepdims=True) for h in range(ATT_HEADS)]
    bound = jnp.sqrt(jnp.concatenate(q2, axis=0) * knorm_sc[...])
    reach = (2.0 * BOUND_SLACK) * bound + UNDERFLOW
    c_own = cstart_sc[own_win]
    counts = [jnp.zeros((1, KV_BLOCK), F32) for _ in groups]
    for w in range(nb // WIN_BLOCKS):
        matters = jnp.where(w < own_win, jnp.where(cend_sc[w] - c_own < reach, 1.0, 0.0), 0.0)
        for g, heads in enumerate(groups):
            counts[g] = counts[g] + functools.reduce(jnp.maximum, [matters[h:h + 1, :] for h in heads])
    firsts = []
    for g in range(len(groups)):
        widest = functools.reduce(jnp.maximum, counts[g:])
        firsts.append(own_win - widest[0, 0].astype(jnp.int32))

    _flash_groups(pipes, firsts, first_sc, i, step, nb, o_ref, acc_sc)


def _fox(qt, k, vt, sm, W, layer, batch, seq):
    nb = seq // KV_BLOCK
    return pl.pallas_call(
        functools.partial(_fox_kernel, nb=nb),
        out_shape=jax.ShapeDtypeStruct((batch * seq, ATT_W), BF16),
        grid=(batch, nb + 1),
        in_specs=[pl.BlockSpec((ATT_W, KV_BLOCK), lambda b, i: (1, b * nb + jnp.minimum(i, nb - 1))),
                  pl.BlockSpec((seq, ATT_W), lambda b, i: (b, 1)),
                  pl.BlockSpec((ATT_W, seq), lambda b, i: (1, b)),
                  pl.BlockSpec((seq, SMALL_W), lambda b, i: (b, 0)),
                  pl.BlockSpec((None, 1, SMALL_W), lambda b, i: (layer, 0, 0))],
        out_specs=pl.BlockSpec((KV_BLOCK, ATT_W), lambda b, i: (b * nb + jnp.maximum(i - 1, 0), 0)),
        scratch_shapes=[pltpu.VMEM((seq, ATT_HEADS * PAIR_W), BF16),
                        pltpu.VMEM((nb // WIN_BLOCKS, ATT_HEADS, KV_BLOCK), F32),
                        pltpu.VMEM((nb // WIN_BLOCKS, ATT_HEADS, KV_BLOCK), F32),
                        pltpu.VMEM((ATT_HEADS, KV_BLOCK), F32),
                        pltpu.SMEM((ATT_HEADS,), jnp.int32)]
                       + _flash_scratch(),
        compiler_params=pltpu.CompilerParams(dimension_semantics=("arbitrary", "arbitrary"),
                                             vmem_limit_bytes=VMEM_LIMIT),
        name="fox",
    )(qt, k, vt, sm, W["fox_fb"])


def _ssd_kernel(xbc_ref, z_ref, sm_ref, cw_ref, cb_ref, dtb_ref, alog_ref, dskip_ref, nw_ref, o_ref,
                ext_sc, state_sc, y_sc):
    c = pl.program_id(1)
    L = SSD_CHUNK
    tail = 8

    @pl.when(c == 0)
    def _():
        ext_sc[0:tail, :] = jnp.zeros((tail, SSD_XBC), F32)
        state_sc[...] = jnp.zeros(state_sc.shape, F32)

    ext_sc[tail:tail + L, :] = xbc_ref[...]
    u = cb_ref[...]
    for k in range(SSD_CONV):
        back = SSD_CONV - 1 - k
        u = u + cw_ref[k:k + 1, :] * ext_sc[tail - back:tail - back + L, :]
    ext_sc[0:tail, :] = ext_sc[L:L + tail, :]
    xc = _silu(u)

    dt = _softplus(sm_ref[...] + dtb_ref[...])
    a = dt * (-jnp.exp(alog_ref[...]))
    r = lax.broadcasted_iota(jnp.int32, (L, L), 0)
    cidx = lax.broadcasted_iota(jnp.int32, (L, L), 1)
    causal = cidx <= r
    tril = causal.astype(BF16)
    a_hi, a_mid, a_lo = _split3(a)
    acs = (_dot(tril, a_hi) + _dot(tril, a_mid) + _dot(tril, a_lo)) * LOG2E
    acs_t = acs.T

    widen_m = (lax.broadcasted_iota(jnp.int32, (SMALL_W, SSD_INNER), 0)
               == lax.broadcasted_iota(jnp.int32, (SMALL_W, SSD_INNER), 1) // SSD_P).astype(BF16)

    def widen(v):
        hi, mid, lo = _split3(v)
        return _dot(hi, widen_m) + _dot(mid, widen_m) + _dot(lo, widen_m)

    total = acs[L - 1:L, :]
    xs = xc[:, 0:SSD_INNER]
    xh = xs * widen(dt)
    x_end = (xh * widen(jnp.exp2(total - acs))).astype(BF16)
    from_start = widen(jnp.exp2(acs))
    keep = widen(jnp.broadcast_to(jnp.exp2(total), (8, SMALL_W)))[0:1, :]
    lane = lax.broadcasted_iota(jnp.int32, (L, PAIR_W), 1)
    heads_per_group = SSD_HEADS // SSD_GROUPS
    gw = heads_per_group * SSD_P

    for g in range(SSD_GROUPS):
        bg = xc[:, SSD_INNER + g * SSD_N:SSD_INNER + (g + 1) * SSD_N].astype(BF16)
        cg = xc[:, SSD_INNER + (SSD_GROUPS + g) * SSD_N:SSD_INNER + (SSD_GROUPS + g + 1) * SSD_N].astype(BF16)
        bgt = xc[:, SSD_INNER + g * SSD_N:SSD_INNER + (g + 1) * SSD_N].T.astype(BF16)
        cb = jnp.where(causal, _dot_nt(cg, bg), 0.0)
        gcols = slice(g * gw, (g + 1) * gw)
        st = state_sc[:, gcols]
        y_off = from_start[:, gcols] * _dot(cg, st.astype(BF16))
        state_sc[:, gcols] = keep[:, gcols] * st + _dot(bgt, x_end[:, gcols])
        for pair in range(heads_per_group // 2):
            h0 = g * heads_per_group + 2 * pair
            pcols = slice(h0 * SSD_P, (h0 + 2) * SSD_P)
            xp = xh[:, pcols].astype(BF16)
            y_pair = []
            for h in (h0, h0 + 1):
                decay = jnp.exp2(jnp.minimum(acs[:, h:h + 1] - acs_t[h:h + 1, :], 0.0))
                y_pair.append(_dot((cb * decay).astype(BF16), xp))
            y_diag = jnp.where(lane < SSD_P, y_pair[0], y_pair[1])
            y_sc[:, pcols] = (y_diag + y_off[:, pair * PAIR_W:(pair + 1) * PAIR_W]
                              + dskip_ref[:, pcols] * xs[:, pcols])

    y = y_sc[...] * _silu(z_ref[...])
    o_ref[...] = _rms(y, nw_ref[...]).astype(o_ref.dtype)


def _ssd(xbc, z, sm, W, layer, batch, seq):
    nc = seq // SSD_CHUNK
    L = SSD_CHUNK
    row = lambda b, c: (b * nc + c, 0)
    wspec = lambda a: pl.BlockSpec((None,) + a.shape[1:], lambda b, c: (layer, 0, 0))
    ws = tuple(W[n] for n in ("conv_w", "conv_b", "dt_bias", "a_log", "d_skip", "ssd_norm_w"))
    return pl.pallas_call(
        _ssd_kernel,
        out_shape=jax.ShapeDtypeStruct((batch * seq, SSD_INNER), BF16),
        grid=(batch, nc),
        in_specs=[pl.BlockSpec((L, SSD_XBC), row), pl.BlockSpec((L, SSD_INNER), row),
                  pl.BlockSpec((L, SMALL_W), row)] + [wspec(a) for a in ws],
        out_specs=pl.BlockSpec((L, SSD_INNER), row),
        scratch_shapes=[pltpu.VMEM((L + 8, SSD_XBC), F32),
                        pltpu.VMEM((SSD_N, SSD_INNER), F32),
                        pltpu.VMEM((L, SSD_INNER), F32)],
        compiler_params=pltpu.CompilerParams(dimension_semantics=("arbitrary", "arbitrary"),
                                             vmem_limit_bytes=VMEM_LIMIT),
        name="ssd",
    )(xbc, z, sm, *ws)


def _post_kernel(x_ref, a_ref, b_ref, c_ref, p_ref, wo_ref, gm_ref, w1_ref, w2_ref, gp_ref, wg_ref, wp_ref,
                 gf_ref, o_ref, acc_sc, h_sc, *, final):
    j = pl.program_id(1)

    @pl.when(j == 0)
    def _():
        mixed = jnp.concatenate([a_ref[...], b_ref[...], c_ref[...]], axis=1)
        x1 = x_ref[...] + _dot(mixed, wo_ref[...])
        acc_sc[...] = x1
        h_sc[...] = _rms(x1, gm_ref[...]).astype(BF16)

    u = jnp.maximum(_dot(h_sc[...], w1_ref[...]), 0.0)
    acc_sc[...] += _dot((u * u).astype(BF16), w2_ref[...])

    @pl.when(j == pl.num_programs(1) - 1)
    def _():
        x2 = acc_sc[...]
        gate = 1.0 / (1.0 + jnp.exp(-_dot(_rms(x2, gp_ref[...]).astype(BF16), wg_ref[...])))
        x3 = x2 + gate * _dot(p_ref[...].astype(BF16), wp_ref[...])
        o_ref[...] = _rms(x3, gf_ref[...]) if final else x3


def _post(x, a, b, c, W, layer, final):
    t, d = x.shape
    tm = POST_ROW_TILE
    d_ff = W["w_ff1"].shape[2]
    row = lambda i, j: (i, 0)
    wspec = lambda n: pl.BlockSpec((None,) + W[n].shape[1:], lambda i, j: (layer, 0, 0))
    return pl.pallas_call(
        functools.partial(_post_kernel, final=final),
        out_shape=jax.ShapeDtypeStruct((t, d), F32),
        grid=(t // tm, d_ff // FF_CHUNK),
        in_specs=[pl.BlockSpec((tm, d), row), pl.BlockSpec((tm, ATT_W), row), pl.BlockSpec((tm, SSD_INNER), row),
                  pl.BlockSpec((tm, ATT_W), row), pl.BlockSpec((None, tm, PLE_DIM), lambda i, j: (layer, i, 0)),
                  wspec("w_out"), wspec("g_mlp"),
                  pl.BlockSpec((None, d, FF_CHUNK), lambda i, j: (layer, 0, j)),
                  pl.BlockSpec((None, FF_CHUNK, d), lambda i, j: (layer, j, 0)),
                  wspec("g_ple"), wspec("w_ple_gate"), wspec("w_ple_proj"),
                  pl.BlockSpec(W["g_final"].shape, lambda i, j: (0, 0))],
        out_specs=pl.BlockSpec((tm, d), row),
        scratch_shapes=[pltpu.VMEM((tm, d), F32), pltpu.VMEM((tm, d), BF16)],
        compiler_params=pltpu.CompilerParams(dimension_semantics=("arbitrary", "arbitrary"),
                                             vmem_limit_bytes=VMEM_LIMIT),
        name="post",
    )(x, a, b, c, W["p"], W["w_out"], W["g_mlp"], W["w_ff1"], W["w_ff2"], W["g_ple"], W["w_ple_gate"],
      W["w_ple_proj"], W["g_final"])


def _prepare(p, g_mix, w_in, conv_w, conv_b, dt_bias, a_log, d_skip, ssd_norm_w, fox_f_bias, w_out,
             g_mlp, w_ff1, w_ff2, g_ple, w_ple_gate, w_ple_proj, g_final):
    depth, d, _ = w_in.shape
    o = 0
    mq, mk, mv = (w_in[:, :, o + k * ATT_W:o + (k + 1) * ATT_W] for k in range(3)); o += 3 * ATT_W
    wz = w_in[:, :, o:o + SSD_INNER]; o += SSD_INNER
    wxbc = w_in[:, :, o:o + SSD_XBC]; o += SSD_XBC
    wdt = w_in[:, :, o:o + SSD_HEADS]; o += SSD_HEADS
    fq, fk, fv = (w_in[:, :, o + k * ATT_W:o + (k + 1) * ATT_W] for k in range(3)); o += 3 * ATT_W
    wf = w_in[:, :, o:o + ATT_HEADS]
    order = jnp.argsort(-fox_f_bias, axis=1)

    def by_head(a, axis):
        shape = a.shape
        a = a.reshape(shape[:axis] + (ATT_HEADS, shape[axis] // ATT_HEADS) + shape[axis + 1:])
        idx = order.reshape((depth,) + (1,) * (axis - 1) + (ATT_HEADS,) + (1,) * (a.ndim - axis - 1))
        return jnp.take_along_axis(a, idx, axis=axis).reshape(shape)

    fq, fk, fv, wf = by_head(fq, 2), by_head(fk, 2), by_head(fv, 2), by_head(wf, 2)
    fox_f_bias = jnp.take_along_axis(fox_f_bias, order, axis=1)
    w_out = jnp.concatenate([w_out[:, :ATT_W + SSD_INNER], by_head(w_out[:, ATT_W + SSD_INNER:], 1)], axis=1)
    src = [SSD_HEADS + ATT_HEADS] * SMALL_W
    for j in range(SSD_HEADS):
        src[j] = j
    for h in range(ATT_HEADS):
        for k in range(3):
            src[_f_lane(h) + k] = SSD_HEADS + h
    src = jnp.asarray(src, jnp.int32)
    small = jnp.take(jnp.concatenate([wdt, wf, jnp.zeros_like(wf[:, :, :1])], axis=2), src, axis=2)
    fb = jnp.take(jnp.concatenate([jnp.zeros((depth, SSD_HEADS), F32), fox_f_bias, jnp.zeros((depth, 1), F32)],
                                  axis=1), src, axis=1)
    w_all = jnp.concatenate([mq, fq, mv, fv, mk, fk, wz, wxbc, small], axis=2).astype(BF16)
    assert w_all.shape[2] == IN_COLS["sm"][0] + SMALL_W
    row = lambda a: a[:, None, :]
    pad = lambda a: jnp.pad(a, ((0, 0), (0, SMALL_W - a.shape[1])))
    return {
        "w_in": w_all, "g_mix": row(g_mix),
        "conv_w": conv_w, "conv_b": row(conv_b), "dt_bias": row(pad(dt_bias)), "a_log": row(pad(a_log)),
        "d_skip": row(jnp.repeat(d_skip, SSD_P, axis=1)), "ssd_norm_w": row(ssd_norm_w),
        "fox_fb": row(fb),
        "p": p.reshape(depth, -1, p.shape[-1]),
        "w_out": w_out.astype(BF16), "g_mlp": row(g_mlp), "w_ff1": w_ff1.astype(BF16), "w_ff2": w_ff2.astype(BF16),
        "g_ple": row(g_ple), "w_ple_gate": w_ple_gate.astype(BF16), "w_ple_proj": w_ple_proj.astype(BF16),
        "g_final": g_final[None, :],
    }


def kernel(x, p, g_mix, w_in, conv_w, conv_b, dt_bias, a_log, d_skip, ssd_norm_w, fox_f_bias, w_out, g_mlp, w_ff1, w_ff2, g_ple, w_ple_gate, w_ple_proj, g_final):
    batch, seq, d = x.shape
    depth = w_in.shape[0]
    assert seq % KEY_WIN == 0 and seq % SSD_CHUNK == 0
    assert (batch * seq) % ROW_TILE == 0 and (batch * seq) % POST_ROW_TILE == 0 and w_ff1.shape[2] % FF_CHUNK == 0
    assert seq // KV_BLOCK <= AUG_POS, "one-hot block ids use the augmentation slots below AUG_POS"
    xt = x.reshape(batch * seq, d)
    W = _prepare(p, g_mix, w_in, conv_w, conv_b, dt_bias, a_log, d_skip, ssd_norm_w, fox_f_bias, w_out,
                 g_mlp, w_ff1, w_ff2, g_ple, w_ple_gate, w_ple_proj, g_final)
    for i in range(depth):
        qt, k, vt, z, xbc, sm = _inproj(xt, W, i)
        out_a = _moba(qt, k, vt, batch, seq)
        out_b = _ssd(xbc, z, sm, W, i, batch, seq)
        out_c = _fox(qt, k, vt, sm, W, i, batch, seq)
        xt = _post(xt, out_a, out_b, out_c, W, i, final=(i == depth - 1))
    return xt.reshape(batch, seq, d)
```

```python
import functools

import jax
import jax.numpy as jnp
from jax import lax
from jax.experimental import pallas as pl
from jax.experimental.pallas import tpu as pltpu

F32 = jnp.float32
BF16 = jnp.bfloat16

HEAD_DIM = 64
ATT_HEADS = 4
ATT_W = ATT_HEADS * HEAD_DIM
PAIR_W = 2 * HEAD_DIM
KV_BLOCK = 256
WIN_BLOCKS = 2
KEY_WIN = WIN_BLOCKS * KV_BLOCK
SOFTMAX_ROWS = 64
MOBA_TOPK = 3
AUG_POS = 32
ACC_ROWS = HEAD_DIM + 16
LOG2E = 1.4426950408889634
UNDERFLOW = 152.0
BOUND_SLACK = 1.01
SSD_HEADS = 8
SSD_P = 64
SSD_INNER = SSD_HEADS * SSD_P
SSD_GROUPS = 2
SSD_N = 128
SSD_CONV = 4
SSD_CHUNK = 256
SSD_XBC = SSD_INNER + 2 * SSD_GROUPS * SSD_N
SMALL_W = 128
IN_COLS = {"qv": (0, 4 * ATT_W), "k": (4 * ATT_W, 2 * ATT_W), "z": (6 * ATT_W, SSD_INNER),
           "xbc": (6 * ATT_W + SSD_INNER, SSD_XBC), "sm": (6 * ATT_W + SSD_INNER + SSD_XBC, SMALL_W)}
PLE_DIM = 256
RMS_EPS = 1e-6
NEG = -1e30
VMEM_LIMIT = 56 * 1024 * 1024

ROW_TILE = 1024
POST_ROW_TILE = 1024
FF_CHUNK = 1024


def _rms(x, g):
    return x * lax.rsqrt(jnp.mean(x * x, axis=-1, keepdims=True) + RMS_EPS) * g


def _dot(a, b):
    return jnp.dot(a, b, preferred_element_type=F32)


def _dot_nt(a, b):
    return lax.dot_general(a, b, (((1,), (1,)), ((), ())), preferred_element_type=F32)


def _split3(a):
    hi = a.astype(BF16)
    r1 = a - hi.astype(F32)
    mid = r1.astype(BF16)
    lo = (r1 - mid.astype(F32)).astype(BF16)
    return hi, mid, lo


def _softplus(x):
    return jnp.maximum(x, 0.0) + jnp.log(1.0 + jnp.exp(-jnp.abs(x)))


def _silu(x):
    return x * (1.0 / (1.0 + jnp.exp(-x)))


def _inproj_kernel(x_ref, g_ref, wqv_ref, wk_ref, wz_ref, wxbc_ref, wsm_ref,
                   qt_ref, k_ref, vt_ref, z_ref, xbc_ref, sm_ref, wqvt_sc):
    @pl.when(pl.program_id(0) == 0)
    def _():
        wqvt_sc[...] = wqv_ref[...].astype(F32).T.astype(BF16)

    h = _rms(x_ref[...], g_ref[...]).astype(BF16)
    qt_ref[...] = (_dot_nt(wqvt_sc[0:2 * ATT_W, :], h) * (HEAD_DIM ** -0.5 * LOG2E)).astype(BF16)
    k_ref[...] = _dot(h, wk_ref[...]).astype(BF16)
    vt_ref[...] = _dot_nt(wqvt_sc[2 * ATT_W:, :], h).astype(BF16)
    z_ref[...] = _dot(h, wz_ref[...])
    xbc_ref[...] = _dot(h, wxbc_ref[...])
    sm_ref[...] = _dot(h, wsm_ref[...])


def _inproj(x, W, layer):
    t, d = x.shape
    tm = ROW_TILE
    row = lambda i: (i, 0)
    col = lambda i: (0, i)

    def wspec(name):
        start, width = IN_COLS[name]
        assert start % width == 0
        return pl.BlockSpec((None, d, width), lambda i: (layer, 0, start // width))
    outs = (
        jax.ShapeDtypeStruct((2 * ATT_W, t), BF16),
        jax.ShapeDtypeStruct((t, 2 * ATT_W), BF16),
        jax.ShapeDtypeStruct((2 * ATT_W, t), BF16),
        jax.ShapeDtypeStruct((t, SSD_INNER), F32),
        jax.ShapeDtypeStruct((t, SSD_XBC), F32),
        jax.ShapeDtypeStruct((t, SMALL_W), F32),
    )
    names = ("qv", "k", "z", "xbc", "sm")
    return pl.pallas_call(
        _inproj_kernel,
        out_shape=outs,
        grid=(t // tm,),
        in_specs=[pl.BlockSpec((tm, d), row), pl.BlockSpec((None, 1, d), lambda i: (layer, 0, 0))]
                 + [wspec(n) for n in names],
        out_specs=(pl.BlockSpec((2 * ATT_W, tm), col), pl.BlockSpec((tm, 2 * ATT_W), row),
                   pl.BlockSpec((2 * ATT_W, tm), col), pl.BlockSpec((tm, SSD_INNER), row),
                   pl.BlockSpec((tm, SSD_XBC), row), pl.BlockSpec((tm, SMALL_W), row)),
        scratch_shapes=[pltpu.VMEM((4 * ATT_W, d), BF16)],
        compiler_params=pltpu.CompilerParams(dimension_semantics=("arbitrary",), vmem_limit_bytes=VMEM_LIMIT),
        name="inproj",
    )(x, W["g_mix"], *([W["w_in"]] * len(names)))


class _Flash:
    def __init__(self, kaug_sc, qaug_sc, vt_ref, shift_of, s0_sc, s_sc, p_sc, acc_sc, m_sc, alpha_sc,
                 heads=range(ATT_HEADS)):
        self.kaug_sc, self.qaug_sc, self.vt_ref, self.shift_of = kaug_sc, qaug_sc, vt_ref, shift_of
        self.s0_sc, self.s_sc, self.p_sc, self.acc_sc = s0_sc, s_sc, p_sc, acc_sc
        self.m_sc, self.alpha_sc = m_sc, alpha_sc
        self.heads = tuple(heads)
        self.ones = jnp.ones((ACC_ROWS - HEAD_DIM, KEY_WIN), BF16)

    def reset(self):
        self.s_sc[...] = jnp.zeros(self.s_sc.shape, F32)
        self.p_sc[...] = jnp.zeros(self.p_sc.shape, BF16)
        self.acc_sc[...] = jnp.ones(self.acc_sc.shape, F32)
        self.m_sc[...] = jnp.zeros(self.m_sc.shape, F32)
        self.alpha_sc[...] = jnp.ones(self.alpha_sc.shape, F32)

    def scores(self, h, win):
        c0 = pl.multiple_of(win * KEY_WIN, KEY_WIN)
        return _dot(self.kaug_sc[pl.ds(c0, KEY_WIN), h * PAIR_W:(h + 1) * PAIR_W], self.qaug_sc[h])

    @staticmethod
    def plan(tile, first_win, live=True):
        own_win = tile // WIN_BLOCKS
        n_past = jnp.where(live, own_win - first_win, 0)
        dead = jnp.full((1, KV_BLOCK), n_past, jnp.int32) == 0
        return own_win, first_win, jnp.maximum(n_past, 1), jnp.where(dead, 1.0, 0.0)

    @staticmethod
    def tile_win(t, plan):
        own_win, first_win, _, _ = plan
        return jnp.where(t < 1, own_win, first_win + t - 1)

    def pv_stage(self, t, plan, alpha):
        c0 = pl.multiple_of(self.tile_win(t, plan) * KEY_WIN, KEY_WIN)
        for h in self.heads:
            vt_h = jnp.concatenate([self.vt_ref[h * HEAD_DIM:(h + 1) * HEAD_DIM, pl.ds(c0, KEY_WIN)], self.ones],
                                   axis=0)
            self.acc_sc[h] = alpha[h] * self.acc_sc[h] + _dot(vt_h, self.p_sc[h])

    def softmax_stage(self, t, tile, plan, m, src, refill):
        m_out, alpha = {}, {}
        for h in self.heads:
            chunks = [pl.ds(r, SOFTMAX_ROWS) for r in range(0, KEY_WIN, SOFTMAX_ROWS)]
            top = src[h, chunks[0], :]
            for rows in chunks[1:]:
                top = jnp.maximum(top, src[h, rows, :])
            shift = self.shift_of(h, self.tile_win(t, plan), tile)
            if src is self.s_sc:
                shift = shift + plan[3] * NEG
            m_new = jnp.maximum(m[h], jnp.max(top, axis=0, keepdims=True) + shift)
            base = m_new - shift
            for rows in chunks:
                self.p_sc[h, rows, :] = jnp.exp2(src[h, rows, :] - base).astype(BF16)
            if refill:
                self.s_sc[h] = self.scores(h, self.tile_win(t + 1, plan))
            m_out[h] = m_new
            alpha[h] = jnp.exp2(m[h] - m_new)
        return m_out, alpha

    def own_scores(self, tile):
        own_win = tile // WIN_BLOCKS
        key = lax.broadcasted_iota(jnp.int32, (KEY_WIN, KV_BLOCK), 0) + (own_win * KEY_WIN - tile * KV_BLOCK)
        causal = key <= lax.broadcasted_iota(jnp.int32, (KEY_WIN, KV_BLOCK), 1)
        for h in self.heads:
            self.s0_sc[h] = jnp.where(causal, self.scores(h, own_win), NEG)

    def finish(self, tile, first_win):
        plan = self.plan(tile, first_win)
        n = plan[2]
        m = {h: self.m_sc[h] for h in self.heads}
        alpha = {h: self.alpha_sc[h] for h in self.heads}
        self.pv_stage(n - 1, plan, alpha)
        m, alpha = self.softmax_stage(n, tile, plan, m, self.s_sc, refill=False)
        self.pv_stage(n, plan, alpha)

    def save(self, carry):
        m, alpha = carry
        for h in self.heads:
            self.m_sc[h] = m[h]
            self.alpha_sc[h] = alpha[h]

    def first_softmax(self, tile, plan):
        start = {h: jnp.full((1, KV_BLOCK), NEG, F32) for h in self.heads}
        return self.softmax_stage(0, tile, plan, start, self.s0_sc, refill=True)

    def step(self, t, tile, plan, carry):
        m, alpha = carry
        self.pv_stage(t - 1, plan, alpha)
        return self.softmax_stage(t, tile, plan, m, self.s_sc, refill=True)


def _flash_output(o_ref, acc_sc):
    out_t = jnp.concatenate([acc_sc[h][0:HEAD_DIM, :] / acc_sc[h][HEAD_DIM:HEAD_DIM + 1, :]
                             for h in range(ATT_HEADS)], axis=0)
    o_ref[...] = out_t.T.astype(o_ref.dtype)


def _flash_groups(pipes, firsts, first_sc, tile, step, nb, o_ref, acc_sc):
    prev = jnp.maximum(step - 1, 0)
    for g, pipe in enumerate(pipes):
        pipe.finish(prev, first_sc[g])
    _flash_output(o_ref, acc_sc)

    plans = [pipe.plan(tile, first, step < nb) for pipe, first in zip(pipes, firsts)]
    for pipe in pipes:
        pipe.own_scores(tile)
    for h in range(ATT_HEADS):
        acc_sc[h] = jnp.zeros(acc_sc.shape[1:], F32)
    carries = [pipe.first_softmax(tile, plan) for pipe, plan in zip(pipes, plans)]
    lag = [plans[0][2] - plan[2] for plan in plans]
    for k in range(len(pipes)):
        lo = lag[k] + 1
        hi = lag[k + 1] + 1 if k + 1 < len(pipes) else plans[0][2]

        def phase(t, active, k=k):
            return [pipes[g].step(t - lag[g], tile, plans[g], active[g]) for g in range(k + 1)]

        carries[:k + 1] = lax.fori_loop(lo, hi, phase, carries[:k + 1])
    for g, pipe in enumerate(pipes):
        pipe.save(carries[g])
        first_sc[g] = firsts[g]


def _flash_scratch():
    return [pltpu.VMEM((ATT_HEADS, PAIR_W, KV_BLOCK), BF16),
            pltpu.VMEM((ATT_HEADS, KEY_WIN, KV_BLOCK), F32),
            pltpu.VMEM((ATT_HEADS, KEY_WIN, KV_BLOCK), F32),
            pltpu.VMEM((ATT_HEADS, KEY_WIN, KV_BLOCK), BF16),
            pltpu.VMEM((ATT_HEADS, ACC_ROWS, KV_BLOCK), F32),
            pltpu.VMEM((ATT_HEADS, 1, KV_BLOCK), F32),
            pltpu.VMEM((ATT_HEADS, 1, KV_BLOCK), F32)]


def _f_lane(h):
    return (HEAD_DIM if h % 2 == 0 else 0) + AUG_POS + 3 * (h // 2)


def _own_lanes(h, shape):
    lane = lax.broadcasted_iota(jnp.int32, shape, 1)
    if h % 2 == 0:
        return lane < HEAD_DIM, lane - HEAD_DIM
    return lane >= HEAD_DIM, lane


def _stack_q(h, q_h, aug_rows):
    return jnp.concatenate([q_h, aug_rows] if h % 2 == 0 else [aug_rows, q_h], axis=0)


def _moba_kernel(qt_ref, k_ref, vt_ref, o_ref, kaug_sc, km_sc, knorm_sc, first_sc, qaug_sc, s0_sc, s_sc, p_sc, acc_sc,
                 m_sc, alpha_sc, *, nb):
    step = pl.program_id(1)
    i = jnp.minimum(step, nb - 1)
    slopes = [LOG2E * 2.0 ** (-8.0 * (h + 1) / ATT_HEADS) for h in range(ATT_HEADS)]

    def shift_of(h, win, tile):
        return (slopes[h] * KEY_WIN) * jnp.full((1, KV_BLOCK), win - tile // WIN_BLOCKS, jnp.int32).astype(F32)

    groups = ((2, 3), (1,), (0,))
    near_heads = tuple(h for g in groups[1:] for h in g)
    scratch = (kaug_sc, qaug_sc, vt_ref, shift_of, s0_sc, s_sc, p_sc, acc_sc, m_sc, alpha_sc)
    pipes = [_Flash(*scratch, heads=g) for g in groups]

    @pl.when(step == 0)
    def _():
        pipes[0].reset()
        for g in range(len(groups)):
            first_sc[g] = 0
        knorm_sc[...] = jnp.zeros(knorm_sc.shape, F32)
        km_sc[...] = jnp.zeros(km_sc.shape, F32)
        offs = lax.broadcasted_iota(jnp.int32, (KV_BLOCK, PAIR_W), 0).astype(F32)
        km_row = lax.broadcasted_iota(jnp.int32, (HEAD_DIM, PAIR_W), 0)

        def build(n, carry):
            c0 = pl.multiple_of(n * KV_BLOCK, KV_BLOCK)
            in_win = jnp.full((KV_BLOCK, PAIR_W), n % WIN_BLOCKS, jnp.int32).astype(F32)
            for j in range(ATT_HEADS // 2):
                kp = k_ref[pl.ds(c0, KV_BLOCK), j * PAIR_W:(j + 1) * PAIR_W]
                mean = jnp.sum(kp.astype(F32), axis=0, keepdims=True) * (1.0 / KV_BLOCK)
                for h in (2 * j, 2 * j + 1):
                    own, slot = _own_lanes(h, (KV_BLOCK, PAIR_W))
                    aug = jnp.where(slot == n, 1.0,
                                    jnp.where((slot >= AUG_POS) & (slot < AUG_POS + 3), offs,
                                              jnp.where((slot >= AUG_POS + 3) & (slot < AUG_POS + 6), in_win, 0.0)))
                    kaug_sc[pl.ds(c0, KV_BLOCK), h * PAIR_W:(h + 1) * PAIR_W] = jnp.where(own, kp, aug.astype(BF16))
                    own_km, _ = _own_lanes(h, (HEAD_DIM, PAIR_W))
                    km_sc[h] = jnp.where(km_row == n, jnp.where(own_km, mean, 0.0), km_sc[h])
                    if h in near_heads:
                        kf = jnp.where(own, kp.astype(F32), 0.0)
                        sq = jnp.max(jnp.sum(kf * kf, axis=1, keepdims=True), axis=0, keepdims=True)
                        knorm_sc[h:h + 1, :] = jnp.maximum(knorm_sc[h:h + 1, :], sq)
            return carry

        lax.fori_loop(0, nb, build, 0)

    blk_id = lax.broadcasted_iota(jnp.int32, (HEAD_DIM, KV_BLOCK), 0)
    blk_idf = blk_id.astype(F32)
    for h in range(ATT_HEADS):
        j = h // 2
        qp = qt_ref[j * PAIR_W:(j + 1) * PAIR_W, :]
        km_hi, km_mid, km_lo = _split3(km_sc[h])
        score = jnp.where(blk_id < i, _dot(km_hi, qp) + _dot(km_mid, qp) + _dot(km_lo, qp), NEG)
        sel = blk_id == i
        for _ in range(MOBA_TOPK):
            top = jnp.max(score, axis=0, keepdims=True)
            first = jnp.min(jnp.where(score == top, blk_idf, 1e9), axis=0, keepdims=True)
            hit = blk_idf == first
            sel = sel | (hit & (top > 0.5 * NEG))
            score = jnp.where(hit, 3.0 * NEG, score)
        terms = [t.astype(F32) for t in _split3(jnp.full((1, 1), slopes[h], F32))]
        terms += [t * KV_BLOCK for t in terms]
        aug = jnp.where(blk_id < AUG_POS, jnp.where(sel, 0.0, NEG), 0.0)
        for k, term in enumerate(terms):
            aug = jnp.where(blk_id == AUG_POS + k, term, aug)
        qaug_sc[h] = _stack_q(h, qt_ref[h * HEAD_DIM:(h + 1) * HEAD_DIM, :], aug.astype(BF16))

    own_win = i // WIN_BLOCKS
    firsts = [0]
    for g in groups[1:]:
        count = jnp.zeros((1, KV_BLOCK), F32)
        reach = {}
        for h in g:
            q2 = jnp.max(jnp.sum(jnp.square(qt_ref[h * HEAD_DIM:(h + 1) * HEAD_DIM, :].astype(F32)), axis=0,
                                 keepdims=True), axis=1, keepdims=True)
            reach[h] = (2.0 * BOUND_SLACK) * jnp.sqrt(q2 * knorm_sc[h:h + 1, :]) + UNDERFLOW
        for w in range(nb // WIN_BLOCKS):
            gap = jnp.full((1, KV_BLOCK), own_win - 1 - w, jnp.int32).astype(F32) * KEY_WIN
            matters = functools.reduce(jnp.maximum, [jnp.where(gap * slopes[h] < reach[h], 1.0, 0.0) for h in g])
            count = count + jnp.where(w < own_win, matters, 0.0)
        firsts.append(jnp.maximum(own_win - count[0, 0].astype(jnp.int32), firsts[-1]))

    _flash_groups(pipes, firsts, first_sc, i, step, nb, o_ref, acc_sc)


def _moba(qt, k, vt, batch, seq):
    nb = seq // KV_BLOCK
    return pl.pallas_call(
        functools.partial(_moba_kernel, nb=nb),
        out_shape=jax.ShapeDtypeStruct((batch * seq, ATT_W), BF16),
        grid=(batch, nb + 1),
        in_specs=[pl.BlockSpec((ATT_W, KV_BLOCK), lambda b, i: (0, b * nb + jnp.minimum(i, nb - 1))),
                  pl.BlockSpec((seq, ATT_W), lambda b, i: (b, 0)),
                  pl.BlockSpec((ATT_W, seq), lambda b, i: (0, b))],
        out_specs=pl.BlockSpec((KV_BLOCK, ATT_W), lambda b, i: (b * nb + jnp.maximum(i - 1, 0), 0)),
        scratch_shapes=[pltpu.VMEM((seq, ATT_HEADS * PAIR_W), BF16),
                        pltpu.VMEM((ATT_HEADS, HEAD_DIM, PAIR_W), F32),
                        pltpu.VMEM((ATT_HEADS, KV_BLOCK), F32),
                        pltpu.SMEM((ATT_HEADS,), jnp.int32)]
                       + _flash_scratch(),
        compiler_params=pltpu.CompilerParams(dimension_semantics=("arbitrary", "arbitrary"),
                                             vmem_limit_bytes=VMEM_LIMIT),
        name="moba",
    )(qt, k, vt)


def _fox_kernel(qt_ref, k_ref, vt_ref, sm_ref, fb_ref, o_ref, kaug_sc, cstart_sc, cend_sc, knorm_sc, first_sc,
                qaug_sc, s0_sc, s_sc, p_sc, acc_sc, m_sc, alpha_sc, *, nb):
    step = pl.program_id(1)
    i = jnp.minimum(step, nb - 1)

    def shift_of(h, win, tile):
        return -cstart_sc[win][h:h + 1, :]

    groups = ((0,), (1,), (2,), (3,))
    scratch = (kaug_sc, qaug_sc, vt_ref, shift_of, s0_sc, s_sc, p_sc, acc_sc, m_sc, alpha_sc)
    pipes = [_Flash(*scratch, heads=g) for g in groups]

    @pl.when(step == 0)
    def _():
        pipes[0].reset()
        for g in range(len(groups)):
            first_sc[g] = 0
        r = lax.broadcasted_iota(jnp.int32, (KV_BLOCK, KV_BLOCK), 0)
        c = lax.broadcasted_iota(jnp.int32, (KV_BLOCK, KV_BLOCK), 1)
        tril = (c <= r).astype(BF16)
        lane = lax.broadcasted_iota(jnp.int32, (KV_BLOCK, PAIR_W), 1)
        term = (lane % HEAD_DIM - AUG_POS) % 3

        def head_rows(v):
            return jnp.concatenate(
                [jnp.broadcast_to(v[0:1, _f_lane(h):_f_lane(h) + 1], (1, KV_BLOCK)) for h in range(ATT_HEADS)], axis=0)

        def scan(n, carry):
            c0 = pl.multiple_of(n * KV_BLOCK, KV_BLOCK)
            first = jnp.full((8, SMALL_W), n % WIN_BLOCKS, jnp.int32) == 0
            blk_start, win_start = carry[0], jnp.where(first, carry[0], carry[1])
            logf = -LOG2E * _softplus(-(sm_ref[pl.ds(c0, KV_BLOCK), :] + fb_ref[...]))
            hi, mid, lo = _split3(logf)
            cs = _dot(tril, hi) + _dot(tril, mid) + _dot(tril, lo)
            in_win = cs + (blk_start - win_start)[0:1, :]
            t_hi, t_mid, t_lo = (t.astype(F32) for t in _split3(in_win))
            terms = jnp.where(term == 0, t_hi, jnp.where(term == 1, t_mid, t_lo))
            blk_end = blk_start + jnp.broadcast_to(cs[KV_BLOCK - 1:KV_BLOCK, :], (8, SMALL_W))
            cstart_sc[n // WIN_BLOCKS] = head_rows(win_start)
            cend_sc[n // WIN_BLOCKS] = head_rows(blk_end)
            knorm = list(carry[2:])
            for h in range(ATT_HEADS):
                j = h // 2
                own, _ = _own_lanes(h, (KV_BLOCK, PAIR_W))
                mine = (lane >= _f_lane(h)) & (lane < _f_lane(h) + 3)
                kp = k_ref[pl.ds(c0, KV_BLOCK), j * PAIR_W:(j + 1) * PAIR_W]
                kaug_sc[pl.ds(c0, KV_BLOCK), h * PAIR_W:(h + 1) * PAIR_W] = jnp.where(
                    own, kp, jnp.where(mine, terms, 0.0).astype(BF16))
                kf = jnp.where(own, kp.astype(F32), 0.0)
                sq = jnp.max(jnp.sum(kf * kf, axis=1, keepdims=True), axis=0, keepdims=True)
                knorm[h] = jnp.maximum(knorm[h], jnp.broadcast_to(sq, (8, SMALL_W)))
            return (blk_end, win_start, *knorm)

        zero = jnp.broadcast_to(fb_ref[...] * 0.0, (8, SMALL_W))
        done = lax.fori_loop(0, nb, scan, (zero,) * (2 + ATT_HEADS))
        knorm_sc[...] = jnp.concatenate(
            [jnp.concatenate([v[0:1, :]] * (KV_BLOCK // SMALL_W), axis=1) for v in done[2:]], axis=0)

    aug_id = lax.broadcasted_iota(jnp.int32, (HEAD_DIM, KV_BLOCK), 0)
    for h in range(ATT_HEADS):
        slot = _f_lane(h) % HEAD_DIM
        aug = jnp.where((aug_id >= slot) & (aug_id < slot + 3), -1.0, 0.0).astype(BF16)
        qaug_sc[h] = _stack_q(h, qt_ref[h * HEAD_DIM:(h + 1) * HEAD_DIM, :], aug)

    own_win = i // WIN_BLOCKS
    q2 = [jnp.max(jnp.sum(jnp.square(qt_ref[h * HEAD_DIM:(h + 1) * HEAD_DIM, :].astype(F32)), axis=0, keepdims=True),
                  axis=1, keepdims=True) for h in range(ATT_HEADS)]
    bound = jnp.sqrt(jnp.concatenate(q2, axis=0) * knorm_sc[...])
    reach = (2.0 * BOUND_SLACK) * bound + UNDERFLOW
    c_own = cstart_sc[own_win]
    counts = [jnp.zeros((1, KV_BLOCK), F32) for _ in groups]
    for w in range(nb // WIN_BLOCKS):
        matters = jnp.where(w < own_win, jnp.where(cend_sc[w] - c_own < reach, 1.0, 0.0), 0.0)
        for g, heads in enumerate(groups):
            counts[g] = counts[g] + functools.reduce(jnp.maximum, [matters[h:h + 1, :] for h in heads])
    firsts = []
    for g in range(len(groups)):
        widest = functools.reduce(jnp.maximum, counts[g:])
        firsts.append(own_win - widest[0, 0].astype(jnp.int32))

    _flash_groups(pipes, firsts, first_sc, i, step, nb, o_ref, acc_sc)


def _fox(qt, k, vt, sm, W, layer, batch, seq):
    nb = seq // KV_BLOCK
    return pl.pallas_call(
        functools.partial(_fox_kernel, nb=nb),
        out_shape=jax.ShapeDtypeStruct((batch * seq, ATT_W), BF16),
        grid=(batch, nb + 1),
        in_specs=[pl.BlockSpec((ATT_W, KV_BLOCK), lambda b, i: (1, b * nb + jnp.minimum(i, nb - 1))),
                  pl.BlockSpec((seq, ATT_W), lambda b, i: (b, 1)),
                  pl.BlockSpec((ATT_W, seq), lambda b, i: (1, b)),
                  pl.BlockSpec((seq, SMALL_W), lambda b, i: (b, 0)),
                  pl.BlockSpec((None, 1, SMALL_W), lambda b, i: (layer, 0, 0))],
        out_specs=pl.BlockSpec((KV_BLOCK, ATT_W), lambda b, i: (b * nb + jnp.maximum(i - 1, 0), 0)),
        scratch_shapes=[pltpu.VMEM((seq, ATT_HEADS * PAIR_W), BF16),
                        pltpu.VMEM((nb // WIN_BLOCKS, ATT_HEADS, KV_BLOCK), F32),
                        pltpu.VMEM((nb // WIN_BLOCKS, ATT_HEADS, KV_BLOCK), F32),
                        pltpu.VMEM((ATT_HEADS, KV_BLOCK), F32),
                        pltpu.SMEM((ATT_HEADS,), jnp.int32)]
                       + _flash_scratch(),
        compiler_params=pltpu.CompilerParams(dimension_semantics=("arbitrary", "arbitrary"),
                                             vmem_limit_bytes=VMEM_LIMIT),
        name="fox",
    )(qt, k, vt, sm, W["fox_fb"])


def _ssd_kernel(xbc_ref, z_ref, sm_ref, cw_ref, cb_ref, dtb_ref, alog_ref, dskip_ref, nw_ref, o_ref,
                ext_sc, state_sc, y_sc):
    c = pl.program_id(1)
    L = SSD_CHUNK
    tail = 8

    @pl.when(c == 0)
    def _():
        ext_sc[0:tail, :] = jnp.zeros((tail, SSD_XBC), F32)
        state_sc[...] = jnp.zeros(state_sc.shape, F32)

    ext_sc[tail:tail + L, :] = xbc_ref[...]
    u = cb_ref[...]
    for k in range(SSD_CONV):
        back = SSD_CONV - 1 - k
        u = u + cw_ref[k:k + 1, :] * ext_sc[tail - back:tail - back + L, :]
    ext_sc[0:tail, :] = ext_sc[L:L + tail, :]
    xc = _silu(u)

    dt = _softplus(sm_ref[...] + dtb_ref[...])
    a = dt * (-jnp.exp(alog_ref[...]))
    r = lax.broadcasted_iota(jnp.int32, (L, L), 0)
    cidx = lax.broadcasted_iota(jnp.int32, (L, L), 1)
    causal = cidx <= r
    tril = causal.astype(BF16)
    a_hi, a_mid, a_lo = _split3(a)
    acs = (_dot(tril, a_hi) + _dot(tril, a_mid) + _dot(tril, a_lo)) * LOG2E
    acs_t = acs.T

    widen_m = (lax.broadcasted_iota(jnp.int32, (SMALL_W, SSD_INNER), 0)
               == lax.broadcasted_iota(jnp.int32, (SMALL_W, SSD_INNER), 1) // SSD_P).astype(BF16)

    def widen(v):
        hi, mid, lo = _split3(v)
        return _dot(hi, widen_m) + _dot(mid, widen_m) + _dot(lo, widen_m)

    total = acs[L - 1:L, :]
    xs = xc[:, 0:SSD_INNER]
    xh = xs * widen(dt)
    x_end = (xh * widen(jnp.exp2(total - acs))).astype(BF16)
    from_start = widen(jnp.exp2(acs))
    keep = widen(jnp.broadcast_to(jnp.exp2(total), (8, SMALL_W)))[0:1, :]
    lane = lax.broadcasted_iota(jnp.int32, (L, PAIR_W), 1)
    heads_per_group = SSD_HEADS // SSD_GROUPS
    gw = heads_per_group * SSD_P

    for g in range(SSD_GROUPS):
        bg = xc[:, SSD_INNER + g * SSD_N:SSD_INNER + (g + 1) * SSD_N].astype(BF16)
        cg = xc[:, SSD_INNER + (SSD_GROUPS + g) * SSD_N:SSD_INNER + (SSD_GROUPS + g + 1) * SSD_N].astype(BF16)
        bgt = xc[:, SSD_INNER + g * SSD_N:SSD_INNER + (g + 1) * SSD_N].T.astype(BF16)
        cb = jnp.where(causal, _dot_nt(cg, bg), 0.0)
        gcols = slice(g * gw, (g + 1) * gw)
        st = state_sc[:, gcols]
        y_off = from_start[:, gcols] * _dot(cg, st.astype(BF16))
        state_sc[:, gcols] = keep[:, gcols] * st + _dot(bgt, x_end[:, gcols])
        for pair in range(heads_per_group // 2):
            h0 = g * heads_per_group + 2 * pair
            pcols = slice(h0 * SSD_P, (h0 + 2) * SSD_P)
            xp = xh[:, pcols].astype(BF16)
            y_pair = []
            for h in (h0, h0 + 1):
                decay = jnp.exp2(jnp.minimum(acs[:, h:h + 1] - acs_t[h:h + 1, :], 0.0))
                y_pair.append(_dot((cb * decay).astype(BF16), xp))
            y_diag = jnp.where(lane < SSD_P, y_pair[0], y_pair[1])
            y_sc[:, pcols] = (y_diag + y_off[:, pair * PAIR_W:(pair + 1) * PAIR_W]
                              + dskip_ref[:, pcols] * xs[:, pcols])

    y = y_sc[...] * _silu(z_ref[...])
    o_ref[...] = _rms(y, nw_ref[...]).astype(o_ref.dtype)


def _ssd(xbc, z, sm, W, layer, batch, seq):
    nc = seq // SSD_CHUNK
    L = SSD_CHUNK
    row = lambda b, c: (b * nc + c, 0)
    wspec = lambda a: pl.BlockSpec((None,) + a.shape[1:], lambda b, c: (layer, 0, 0))
    ws = tuple(W[n] for n in ("conv_w", "conv_b", "dt_bias", "a_log", "d_skip", "ssd_norm_w"))
    return pl.pallas_call(
        _ssd_kernel,
        out_shape=jax.ShapeDtypeStruct((batch * seq, SSD_INNER), BF16),
        grid=(batch, nc),
        in_specs=[pl.BlockSpec((L, SSD_XBC), row), pl.BlockSpec((L, SSD_INNER), row),
                  pl.BlockSpec((L, SMALL_W), row)] + [wspec(a) for a in ws],
        out_specs=pl.BlockSpec((L, SSD_INNER), row),
        scratch_shapes=[pltpu.VMEM((L + 8, SSD_XBC), F32),
                        pltpu.VMEM((SSD_N, SSD_INNER), F32),
                        pltpu.VMEM((L, SSD_INNER), F32)],
        compiler_params=pltpu.CompilerParams(dimension_semantics=("arbitrary", "arbitrary"),
                                             vmem_limit_bytes=VMEM_LIMIT),
        name="ssd",
    )(xbc, z, sm, *ws)


def _post_kernel(x_ref, a_ref, b_ref, c_ref, p_ref, wo_ref, gm_ref, w1_ref, w2_ref, gp_ref, wg_ref, wp_ref,
                 gf_ref, o_ref, acc_sc, h_sc, *, final):
    j = pl.program_id(1)

    @pl.when(j == 0)
    def _():
        mixed = jnp.concatenate([a_ref[...], b_ref[...], c_ref[...]], axis=1)
        x1 = x_ref[...] + _dot(mixed, wo_ref[...])
        acc_sc[...] = x1
        h_sc[...] = _rms(x1, gm_ref[...]).astype(BF16)

    u = jnp.maximum(_dot(h_sc[...], w1_ref[...]), 0.0)
    acc_sc[...] += _dot((u * u).astype(BF16), w2_ref[...])

    @pl.when(j == pl.num_programs(1) - 1)
    def _():
        x2 = acc_sc[...]
        gate = 1.0 / (1.0 + jnp.exp(-_dot(_rms(x2, gp_ref[...]).astype(BF16), wg_ref[...])))
        x3 = x2 + gate * _dot(p_ref[...].astype(BF16), wp_ref[...])
        o_ref[...] = _rms(x3, gf_ref[...]) if final else x3


def _post(x, a, b, c, W, layer, final):
    t, d = x.shape
    tm = POST_ROW_TILE
    d_ff = W["w_ff1"].shape[2]
    row = lambda i, j: (i, 0)
    wspec = lambda n: pl.BlockSpec((None,) + W[n].shape[1:], lambda i, j: (layer, 0, 0))
    return pl.pallas_call(
        functools.partial(_post_kernel, final=final),
        out_shape=jax.ShapeDtypeStruct((t, d), F32),
        grid=(t // tm, d_ff // FF_CHUNK),
        in_specs=[pl.BlockSpec((tm, d), row), pl.BlockSpec((tm, ATT_W), row), pl.BlockSpec((tm, SSD_INNER), row),
                  pl.BlockSpec((tm, ATT_W), row), pl.BlockSpec((None, tm, PLE_DIM), lambda i, j: (layer, i, 0)),
                  wspec("w_out"), wspec("g_mlp"),
                  pl.BlockSpec((None, d, FF_CHUNK), lambda i, j: (layer, 0, j)),
                  pl.BlockSpec((None, FF_CHUNK, d), lambda i, j: (layer, j, 0)),
                  wspec("g_ple"), wspec("w_ple_gate"), wspec("w_ple_proj"),
                  pl.BlockSpec(W["g_final"].shape, lambda i, j: (0, 0))],
        out_specs=pl.BlockSpec((tm, d), row),
        scratch_shapes=[pltpu.VMEM((tm, d), F32), pltpu.VMEM((tm, d), BF16)],
        compiler_params=pltpu.CompilerParams(dimension_semantics=("arbitrary", "arbitrary"),
                                             vmem_limit_bytes=VMEM_LIMIT),
        name="post",
    )(x, a, b, c, W["p"], W["w_out"], W["g_mlp"], W["w_ff1"], W["w_ff2"], W["g_ple"], W["w_ple_gate"],
      W["w_ple_proj"], W["g_final"])


def _prepare(p, g_mix, w_in, conv_w, conv_b, dt_bias, a_log, d_skip, ssd_norm_w, fox_f_bias, w_out,
             g_mlp, w_ff1, w_ff2, g_ple, w_ple_gate, w_ple_proj, g_final):
    depth, d, _ = w_in.shape
    o = 0
    mq, mk, mv = (w_in[:, :, o + k * ATT_W:o + (k + 1) * ATT_W] for k in range(3)); o += 3 * ATT_W
    wz = w_in[:, :, o:o + SSD_INNER]; o += SSD_INNER
    wxbc = w_in[:, :, o:o + SSD_XBC]; o += SSD_XBC
    wdt = w_in[:, :, o:o + SSD_HEADS]; o += SSD_HEADS
    fq, fk, fv = (w_in[:, :, o + k * ATT_W:o + (k + 1) * ATT_W] for k in range(3)); o += 3 * ATT_W
    wf = w_in[:, :, o:o + ATT_HEADS]
    order = jnp.argsort(-fox_f_bias, axis=1)

    def by_head(a, axis):
        shape = a.shape
        a = a.reshape(shape[:axis] + (ATT_HEADS, shape[axis] // ATT_HEADS) + shape[axis + 1:])
        idx = order.reshape((depth,) + (1,) * (axis - 1) + (ATT_HEADS,) + (1,) * (a.ndim - axis - 1))
        return jnp.take_along_axis(a, idx, axis=axis).reshape(shape)

    fq, fk, fv, wf = by_head(fq, 2), by_head(fk, 2), by_head(fv, 2), by_head(wf, 2)
    fox_f_bias = jnp.take_along_axis(fox_f_bias, order, axis=1)
    w_out = jnp.concatenate([w_out[:, :ATT_W + SSD_INNER], by_head(w_out[:, ATT_W + SSD_INNER:], 1)], axis=1)
    src = [SSD_HEADS + ATT_HEADS] * SMALL_W
    for j in range(SSD_HEADS):
        src[j] = j
    for h in range(ATT_HEADS):
        for k in range(3):
            src[_f_lane(h) + k] = SSD_HEADS + h
    src = jnp.asarray(src, jnp.int32)
    small = jnp.take(jnp.concatenate([wdt, wf, jnp.zeros_like(wf[:, :, :1])], axis=2), src, axis=2)
    fb = jnp.take(jnp.concatenate([jnp.zeros((depth, SSD_HEADS), F32), fox_f_bias, jnp.zeros((depth, 1), F32)],
                                  axis=1), src, axis=1)
    w_all = jnp.concatenate([mq, fq, mv, fv, mk, fk, wz, wxbc, small], axis=2).astype(BF16)
    assert w_all.shape[2] == IN_COLS["sm"][0] + SMALL_W
    row = lambda a: a[:, None, :]
    pad = lambda a: jnp.pad(a, ((0, 0), (0, SMALL_W - a.shape[1])))
    return {
        "w_in": w_all, "g_mix": row(g_mix),
        "conv_w": conv_w, "conv_b": row(conv_b), "dt_bias": row(pad(dt_bias)), "a_log": row(pad(a_log)),
        "d_skip": row(jnp.repeat(d_skip, SSD_P, axis=1)), "ssd_norm_w": row(ssd_norm_w),
        "fox_fb": row(fb),
        "p": p.reshape(depth, -1, p.shape[-1]),
        "w_out": w_out.astype(BF16), "g_mlp": row(g_mlp), "w_ff1": w_ff1.astype(BF16), "w_ff2": w_ff2.astype(BF16),
        "g_ple": row(g_ple), "w_ple_gate": w_ple_gate.astype(BF16), "w_ple_proj": w_ple_proj.astype(BF16),
        "g_final": g_final[None, :],
    }


def kernel(x, p, g_mix, w_in, conv_w, conv_b, dt_bias, a_log, d_skip, ssd_norm_w, fox_f_bias, w_out, g_mlp, w_ff1, w_ff2, g_ple, w_ple_gate, w_ple_proj, g_final):
    batch, seq, d = x.shape
    depth = w_in.shape[0]
    assert seq % KEY_WIN == 0 and seq % SSD_CHUNK == 0
    assert (batch * seq) % ROW_TILE == 0 and (batch * seq) % POST_ROW_TILE == 0 and w_ff1.shape[2] % FF_CHUNK == 0
    assert seq // KV_BLOCK <= AUG_POS, "one-hot block ids use the augmentation slots below AUG_POS"
    xt = x.reshape(batch * seq, d)
    W = _prepare(p, g_mix, w_in, conv_w, conv_b, dt_bias, a_log, d_skip, ssd_norm_w, fox_f_bias, w_out,
                 g_mlp, w_ff1, w_ff2, g_ple, w_ple_gate, w_ple_proj, g_final)
    for i in range(depth):
        qt, k, vt, z, xbc, sm = _inproj(xt, W, i)
        out_a = _moba(qt, k, vt, batch, seq)
        out_b = _ssd(xbc, z, sm, W, i, batch, seq)
        out_c = _fox(qt, k, vt, sm, W, i, batch, seq)
        xt = _post(xt, out_a, out_b, out_c, W, i, final=(i == depth - 1))
    return xt.reshape(batch, seq, d)
```

```python
import functools

import jax
import jax.numpy as jnp
from jax import lax
from jax.experimental import pallas as pl
from jax.experimental.pallas import tpu as pltpu

F32 = jnp.float32
BF16 = jnp.bfloat16

HEAD_DIM = 64
ATT_HEADS = 4
ATT_W = ATT_HEADS * HEAD_DIM
PAIR_W = 2 * HEAD_DIM
KV_BLOCK = 256
WIN_BLOCKS = 2
KEY_WIN = WIN_BLOCKS * KV_BLOCK
SOFTMAX_ROWS = 64
MOBA_TOPK = 3
AUG_POS = 32
ACC_ROWS = HEAD_DIM + 16
LOG2E = 1.4426950408889634
UNDERFLOW = 152.0
BOUND_SLACK = 1.01
SSD_HEADS = 8
SSD_P = 64
SSD_INNER = SSD_HEADS * SSD_P
SSD_GROUPS = 2
SSD_N = 128
SSD_CONV = 4
SSD_CHUNK = 256
SSD_XBC = SSD_INNER + 2 * SSD_GROUPS * SSD_N
SMALL_W = 128
IN_COLS = {"qv": (0, 4 * ATT_W), "k": (4 * ATT_W, 2 * ATT_W), "z": (6 * ATT_W, SSD_INNER),
           "xbc": (6 * ATT_W + SSD_INNER, SSD_XBC), "sm": (6 * ATT_W + SSD_INNER + SSD_XBC, SMALL_W)}
PLE_DIM = 256
RMS_EPS = 1e-6
NEG = -1e30
VMEM_LIMIT = 56 * 1024 * 1024

ROW_TILE = 1024
POST_ROW_TILE = 1024
FF_CHUNK = 1024


def _rms(x, g):
    return x * lax.rsqrt(jnp.mean(x * x, axis=-1, keepdims=True) + RMS_EPS) * g


def _dot(a, b):
    return jnp.dot(a, b, preferred_element_type=F32)


def _dot_nt(a, b):
    return lax.dot_general(a, b, (((1,), (1,)), ((), ())), preferred_element_type=F32)


def _split3(a):
    hi = a.astype(BF16)
    r1 = a - hi.astype(F32)
    mid = r1.astype(BF16)
    lo = (r1 - mid.astype(F32)).astype(BF16)
    return hi, mid, lo


def _softplus(x):
    return jnp.maximum(x, 0.0) + jnp.log(1.0 + jnp.exp(-jnp.abs(x)))


def _silu(x):
    return x * (1.0 / (1.0 + jnp.exp(-x)))


def _inproj_kernel(x_ref, g_ref, wqv_ref, wk_ref, wz_ref, wxbc_ref, wsm_ref,
                   qt_ref, k_ref, vt_ref, z_ref, xbc_ref, sm_ref, wqvt_sc):
    @pl.when(pl.program_id(0) == 0)
    def _():
        wqvt_sc[...] = wqv_ref[...].astype(F32).T.astype(BF16)

    h = _rms(x_ref[...], g_ref[...]).astype(BF16)
    qt_ref[...] = (_dot_nt(wqvt_sc[0:2 * ATT_W, :], h) * (HEAD_DIM ** -0.5 * LOG2E)).astype(BF16)
    k_ref[...] = _dot(h, wk_ref[...]).astype(BF16)
    vt_ref[...] = _dot_nt(wqvt_sc[2 * ATT_W:, :], h).astype(BF16)
    z_ref[...] = _dot(h, wz_ref[...])
    xbc_ref[...] = _dot(h, wxbc_ref[...])
    sm_ref[...] = _dot(h, wsm_ref[...])


def _inproj(x, W, layer):
    t, d = x.shape
    tm = ROW_TILE
    row = lambda i: (i, 0)
    col = lambda i: (0, i)

    def wspec(name):
        start, width = IN_COLS[name]
        assert start % width == 0
        return pl.BlockSpec((None, d, width), lambda i: (layer, 0, start // width))
    outs = (
        jax.ShapeDtypeStruct((2 * ATT_W, t), BF16),
        jax.ShapeDtypeStruct((t, 2 * ATT_W), BF16),
        jax.ShapeDtypeStruct((2 * ATT_W, t), BF16),
        jax.ShapeDtypeStruct((t, SSD_INNER), F32),
        jax.ShapeDtypeStruct((t, SSD_XBC), F32),
        jax.ShapeDtypeStruct((t, SMALL_W), F32),
    )
    names = ("qv", "k", "z", "xbc", "sm")
    return pl.pallas_call(
        _inproj_kernel,
        out_shape=outs,
        grid=(t // tm,),
        in_specs=[pl.BlockSpec((tm, d), row), pl.BlockSpec((None, 1, d), lambda i: (layer, 0, 0))]
                 + [wspec(n) for n in names],
        out_specs=(pl.BlockSpec((2 * ATT_W, tm), col), pl.BlockSpec((tm, 2 * ATT_W), row),
                   pl.BlockSpec((2 * ATT_W, tm), col), pl.BlockSpec((tm, SSD_INNER), row),
                   pl.BlockSpec((tm, SSD_XBC), row), pl.BlockSpec((tm, SMALL_W), row)),
        scratch_shapes=[pltpu.VMEM((4 * ATT_W, d), BF16)],
        compiler_params=pltpu.CompilerParams(dimension_semantics=("arbitrary",), vmem_limit_bytes=VMEM_LIMIT),
        name="inproj",
    )(x, W["g_mix"], *([W["w_in"]] * len(names)))


class _Flash:
    def __init__(self, kaug_sc, qaug_sc, vt_ref, shift_of, s0_sc, s_sc, p_sc, acc_sc, m_sc, alpha_sc,
                 heads=range(ATT_HEADS)):
        self.kaug_sc, self.qaug_sc, self.vt_ref, self.shift_of = kaug_sc, qaug_sc, vt_ref, shift_of
        self.s0_sc, self.s_sc, self.p_sc, self.acc_sc = s0_sc, s_sc, p_sc, acc_sc
        self.m_sc, self.alpha_sc = m_sc, alpha_sc
        self.heads = tuple(heads)
        self.ones = jnp.ones((ACC_ROWS - HEAD_DIM, KEY_WIN), BF16)

    def reset(self):
        self.s_sc[...] = jnp.zeros(self.s_sc.shape, F32)
        self.p_sc[...] = jnp.zeros(self.p_sc.shape, BF16)
        self.acc_sc[...] = jnp.ones(self.acc_sc.shape, F32)
        self.m_sc[...] = jnp.zeros(self.m_sc.shape, F32)
        self.alpha_sc[...] = jnp.ones(self.alpha_sc.shape, F32)

    def scores(self, h, win):
        c0 = pl.multiple_of(win * KEY_WIN, KEY_WIN)
        return _dot(self.kaug_sc[pl.ds(c0, KEY_WIN), h * PAIR_W:(h + 1) * PAIR_W], self.qaug_sc[h])

    @staticmethod
    def plan(tile, first_win, live=True):
        own_win = tile // WIN_BLOCKS
        n_past = jnp.where(live, own_win - first_win, 0)
        dead = jnp.full((1, KV_BLOCK), n_past, jnp.int32) == 0
        return own_win, first_win, jnp.maximum(n_past, 1), jnp.where(dead, 1.0, 0.0)

    @staticmethod
    def tile_win(t, plan):
        own_win, first_win, _, _ = plan
        return jnp.where(t < 1, own_win, first_win + t - 1)

    def pv_stage(self, t, plan, alpha):
        c0 = pl.multiple_of(self.tile_win(t, plan) * KEY_WIN, KEY_WIN)
        for h in self.heads:
            vt_h = jnp.concatenate([self.vt_ref[h * HEAD_DIM:(h + 1) * HEAD_DIM, pl.ds(c0, KEY_WIN)], self.ones],
                                   axis=0)
            self.acc_sc[h] = alpha[h] * self.acc_sc[h] + _dot(vt_h, self.p_sc[h])

    def softmax_stage(self, t, tile, plan, m, src, refill):
        m_out, alpha = {}, {}
        for h in self.heads:
            chunks = [pl.ds(r, SOFTMAX_ROWS) for r in range(0, KEY_WIN, SOFTMAX_ROWS)]
            top = src[h, chunks[0], :]
            for rows in chunks[1:]:
                top = jnp.maximum(top, src[h, rows, :])
            shift = self.shift_of(h, self.tile_win(t, plan), tile)
            if src is self.s_sc:
                shift = shift + plan[3] * NEG
            m_new = jnp.maximum(m[h], jnp.max(top, axis=0, keepdims=True) + shift)
            base = m_new - shift
            for rows in chunks:
                self.p_sc[h, rows, :] = jnp.exp2(src[h, rows, :] - base).astype(BF16)
            if refill:
                self.s_sc[h] = self.scores(h, self.tile_win(t + 1, plan))
            m_out[h] = m_new
            alpha[h] = jnp.exp2(m[h] - m_new)
        return m_out, alpha

    def own_scores(self, tile):
        own_win = tile // WIN_BLOCKS
        key = lax.broadcasted_iota(jnp.int32, (KEY_WIN, KV_BLOCK), 0) + (own_win * KEY_WIN - tile * KV_BLOCK)
        causal = key <= lax.broadcasted_iota(jnp.int32, (KEY_WIN, KV_BLOCK), 1)
        for h in self.heads:
            self.s0_sc[h] = jnp.where(causal, self.scores(h, own_win), NEG)

    def finish(self, tile, first_win):
        plan = self.plan(tile, first_win)
        n = plan[2]
        m = {h: self.m_sc[h] for h in self.heads}
        alpha = {h: self.alpha_sc[h] for h in self.heads}
        self.pv_stage(n - 1, plan, alpha)
        m, alpha = self.softmax_stage(n, tile, plan, m, self.s_sc, refill=False)
        self.pv_stage(n, plan, alpha)

    def save(self, carry):
        m, alpha = carry
        for h in self.heads:
            self.m_sc[h] = m[h]
            self.alpha_sc[h] = alpha[h]

    def first_softmax(self, tile, plan):
        start = {h: jnp.full((1, KV_BLOCK), NEG, F32) for h in self.heads}
        return self.softmax_stage(0, tile, plan, start, self.s0_sc, refill=True)

    def step(self, t, tile, plan, carry):
        m, alpha = carry
        self.pv_stage(t - 1, plan, alpha)
        return self.softmax_stage(t, tile, plan, m, self.s_sc, refill=True)


def _flash_output(o_ref, acc_sc):
    out_t = jnp.concatenate([acc_sc[h][0:HEAD_DIM, :] / acc_sc[h][HEAD_DIM:HEAD_DIM + 1, :]
                             for h in range(ATT_HEADS)], axis=0)
    o_ref[...] = out_t.T.astype(o_ref.dtype)


def _flash_groups(pipes, firsts, first_sc, tile, step, nb, o_ref, acc_sc):
    prev = jnp.maximum(step - 1, 0)
    for g, pipe in enumerate(pipes):
        pipe.finish(prev, first_sc[g])
    _flash_output(o_ref, acc_sc)

    plans = [pipe.plan(tile, first, step < nb) for pipe, first in zip(pipes, firsts)]
    for pipe in pipes:
        pipe.own_scores(tile)
    for h in range(ATT_HEADS):
        acc_sc[h] = jnp.zeros(acc_sc.shape[1:], F32)
    carries = [pipe.first_softmax(tile, plan) for pipe, plan in zip(pipes, plans)]
    lag = [plans[0][2] - plan[2] for plan in plans]
    for k in range(len(pipes)):
        lo = lag[k] + 1
        hi = lag[k + 1] + 1 if k + 1 < len(pipes) else plans[0][2]

        def phase(t, active, k=k):
            return [pipes[g].step(t - lag[g], tile, plans[g], active[g]) for g in range(k + 1)]

        carries[:k + 1] = lax.fori_loop(lo, hi, phase, carries[:k + 1])
    for g, pipe in enumerate(pipes):
        pipe.save(carries[g])
        first_sc[g] = firsts[g]


def _flash_scratch():
    return [pltpu.VMEM((ATT_HEADS, PAIR_W, KV_BLOCK), BF16),
            pltpu.VMEM((ATT_HEADS, KEY_WIN, KV_BLOCK), F32),
            pltpu.VMEM((ATT_HEADS, KEY_WIN, KV_BLOCK), F32),
            pltpu.VMEM((ATT_HEADS, KEY_WIN, KV_BLOCK), BF16),
            pltpu.VMEM((ATT_HEADS, ACC_ROWS, KV_BLOCK), F32),
            pltpu.VMEM((ATT_HEADS, 1, KV_BLOCK), F32),
            pltpu.VMEM((ATT_HEADS, 1, KV_BLOCK), F32)]


def _f_lane(h):
    return (HEAD_DIM if h % 2 == 0 else 0) + AUG_POS + 3 * (h // 2)


def _own_lanes(h, shape):
    lane = lax.broadcasted_iota(jnp.int32, shape, 1)
    if h % 2 == 0:
        return lane < HEAD_DIM, lane - HEAD_DIM
    return lane >= HEAD_DIM, lane


def _stack_q(h, q_h, aug_rows):
    return jnp.concatenate([q_h, aug_rows] if h % 2 == 0 else [aug_rows, q_h], axis=0)


def _moba_kernel(qt_ref, k_ref, vt_ref, o_ref, kaug_sc, km_sc, knorm_sc, first_sc, qaug_sc, s0_sc, s_sc, p_sc, acc_sc,
                 m_sc, alpha_sc, *, nb):
    step = pl.program_id(1)
    i = jnp.minimum(step, nb - 1)
    slopes = [LOG2E * 2.0 ** (-8.0 * (h + 1) / ATT_HEADS) for h in range(ATT_HEADS)]

    def shift_of(h, win, tile):
        return (slopes[h] * KEY_WIN) * jnp.full((1, KV_BLOCK), win - tile // WIN_BLOCKS, jnp.int32).astype(F32)

    groups = ((2, 3), (1,), (0,))
    near_heads = tuple(h for g in groups[1:] for h in g)
    scratch = (kaug_sc, qaug_sc, vt_ref, shift_of, s0_sc, s_sc, p_sc, acc_sc, m_sc, alpha_sc)
    pipes = [_Flash(*scratch, heads=g) for g in groups]

    @pl.when(step == 0)
    def _():
        pipes[0].reset()
        for g in range(len(groups)):
            first_sc[g] = 0
        knorm_sc[...] = jnp.zeros(knorm_sc.shape, F32)
        km_sc[...] = jnp.zeros(km_sc.shape, F32)
        offs = lax.broadcasted_iota(jnp.int32, (KV_BLOCK, PAIR_W), 0).astype(F32)
        km_row = lax.broadcasted_iota(jnp.int32, (HEAD_DIM, PAIR_W), 0)

        def build(n, carry):
            c0 = pl.multiple_of(n * KV_BLOCK, KV_BLOCK)
            in_win = jnp.full((KV_BLOCK, PAIR_W), n % WIN_BLOCKS, jnp.int32).astype(F32)
            for j in range(ATT_HEADS // 2):
                kp = k_ref[pl.ds(c0, KV_BLOCK), j * PAIR_W:(j + 1) * PAIR_W]
                mean = jnp.sum(kp.astype(F32), axis=0, keepdims=True) * (1.0 / KV_BLOCK)
                for h in (2 * j, 2 * j + 1):
                    own, slot = _own_lanes(h, (KV_BLOCK, PAIR_W))
                    aug = jnp.where(slot == n, 1.0,
                                    jnp.where((slot >= AUG_POS) & (slot < AUG_POS + 3), offs,
                                              jnp.where((slot >= AUG_POS + 3) & (slot < AUG_POS + 6), in_win, 0.0)))
                    kaug_sc[pl.ds(c0, KV_BLOCK), h * PAIR_W:(h + 1) * PAIR_W] = jnp.where(own, kp, aug.astype(BF16))
                    own_km, _ = _own_lanes(h, (HEAD_DIM, PAIR_W))
                    km_sc[h] = jnp.where(km_row == n, jnp.where(own_km, mean, 0.0), km_sc[h])
                    if h in near_heads:
                        kf = jnp.where(own, kp.astype(F32), 0.0)
                        sq = jnp.max(jnp.sum(kf * kf, axis=1, keepdims=True), axis=0, keepdims=True)
                        knorm_sc[h:h + 1, :] = jnp.maximum(knorm_sc[h:h + 1, :], sq)
            return carry

        lax.fori_loop(0, nb, build, 0)

    blk_id = lax.broadcasted_iota(jnp.int32, (HEAD_DIM, KV_BLOCK), 0)
    blk_idf = blk_id.astype(F32)
    for h in range(ATT_HEADS):
        j = h // 2
        qp = qt_ref[j * PAIR_W:(j + 1) * PAIR_W, :]
        km_hi, km_mid, km_lo = _split3(km_sc[h])
        score = jnp.where(blk_id < i, _dot(km_hi, qp) + _dot(km_mid, qp) + _dot(km_lo, qp), NEG)
        sel = blk_id == i
        for _ in range(MOBA_TOPK):
            top = jnp.max(score, axis=0, keepdims=True)
            first = jnp.min(jnp.where(score == top, blk_idf, 1e9), axis=0, keepdims=True)
            hit = blk_idf == first
            sel = sel | (hit & (top > 0.5 * NEG))
            score = jnp.where(hit, 3.0 * NEG, score)
        terms = [t.astype(F32) for t in _split3(jnp.full((1, 1), slopes[h], F32))]
        terms += [t * KV_BLOCK for t in terms]
        aug = jnp.where(blk_id < AUG_POS, jnp.where(sel, 0.0, NEG), 0.0)
        for k, term in enumerate(terms):
            aug = jnp.where(blk_id == AUG_POS + k, term, aug)
        qaug_sc[h] = _stack_q(h, qt_ref[h * HEAD_DIM:(h + 1) * HEAD_DIM, :], aug.astype(BF16))

    own_win = i // WIN_BLOCKS
    firsts = [0]
    for g in groups[1:]:
        count = jnp.zeros((1, KV_BLOCK), F32)
        reach = {}
        for h in g:
            q2 = jnp.max(jnp.sum(jnp.square(qt_ref[h * HEAD_DIM:(h + 1) * HEAD_DIM, :].astype(F32)), axis=0,
                                 keepdims=True), axis=1, keepdims=True)
            reach[h] = (2.0 * BOUND_SLACK) * jnp.sqrt(q2 * knorm_sc[h:h + 1, :]) + UNDERFLOW
        for w in range(nb // WIN_BLOCKS):
            gap = jnp.full((1, KV_BLOCK), own_win - 1 - w, jnp.int32).astype(F32) * KEY_WIN
            matters = functools.reduce(jnp.maximum, [jnp.where(gap * slopes[h] < reach[h], 1.0, 0.0) for h in g])
            count = count + jnp.where(w < own_win, matters, 0.0)
        firsts.append(jnp.maximum(own_win - count[0, 0].astype(jnp.int32), firsts[-1]))

    _flash_groups(pipes, firsts, first_sc, i, step, nb, o_ref, acc_sc)


def _moba(qt, k, vt, batch, seq):
    nb = seq // KV_BLOCK
    return pl.pallas_call(
        functools.partial(_moba_kernel, nb=nb),
        out_shape=jax.ShapeDtypeStruct((batch * seq, ATT_W), BF16),
        grid=(batch, nb + 1),
        in_specs=[pl.BlockSpec((ATT_W, KV_BLOCK), lambda b, i: (0, b * nb + jnp.minimum(i, nb - 1))),
                  pl.BlockSpec((seq, ATT_W), lambda b, i: (b, 0)),
                  pl.BlockSpec((ATT_W, seq), lambda b, i: (0, b))],
        out_specs=pl.BlockSpec((KV_BLOCK, ATT_W), lambda b, i: (b * nb + jnp.maximum(i - 1, 0), 0)),
        scratch_shapes=[pltpu.VMEM((seq, ATT_HEADS * PAIR_W), BF16),
                        pltpu.VMEM((ATT_HEADS, HEAD_DIM, PAIR_W), F32),
                        pltpu.VMEM((ATT_HEADS, KV_BLOCK), F32),
                        pltpu.SMEM((ATT_HEADS,), jnp.int32)]
                       + _flash_scratch(),
        compiler_params=pltpu.CompilerParams(dimension_semantics=("arbitrary", "arbitrary"),
                                             vmem_limit_bytes=VMEM_LIMIT),
        name="moba",
    )(qt, k, vt)


def _fox_kernel(qt_ref, k_ref, vt_ref, sm_ref, fb_ref, o_ref, kaug_sc, cstart_sc, cend_sc, knorm_sc, first_sc,
                qaug_sc, s0_sc, s_sc, p_sc, acc_sc, m_sc, alpha_sc, *, nb):
    step = pl.program_id(1)
    i = jnp.minimum(step, nb - 1)

    def shift_of(h, win, tile):
        return -cstart_sc[win][h:h + 1, :]

    groups = ((0,), (1,), (2,), (3,))
    scratch = (kaug_sc, qaug_sc, vt_ref, shift_of, s0_sc, s_sc, p_sc, acc_sc, m_sc, alpha_sc)
    pipes = [_Flash(*scratch, heads=g) for g in groups]

    @pl.when(step == 0)
    def _():
        pipes[0].reset()
        for g in range(len(groups)):
            first_sc[g] = 0
        r = lax.broadcasted_iota(jnp.int32, (KV_BLOCK, KV_BLOCK), 0)
        c = lax.broadcasted_iota(jnp.int32, (KV_BLOCK, KV_BLOCK), 1)
        tril = (c <= r).astype(BF16)
        lane = lax.broadcasted_iota(jnp.int32, (KV_BLOCK, PAIR_W), 1)
        term = (lane % HEAD_DIM - AUG_POS) % 3

        def head_rows(v):
            return jnp.concatenate(
                [jnp.broadcast_to(v[0:1, _f_lane(h):_f_lane(h) + 1], (1, KV_BLOCK)) for h in range(ATT_HEADS)], axis=0)

        def scan(n, carry):
            c0 = pl.multiple_of(n * KV_BLOCK, KV_BLOCK)
            first = jnp.full((8, SMALL_W), n % WIN_BLOCKS, jnp.int32) == 0
            blk_start, win_start = carry[0], jnp.where(first, carry[0], carry[1])
            logf = -LOG2E * _softplus(-(sm_ref[pl.ds(c0, KV_BLOCK), :] + fb_ref[...]))
            hi, mid, lo = _split3(logf)
            cs = _dot(tril, hi) + _dot(tril, mid) + _dot(tril, lo)
            in_win = cs + (blk_start - win_start)[0:1, :]
            t_hi, t_mid, t_lo = (t.astype(F32) for t in _split3(in_win))
            terms = jnp.where(term == 0, t_hi, jnp.where(term == 1, t_mid, t_lo))
            blk_end = blk_start + jnp.broadcast_to(cs[KV_BLOCK - 1:KV_BLOCK, :], (8, SMALL_W))
            cstart_sc[n // WIN_BLOCKS] = head_rows(win_start)
            cend_sc[n // WIN_BLOCKS] = head_rows(blk_end)
            knorm = list(carry[2:])
            for h in range(ATT_HEADS):
                j = h // 2
                own, _ = _own_lanes(h, (KV_BLOCK, PAIR_W))
                mine = (lane >= _f_lane(h)) & (lane < _f_lane(h) + 3)
                kp = k_ref[pl.ds(c0, KV_BLOCK), j * PAIR_W:(j + 1) * PAIR_W]
                kaug_sc[pl.ds(c0, KV_BLOCK), h * PAIR_W:(h + 1) * PAIR_W] = jnp.where(
                    own, kp, jnp.where(mine, terms, 0.0).astype(BF16))
                kf = jnp.where(own, kp.astype(F32), 0.0)
                sq = jnp.max(jnp.sum(kf * kf, axis=1, keepdims=True), axis=0, keepdims=True)
                knorm[h] = jnp.maximum(knorm[h], jnp.broadcast_to(sq, (8, SMALL_W)))
            return (blk_end, win_start, *knorm)

        zero = jnp.broadcast_to(fb_ref[...] * 0.0, (8, SMALL_W))
        done = lax.fori_loop(0, nb, scan, (zero,) * (2 + ATT_HEADS))
        knorm_sc[...] = jnp.concatenate(
            [jnp.concatenate([v[0:1, :]] * (KV_BLOCK // SMALL_W), axis=1) for v in done[2:]], axis=0)

    aug_id = lax.broadcasted_iota(jnp.int32, (HEAD_DIM, KV_BLOCK), 0)
    for h in range(ATT_HEADS):
        slot = _f_lane(h) % HEAD_DIM
        aug = jnp.where((aug_id >= slot) & (aug_id < slot + 3), -1.0, 0.0).astype(BF16)
        qaug_sc[h] = _stack_q(h, qt_ref[h * HEAD_DIM:(h + 1) * HEAD_DIM, :], aug)

    own_win = i // WIN_BLOCKS
    q2 = [jnp.max(jnp.sum(jnp.square(qt_ref[h * HEAD_DIM:(h + 1) * HEAD_DIM, :].astype(F32)), axis=0, keepdims=True),
                  axis=1, keepdims=True) for h in range(ATT_HEADS)]
    bound = jnp.sqrt(jnp.concatenate(q2, axis=0) * knorm_sc[...])
    reach = (2.0 * BOUND_SLACK) * bound + UNDERFLOW
    c_own = cstart_sc[own_win]
    counts = [jnp.zeros((1, KV_BLOCK), F32) for _ in groups]
    for w in range(nb // WIN_BLOCKS):
        matters = jnp.where(w < own_win, jnp.where(cend_sc[w] - c_own < reach, 1.0, 0.0), 0.0)
        for g, heads in enumerate(groups):
            counts[g] = counts[g] + functools.reduce(jnp.maximum, [matters[h:h + 1, :] for h in heads])
    firsts = []
    for g in range(len(groups)):
        widest = functools.reduce(jnp.maximum, counts[g:])
        firsts.append(own_win - widest[0, 0].astype(jnp.int32))

    _flash_groups(pipes, firsts, first_sc, i, step, nb, o_ref, acc_sc)


def _fox(qt, k, vt, sm, W, layer, batch, seq):
    nb = seq // KV_BLOCK
    return pl.pallas_call(
        functools.partial(_fox_kernel, nb=nb),
        out_shape=jax.ShapeDtypeStruct((batch * seq, ATT_W), BF16),
        grid=(batch, nb + 1),
        in_specs=[pl.BlockSpec((ATT_W, KV_BLOCK), lambda b, i: (1, b * nb + jnp.minimum(i, nb - 1))),
                  pl.BlockSpec((seq, ATT_W), lambda b, i: (b, 1)),
                  pl.BlockSpec((ATT_W, seq), lambda b, i: (1, b)),
                  pl.BlockSpec((seq, SMALL_W), lambda b, i: (b, 0)),
                  pl.BlockSpec((None, 1, SMALL_W), lambda b, i: (layer, 0, 0))],
        out_specs=pl.BlockSpec((KV_BLOCK, ATT_W), lambda b, i: (b * nb + jnp.maximum(i - 1, 0), 0)),
        scratch_shapes=[pltpu.VMEM((seq, ATT_HEADS * PAIR_W), BF16),
                        pltpu.VMEM((nb // WIN_BLOCKS, ATT_HEADS, KV_BLOCK), F32),
                        pltpu.VMEM((nb // WIN_BLOCKS, ATT_HEADS, KV_BLOCK), F32),
                        pltpu.VMEM((ATT_HEADS, KV_BLOCK), F32),
                        pltpu.SMEM((ATT_HEADS,), jnp.int32)]
                       + _flash_scratch(),
        compiler_params=pltpu.CompilerParams(dimension_semantics=("arbitrary", "arbitrary"),
                                             vmem_limit_bytes=VMEM_LIMIT),
        name="fox",
    )(qt, k, vt, sm, W["fox_fb"])


def _ssd_kernel(xbc_ref, z_ref, sm_ref, cw_ref, cb_ref, dtb_ref, alog_ref, dskip_ref, nw_ref, o_ref,
                ext_sc, state_sc, y_sc):
    c = pl.program_id(1)
    L = SSD_CHUNK
    tail = 8

    @pl.when(c == 0)
    def _():
        ext_sc[0:tail, :] = jnp.zeros((tail, SSD_XBC), F32)
        state_sc[...] = jnp.zeros(state_sc.shape, F32)

    ext_sc[tail:tail + L, :] = xbc_ref[...]
    u = cb_ref[...]
    for k in range(SSD_CONV):
        back = SSD_CONV - 1 - k
        u = u + cw_ref[k:k + 1, :] * ext_sc[tail - back:tail - back + L, :]
    ext_sc[0:tail, :] = ext_sc[L:L + tail, :]
    xc = _silu(u)

    dt = _softplus(sm_ref[...] + dtb_ref[...])
    a = dt * (-jnp.exp(alog_ref[...]))
    r = lax.broadcasted_iota(jnp.int32, (L, L), 0)
    cidx = lax.broadcasted_iota(jnp.int32, (L, L), 1)
    causal = cidx <= r
    tril = causal.astype(BF16)
    a_hi, a_mid, a_lo = _split3(a)
    acs = (_dot(tril, a_hi) + _dot(tril, a_mid) + _dot(tril, a_lo)) * LOG2E
    acs_t = acs.T

    widen_m = (lax.broadcasted_iota(jnp.int32, (SMALL_W, SSD_INNER), 0)
               == lax.broadcasted_iota(jnp.int32, (SMALL_W, SSD_INNER), 1) // SSD_P).astype(BF16)

    def widen(v):
        hi, mid, lo = _split3(v)
        return _dot(hi, widen_m) + _dot(mid, widen_m) + _dot(lo, widen_m)

    total = acs[L - 1:L, :]
    xs = xc[:, 0:SSD_INNER]
    xh = xs * widen(dt)
    x_end = (xh * widen(jnp.exp2(total - acs))).astype(BF16)
    from_start = widen(jnp.exp2(acs))
    keep = widen(jnp.broadcast_to(jnp.exp2(total), (8, SMALL_W)))[0:1, :]
    lane = lax.broadcasted_iota(jnp.int32, (L, PAIR_W), 1)
    heads_per_group = SSD_HEADS // SSD_GROUPS
    gw = heads_per_group * SSD_P

    for g in range(SSD_GROUPS):
        bg = xc[:, SSD_INNER + g * SSD_N:SSD_INNER + (g + 1) * SSD_N].astype(BF16)
        cg = xc[:, SSD_INNER + (SSD_GROUPS + g) * SSD_N:SSD_INNER + (SSD_GROUPS + g + 1) * SSD_N].astype(BF16)
        bgt = xc[:, SSD_INNER + g * SSD_N:SSD_INNER + (g + 1) * SSD_N].T.astype(BF16)
        cb = jnp.where(causal, _dot_nt(cg, bg), 0.0)
        gcols = slice(g * gw, (g + 1) * gw)
        st = state_sc[:, gcols]
        y_off = from_start[:, gcols] * _dot(cg, st.astype(BF16))
        state_sc[:, gcols] = keep[:, gcols] * st + _dot(bgt, x_end[:, gcols])
        for pair in range(heads_per_group // 2):
            h0 = g * heads_per_group + 2 * pair
            pcols = slice(h0 * SSD_P, (h0 + 2) * SSD_P)
            xp = xh[:, pcols].astype(BF16)
            y_pair = []
            for h in (h0, h0 + 1):
                decay = jnp.exp2(jnp.minimum(acs[:, h:h + 1] - acs_t[h:h + 1, :], 0.0))
                y_pair.append(_dot((cb * decay).astype(BF16), xp))
            y_diag = jnp.where(lane < SSD_P, y_pair[0], y_pair[1])
            y_sc[:, pcols] = (y_diag + y_off[:, pair * PAIR_W:(pair + 1) * PAIR_W]
                              + dskip_ref[:, pcols] * xs[:, pcols])

    y = y_sc[...] * _silu(z_ref[...])
    o_ref[...] = _rms(y, nw_ref[...]).astype(o_ref.dtype)


def _ssd(xbc, z, sm, W, layer, batch, seq):
    nc = seq // SSD_CHUNK
    L = SSD_CHUNK
    row = lambda b, c: (b * nc + c, 0)
    wspec = lambda a: pl.BlockSpec((None,) + a.shape[1:], lambda b, c: (layer, 0, 0))
    ws = tuple(W[n] for n in ("conv_w", "conv_b", "dt_bias", "a_log", "d_skip", "ssd_norm_w"))
    return pl.pallas_call(
        _ssd_kernel,
        out_shape=jax.ShapeDtypeStruct((batch * seq, SSD_INNER), BF16),
        grid=(batch, nc),
        in_specs=[pl.BlockSpec((L, SSD_XBC), row), pl.BlockSpec((L, SSD_INNER), row),
                  pl.BlockSpec((L, SMALL_W), row)] + [wspec(a) for a in ws],
        out_specs=pl.BlockSpec((L, SSD_INNER), row),
        scratch_shapes=[pltpu.VMEM((L + 8, SSD_XBC), F32),
                        pltpu.VMEM((SSD_N, SSD_INNER), F32),
                        pltpu.VMEM((L, SSD_INNER), F32)],
        compiler_params=pltpu.CompilerParams(dimension_semantics=("arbitrary", "arbitrary"),
                                             vmem_limit_bytes=VMEM_LIMIT),
        name="ssd",
    )(xbc, z, sm, *ws)


def _post_kernel(x_ref, a_ref, b_ref, c_ref, p_ref, wo_ref, gm_ref, w1_ref, w2_ref, gp_ref, wg_ref, wp_ref,
                 gf_ref, o_ref, acc_sc, h_sc, *, final):
    j = pl.program_id(1)

    @pl.when(j == 0)
    def _():
        mixed = jnp.concatenate([a_ref[...], b_ref[...], c_ref[...]], axis=1)
        x1 = x_ref[...] + _dot(mixed, wo_ref[...])
        acc_sc[...] = x1
        h_sc[...] = _rms(x1, gm_ref[...]).astype(BF16)

    u = jnp.maximum(_dot(h_sc[...], w1_ref[...]), 0.0)
    acc_sc[...] += _dot((u * u).astype(BF16), w2_ref[...])

    @pl.when(j == pl.num_programs(1) - 1)
    def _():
        x2 = acc_sc[...]
        gate = 1.0 / (1.0 + jnp.exp(-_dot(_rms(x2, gp_ref[...]).astype(BF16), wg_ref[...])))
        x3 = x2 + gate * _dot(p_ref[...].astype(BF16), wp_ref[...])
        o_ref[...] = _rms(x3, gf_ref[...]) if final else x3


def _post(x, a, b, c, W, layer, final):
    t, d = x.shape
    tm = POST_ROW_TILE
    d_ff = W["w_ff1"].shape[2]
    row = lambda i, j: (i, 0)
    wspec = lambda n: pl.BlockSpec((None,) + W[n].shape[1:], lambda i, j: (layer, 0, 0))
    return pl.pallas_call(
        functools.partial(_post_kernel, final=final),
        out_shape=jax.ShapeDtypeStruct((t, d), F32),
        grid=(t // tm, d_ff // FF_CHUNK),
        in_specs=[pl.BlockSpec((tm, d), row), pl.BlockSpec((tm, ATT_W), row), pl.BlockSpec((tm, SSD_INNER), row),
                  pl.BlockSpec((tm, ATT_W), row), pl.BlockSpec((None, tm, PLE_DIM), lambda i, j: (layer, i, 0)),
                  wspec("w_out"), wspec("g_mlp"),
                  pl.BlockSpec((None, d, FF_CHUNK), lambda i, j: (layer, 0, j)),
                  pl.BlockSpec((None, FF_CHUNK, d), lambda i, j: (layer, j, 0)),
                  wspec("g_ple"), wspec("w_ple_gate"), wspec("w_ple_proj"),
                  pl.BlockSpec(W["g_final"].shape, lambda i, j: (0, 0))],
        out_specs=pl.BlockSpec((tm, d), row),
        scratch_shapes=[pltpu.VMEM((tm, d), F32), pltpu.VMEM((tm, d), BF16)],
        compiler_params=pltpu.CompilerParams(dimension_semantics=("arbitrary", "arbitrary"),
                                             vmem_limit_bytes=VMEM_LIMIT,
                                             allow_input_fusion=[False] * 5 + [True, False, True, True, False, True, True,
                                                                               False]),
        name="post",
    )(x, a, b, c, W["p"], W["w_out"], W["g_mlp"], W["w_ff1"], W["w_ff2"], W["g_ple"], W["w_ple_gate"],
      W["w_ple_proj"], W["g_final"])


def _prepare(p, g_mix, w_in, conv_w, conv_b, dt_bias, a_log, d_skip, ssd_norm_w, fox_f_bias, w_out,
             g_mlp, w_ff1, w_ff2, g_ple, w_ple_gate, w_ple_proj, g_final):
    depth, d, _ = w_in.shape
    o = 0
    mq, mk, mv = (w_in[:, :, o + k * ATT_W:o + (k + 1) * ATT_W] for k in range(3)); o += 3 * ATT_W
    wz = w_in[:, :, o:o + SSD_INNER]; o += SSD_INNER
    wxbc = w_in[:, :, o:o + SSD_XBC]; o += SSD_XBC
    wdt = w_in[:, :, o:o + SSD_HEADS]; o += SSD_HEADS
    fq, fk, fv = (w_in[:, :, o + k * ATT_W:o + (k + 1) * ATT_W] for k in range(3)); o += 3 * ATT_W
    wf = w_in[:, :, o:o + ATT_HEADS]
    order = jnp.argsort(-fox_f_bias, axis=1)

    def by_head(a, axis):
        shape = a.shape
        a = a.reshape(shape[:axis] + (ATT_HEADS, shape[axis] // ATT_HEADS) + shape[axis + 1:])
        idx = order.reshape((depth,) + (1,) * (axis - 1) + (ATT_HEADS,) + (1,) * (a.ndim - axis - 1))
        return jnp.take_along_axis(a, idx, axis=axis).reshape(shape)

    fq, fk, fv, wf = by_head(fq, 2), by_head(fk, 2), by_head(fv, 2), by_head(wf, 2)
    fox_f_bias = jnp.take_along_axis(fox_f_bias, order, axis=1)
    w_out = jnp.concatenate([w_out[:, :ATT_W + SSD_INNER], by_head(w_out[:, ATT_W + SSD_INNER:], 1)], axis=1)
    src = [SSD_HEADS + ATT_HEADS] * SMALL_W
    for j in range(SSD_HEADS):
        src[j] = j
    for h in range(ATT_HEADS):
        for k in range(3):
            src[_f_lane(h) + k] = SSD_HEADS + h
    src = jnp.asarray(src, jnp.int32)
    small = jnp.take(jnp.concatenate([wdt, wf, jnp.zeros_like(wf[:, :, :1])], axis=2), src, axis=2)
    fb = jnp.take(jnp.concatenate([jnp.zeros((depth, SSD_HEADS), F32), fox_f_bias, jnp.zeros((depth, 1), F32)],
                                  axis=1), src, axis=1)
    w_all = jnp.concatenate([mq, fq, mv, fv, mk, fk, wz, wxbc, small], axis=2).astype(BF16)
    assert w_all.shape[2] == IN_COLS["sm"][0] + SMALL_W
    row = lambda a: a[:, None, :]
    pad = lambda a: jnp.pad(a, ((0, 0), (0, SMALL_W - a.shape[1])))
    return {
        "w_in": w_all, "g_mix": row(g_mix),
        "conv_w": conv_w, "conv_b": row(conv_b), "dt_bias": row(pad(dt_bias)), "a_log": row(pad(a_log)),
        "d_skip": row(jnp.repeat(d_skip, SSD_P, axis=1)), "ssd_norm_w": row(ssd_norm_w),
        "fox_fb": row(fb),
        "p": p.reshape(depth, -1, p.shape[-1]),
        "w_out": w_out.astype(BF16), "g_mlp": row(g_mlp), "w_ff1": w_ff1.astype(BF16), "w_ff2": w_ff2.astype(BF16),
        "g_ple": row(g_ple), "w_ple_gate": w_ple_gate.astype(BF16), "w_ple_proj": w_ple_proj.astype(BF16),
        "g_final": g_final[None, :],
    }


def kernel(x, p, g_mix, w_in, conv_w, conv_b, dt_bias, a_log, d_skip, ssd_norm_w, fox_f_bias, w_out, g_mlp, w_ff1, w_ff2, g_ple, w_ple_gate, w_ple_proj, g_final):
    batch, seq, d = x.shape
    depth = w_in.shape[0]
    assert seq % KEY_WIN == 0 and seq % SSD_CHUNK == 0
    assert (batch * seq) % ROW_TILE == 0 and (batch * seq) % POST_ROW_TILE == 0 and w_ff1.shape[2] % FF_CHUNK == 0
    assert seq // KV_BLOCK <= AUG_POS, "one-hot block ids use the augmentation slots below AUG_POS"
    xt = x.reshape(batch * seq, d)
    W = _prepare(p, g_mix, w_in, conv_w, conv_b, dt_bias, a_log, d_skip, ssd_norm_w, fox_f_bias, w_out,
                 g_mlp, w_ff1, w_ff2, g_ple, w_ple_gate, w_ple_proj, g_final)
    for i in range(depth):
        qt, k, vt, z, xbc, sm = _inproj(xt, W, i)
        out_a = _moba(qt, k, vt, batch, seq)
        out_b = _ssd(xbc, z, sm, W, i, batch, seq)
        out_c = _fox(qt, k, vt, sm, W, i, batch, seq)
        xt = _post(xt, out_a, out_b, out_c, W, i, final=(i == depth - 1))
    return xt.reshape(batch, seq, d)
```
